```python
import jax, jax.numpy as jnp
from jax import lax
import numpy as np

D_MODEL = 1024
BATCH = 2
SEQ = 8192
DEPTH = 2

N_A_LAYERS = DEPTH // 2
N_B_LAYERS = DEPTH - N_A_LAYERS
EPS = 1e-6

HG_EXPAND = 128
HG_HEADS = D_MODEL // HG_EXPAND
HG_DK = D_MODEL // HG_HEADS
HG_DV = D_MODEL // HG_HEADS
HG_CHUNK = 64

DIL_GROUPS = ((128, 1), (512, 4), (2048, 16))
N_GROUPS = len(DIL_GROUPS)
HEAD_DIM = 128
KV_PER_GROUP = 2
Q_PER_KV = 2
Q_WIDTH = N_GROUPS * KV_PER_GROUP * Q_PER_KV * HEAD_DIM
KV_WIDTH = N_GROUPS * KV_PER_GROUP * HEAD_DIM
ATT_OUT_WIDTH = KV_PER_GROUP * Q_PER_KV * HEAD_DIM
ROT_DIM = HEAD_DIM // 4
ROPE_THETA = 500000.0
Q_BLOCK = 128

PEER_HEADS = 8
PEER_TOPK = 16
N_KEYS = 128
N_EXPERTS = N_KEYS * N_KEYS
PEER_DKEY = 256
PEER_CHUNK = 128

kernel_name = "yoco_hgrn2_dilated_attn_peer"


def rmsnorm(x, g):
    xf = x.astype(jnp.float32)
    y = xf * lax.rsqrt(jnp.mean(xf * xf, axis=-1, keepdims=True) + EPS)
    return (y * g.astype(jnp.float32)).astype(x.dtype)


def rope_partial(x, pos):
    half = ROT_DIM // 2
    inv = ROPE_THETA ** (-jnp.arange(half, dtype=jnp.float32) * 2.0 / ROT_DIM)
    ang = pos.astype(jnp.float32)[:, None] * inv[None, :]
    ang = ang.reshape((1, pos.shape[0]) + (1,) * (x.ndim - 3) + (half,))
    cos, sin = jnp.cos(ang), jnp.sin(ang)
    xr = x[..., :ROT_DIM].astype(jnp.float32)
    x1, x2 = xr[..., :half], xr[..., half:]
    rot = jnp.concatenate([x1 * cos - x2 * sin, x2 * cos + x1 * sin], axis=-1)
    return jnp.concatenate([rot.astype(x.dtype), x[..., ROT_DIM:]], axis=-1)


def hgrn2_mixer(h, norm_g, w_in, lb, out_g, w_out):
    B, S, _ = h.shape
    nc = S // HG_CHUNK
    xn = rmsnorm(h, norm_g)
    q, fr, i, g = jnp.split(xn @ w_in, 4, axis=-1)
    q = jax.nn.silu(q.astype(jnp.float32))
    f = lb + (1.0 - lb) * jax.nn.sigmoid(fr.astype(jnp.float32))
    k = 1.0 - f
    logf = jnp.log(f)

    def to_chunks(t, d):
        return t.astype(jnp.float32).reshape(B, nc, HG_CHUNK, HG_HEADS, d).transpose(1, 0, 3, 2, 4)

    qc, kc, gc = to_chunks(q, HG_DK), to_chunks(k, HG_DK), to_chunks(logf, HG_DK)
    vc = to_chunks(i, HG_DV)
    causal = jnp.tril(jnp.ones((HG_CHUNK, HG_CHUNK), dtype=bool))

    def step(state, inp):
        qt, kt, vt, gt = inp
        b = jnp.cumsum(gt, axis=-2)
        inter = jnp.einsum('bhcd,bhde->bhce', qt * jnp.exp(b), state)
        diff = b[:, :, :, None, :] - b[:, :, None, :, :]
        decay = jnp.exp(jnp.where(causal[None, None, :, :, None], diff, -jnp.inf))
        scores = jnp.einsum('bhid,bhijd,bhjd->bhij', qt, decay, kt)
        intra = jnp.einsum('bhij,bhje->bhie', scores, vt)
        b_last = b[:, :, -1:, :]
        new_state = jnp.exp(b_last[:, :, 0, :])[..., None] * state + jnp.einsum(
            'bhjd,bhje->bhde', kt * jnp.exp(b_last - b), vt)
        return new_state, inter + intra

    s0 = jnp.zeros((B, HG_HEADS, HG_DK, HG_DV), jnp.float32)
    _, o = lax.scan(step, s0, (qc, kc, vc, gc))
    o = o.transpose(1, 0, 3, 2, 4).reshape(B, S, HG_HEADS, HG_DV)
    o = rmsnorm(o, out_g) * jax.nn.silu(g.astype(jnp.float32).reshape(B, S, HG_HEADS, HG_DV))
    return o.reshape(B, S, D_MODEL).astype(h.dtype) @ w_out


def shared_kv(h, kv_norm, w_kv, k_norm):
    B, S, _ = h.shape
    pos = jnp.arange(S)
    k, v = jnp.split(rmsnorm(h, kv_norm) @ w_kv, 2, axis=-1)
    k = k.reshape(B, S, N_GROUPS, KV_PER_GROUP, HEAD_DIM)
    v = v.reshape(B, S, N_GROUPS, KV_PER_GROUP, HEAD_DIM)
    k = rope_partial(rmsnorm(k, k_norm), pos)
    return k, v


def dilated_attention(h, k, v, norm_g, w_q, q_norm, w_o):
    B, S, _ = h.shape
    pos = jnp.arange(S)
    q = (rmsnorm(h, norm_g) @ w_q).reshape(B, S, N_GROUPS, KV_PER_GROUP, Q_PER_KV, HEAD_DIM)
    q = rope_partial(rmsnorm(q, q_norm), pos)
    scale = HEAD_DIM ** -0.5

    def block(s0):
        t = s0 + jnp.arange(Q_BLOCK)
        qb = lax.dynamic_slice_in_dim(q, s0, Q_BLOCK, axis=1)
        outs, lses = [], []
        for gi, (w, d) in enumerate(DIL_GROUPS):
            m = jnp.arange(w // d + 1)
            idx = t[:, None] - d * m[None, :]
            valid = idx >= 0
            idxc = jnp.maximum(idx, 0)
            kg = jnp.take(k[:, :, gi], idxc, axis=1)
            vg = jnp.take(v[:, :, gi], idxc, axis=1)
            s = jnp.einsum('bqkrd,bqmkd->bqkrm', qb[:, :, gi], kg).astype(jnp.float32) * scale
            s = jnp.where(valid[None, :, None, None, :], s, -jnp.inf)
            lse = jax.nn.logsumexp(s, axis=-1)
            p = jnp.exp(s - lse[..., None])
            o = jnp.einsum('bqkrm,bqmkd->bqkrd', p.astype(vg.dtype), vg)
            outs.append(o.astype(jnp.float32))
            lses.append(lse)
        alpha = jax.nn.softmax(jnp.stack(lses, axis=0), axis=0)
        comb = jnp.sum(alpha[..., None] * jnp.stack(outs, axis=0), axis=0)
        return comb.astype(h.dtype)

    starts = jnp.arange(S // Q_BLOCK) * Q_BLOCK
    o = lax.map(block, starts)
    o = jnp.moveaxis(o, 0, 1).reshape(B, S, ATT_OUT_WIDTH)
    return o @ w_o


def peer(h, norm_g, w_q, sub_keys, u, v):
    B, S, D = h.shape
    T = B * S
    half = PEER_DKEY // 2
    xn = rmsnorm(h, norm_g).reshape(T, D)
    q = (xn @ w_q).reshape(T, PEER_HEADS, PEER_DKEY)
    s1 = jnp.einsum('thd,nd->thn', q[..., :half], sub_keys[0]).astype(jnp.float32)
    s2 = jnp.einsum('thd,nd->thn', q[..., half:], sub_keys[1]).astype(jnp.float32)
    v1, i1 = lax.top_k(s1, PEER_TOPK)
    v2, i2 = lax.top_k(s2, PEER_TOPK)
    cand = (v1[..., :, None] + v2[..., None, :]).reshape(T, PEER_HEADS, PEER_TOPK * PEER_TOPK)
    cs, ci = lax.top_k(cand, PEER_TOPK)
    e1 = jnp.take_along_axis(i1, ci // PEER_TOPK, axis=-1)
    e2 = jnp.take_along_axis(i2, ci % PEER_TOPK, axis=-1)
    experts = (e1 * N_KEYS + e2).reshape(T, PEER_HEADS * PEER_TOPK)
    gates = jax.nn.softmax(cs, axis=-1).reshape(T, PEER_HEADS * PEER_TOPK)

    def chunk(args):
        xc, ec, gc = args
        uc = jnp.take(u, ec, axis=0)
        z = jnp.einsum('td,ted->te', xc, uc).astype(jnp.float32)
        act = jax.nn.gelu(z) * gc
        vc = jnp.take(v, ec, axis=0)
        return jnp.einsum('te,ted->td', act.astype(vc.dtype), vc)

    nch = T // PEER_CHUNK
    out = lax.map(chunk, (xn.reshape(nch, PEER_CHUNK, D),
                          experts.reshape(nch, PEER_CHUNK, -1),
                          gates.reshape(nch, PEER_CHUNK, -1)))
    return out.reshape(B, S, D).astype(h.dtype)


def setup_inputs(seed: int = 0) -> dict:
    key = jax.random.key(seed)
    ks = jax.random.split(key, 20)
    nrm = jax.random.normal
    D = D_MODEL
    return {
        "x": nrm(ks[0], (BATCH, SEQ, D), jnp.float32),
        "hgrn_norm": 1.0 + 0.02 * nrm(ks[1], (N_A_LAYERS, D), jnp.float32),
        "hgrn_w_in": nrm(ks[2], (N_A_LAYERS, D, 4 * D), jnp.float32) * D ** -0.5,
        "hgrn_lb_logits": 0.1 * nrm(ks[3], (N_A_LAYERS + 1, D), jnp.float32),
        "hgrn_out_norm": 1.0 + 0.02 * nrm(ks[4], (N_A_LAYERS, HG_DV), jnp.float32),
        "hgrn_w_out": nrm(ks[5], (N_A_LAYERS, D, D), jnp.float32) * D ** -0.5,
        "kv_norm": 1.0 + 0.02 * nrm(ks[6], (D,), jnp.float32),
        "w_kv": nrm(ks[7], (D, 2 * KV_WIDTH), jnp.float32) * D ** -0.5,
        "k_norm": 1.0 + 0.02 * nrm(ks[8], (HEAD_DIM,), jnp.float32),
        "attn_norm": 1.0 + 0.02 * nrm(ks[9], (N_B_LAYERS, D), jnp.float32),
        "w_q": nrm(ks[10], (N_B_LAYERS, D, Q_WIDTH), jnp.float32) * D ** -0.5,
        "q_norm": 1.0 + 0.02 * nrm(ks[11], (N_B_LAYERS, HEAD_DIM), jnp.float32),
        "w_o": nrm(ks[12], (N_B_LAYERS, ATT_OUT_WIDTH, D), jnp.float32) * ATT_OUT_WIDTH ** -0.5,
        "ffn_norm": 1.0 + 0.02 * nrm(ks[13], (DEPTH, D), jnp.float32),
        "peer_w_q": nrm(ks[14], (DEPTH, D, PEER_HEADS * PEER_DKEY), jnp.float32) * D ** -0.5,
        "peer_sub_keys": nrm(ks[15], (DEPTH, 2, N_KEYS, PEER_DKEY // 2), jnp.float32) * (PEER_DKEY // 2) ** -0.5,
        "peer_u": nrm(ks[16], (DEPTH, N_EXPERTS, D), jnp.float32) * D ** -0.5,
        "peer_v": nrm(ks[17], (DEPTH, N_EXPERTS, D), jnp.float32) * PEER_HEADS ** -0.5,
    }


def reference(x, hgrn_norm, hgrn_w_in, hgrn_lb_logits, hgrn_out_norm, hgrn_w_out,
              kv_norm, w_kv, k_norm, attn_norm, w_q, q_norm, w_o,
              ffn_norm, peer_w_q, peer_sub_keys, peer_u, peer_v):
    lbs = jnp.cumsum(jax.nn.softmax(hgrn_lb_logits.astype(jnp.float32), axis=0), axis=0)
    h = x
    k_sh, v_sh = None, None
    for layer in range(DEPTH):
        if layer < N_A_LAYERS:
            h = h + hgrn2_mixer(h, hgrn_norm[layer], hgrn_w_in[layer], lbs[layer],
                                hgrn_out_norm[layer], hgrn_w_out[layer]).astype(h.dtype)
        else:
            if layer == N_A_LAYERS:
                k_sh, v_sh = shared_kv(h, kv_norm, w_kv, k_norm)
            j = layer - N_A_LAYERS
            h = h + dilated_attention(h, k_sh, v_sh, attn_norm[j], w_q[j],
                                      q_norm[j], w_o[j]).astype(h.dtype)
        h = h + peer(h, ffn_norm[layer], peer_w_q[layer], peer_sub_keys[layer],
                     peer_u[layer], peer_v[layer]).astype(h.dtype)
    return h
```

```python
import functools
import math

import numpy as np
import jax
import jax.numpy as jnp
from jax import lax
from jax.experimental import pallas as pl
from jax.experimental.pallas import tpu as pltpu

F32 = jnp.float32
BF16 = jnp.bfloat16
EPS = 1e-6

LANES = 128
HEAD = 128
HG_CHUNK = 128
DIL_GROUPS = ((128, 1), (512, 4), (2048, 16))
KV_PER_GROUP = 2
Q_PER_KV = 2
ROT_DIM = HEAD // 4
ROPE_THETA = 500000.0
Q_BLOCK = 128
PEER_HEADS = 8
PEER_TOPK = 16
N_KEYS = 128
VMEM_LIMIT = 56 * 1024 * 1024

NT_DIMS = (((1,), (1,)), ((), ()))


def _cparams(sem):
    return pltpu.CompilerParams(dimension_semantics=sem, vmem_limit_bytes=VMEM_LIMIT)


def _rms(x):
    return x * lax.rsqrt(jnp.mean(x * x, axis=-1, keepdims=True) + EPS)


def _sigmoid(x):
    return 1.0 / (1.0 + jnp.exp(-x))


def _norm_matmul_kernel(x_ref, g_ref, w_ref, o_ref, xn_ref):
    @pl.when(pl.program_id(1) == 0)
    def _():
        xn_ref[...] = (_rms(x_ref[...]) * g_ref[...]).astype(xn_ref.dtype)

    o_ref[...] = jnp.dot(xn_ref[...], w_ref[...],
                         preferred_element_type=F32).astype(o_ref.dtype)


def norm_matmul(x, gain, w, *, tm, tn, out_dtype=F32):
    t, d = x.shape
    n = w.shape[1]
    return pl.pallas_call(
        _norm_matmul_kernel,
        grid=(t // tm, n // tn),
        in_specs=[pl.BlockSpec((tm, d), lambda i, j: (i, 0)),
                  pl.BlockSpec((1, d), lambda i, j: (0, 0)),
                  pl.BlockSpec((d, tn), lambda i, j: (0, j))],
        out_specs=pl.BlockSpec((tm, tn), lambda i, j: (i, j)),
        out_shape=jax.ShapeDtypeStruct((t, n), out_dtype),
        scratch_shapes=[pltpu.VMEM((tm, d), BF16)],
        compiler_params=_cparams(("parallel", "arbitrary")),
        name="norm_matmul",
    )(x, gain.reshape(1, d), w)


def _matmul_res_kernel(a_ref, w_ref, r_ref, o_ref):
    o_ref[...] = r_ref[...] + jnp.dot(a_ref[...], w_ref[...], preferred_element_type=F32)


def matmul_residual(a, w, res, *, tm):
    t, k = a.shape
    n = w.shape[1]
    return pl.pallas_call(
        _matmul_res_kernel,
        grid=(t // tm,),
        in_specs=[pl.BlockSpec((tm, k), lambda i: (i, 0)),
                  pl.BlockSpec((k, n), lambda i: (0, 0)),
                  pl.BlockSpec((tm, n), lambda i: (i, 0))],
        out_specs=pl.BlockSpec((tm, n), lambda i: (i, 0)),
        out_shape=jax.ShapeDtypeStruct((t, n), F32),
        compiler_params=_cparams(("parallel",)),
        name="matmul_residual",
    )(a, w, res)


def _hgrn_levels(c):
    return int(math.log2(c))


def _hgrn_sum_matrix(c):
    nlev = _hgrn_levels(c)
    m = np.zeros(((nlev + 2) * c, c), np.float32)
    for l in range(nlev):
        h = c >> (l + 1)
        for r in range(c):
            mid = (r // (2 * h)) * 2 * h + h
            if r >= mid:
                m[l * c + r, mid:r + 1] = 1.0
            else:
                m[l * c + r, r + 1:mid] = 1.0
    for r in range(c):
        m[nlev * c + r, :r + 1] = 1.0
        m[(nlev + 1) * c + r, r + 1:] = 1.0
    return m


def _hgrn_kernel(q_ref, f_ref, i_ref, g_ref, lbl_ref, og_ref, m_ref, o_ref, st_ref, *,
                 layer, chunk, n_chunks):
    c = chunk
    nlev = _hgrn_levels(c)

    @pl.when(pl.program_id(2) == 0)
    def _():
        st_ref[...] = jnp.zeros_like(st_ref)

    lg = lbl_ref[...]
    e = jnp.exp(lg - jnp.max(lg, axis=0, keepdims=True))
    lb = jnp.sum(e[:layer + 1], axis=0, keepdims=True) / jnp.sum(e, axis=0, keepdims=True)
    og = og_ref[...]
    row = lax.broadcasted_iota(jnp.int32, (c, c), 0)
    col = lax.broadcasted_iota(jnp.int32, (c, c), 1)

    def body(ci, carry):
        sl = pl.ds(pl.multiple_of(ci * c, c), c)
        q = q_ref[0, sl, :]
        fr = f_ref[0, sl, :]
        v = i_ref[0, sl, :]
        gt = g_ref[0, sl, :]
        qa = q * _sigmoid(q)
        f = lb + (1.0 - lb) * _sigmoid(fr)
        k = 1.0 - f
        logf = jnp.log(f)
        hi = logf.astype(BF16)
        r1 = logf - hi.astype(F32)
        mid = r1.astype(BF16)
        lo = (r1 - mid.astype(F32)).astype(BF16)
        e3 = jnp.dot(m_ref[...], jnp.concatenate([hi, mid, lo], axis=1),
                     preferred_element_type=F32)
        ex = e3[:, :HEAD] + e3[:, HEAD:2 * HEAD] + e3[:, 2 * HEAD:]

        kb = k.astype(BF16)
        s = jnp.where(row == col,
                      lax.dot_general(qa.astype(BF16), kb, NT_DIMS, preferred_element_type=F32),
                      0.0)
        for l in range(nlev):
            sh = nlev - 1 - l
            x = jnp.exp(ex[l * c:(l + 1) * c])
            p = lax.dot_general((qa * x).astype(BF16), (k * x).astype(BF16), NT_DIMS,
                                preferred_element_type=F32)
            mask = (((row >> (sh + 1)) == (col >> (sh + 1)))
                    & (((row >> sh) & 1) == 1) & (((col >> sh) & 1) == 0))
            s = jnp.where(mask, p, s)
        vb = v.astype(BF16)
        intra = jnp.dot(s.astype(BF16), vb, preferred_element_type=F32)
        b_incl = ex[nlev * c:(nlev + 1) * c]
        st = st_ref[...]
        inter = lax.dot_general((qa * jnp.exp(b_incl)).astype(BF16), st.astype(BF16), NT_DIMS,
                                preferred_element_type=F32)
        o = inter + intra
        y = _rms(o) * og * (gt * _sigmoid(gt))
        o_ref[0, sl, :] = y.astype(o_ref.dtype)

        k2 = (k * jnp.exp(ex[(nlev + 1) * c:(nlev + 2) * c])).astype(BF16)
        upd = jnp.dot(v.T.astype(BF16), k2, preferred_element_type=F32)
        st_ref[...] = st * jnp.exp(b_incl[c - 1:c, :]) + upd
        return carry

    lax.fori_loop(0, n_chunks, body, 0)


def hgrn_scan(y, lb_logits, out_gain, *, layer, seq_block):
    b, s, d4 = y.shape
    d = d4 // 4
    nh = d // HEAD
    c = min(HG_CHUNK, seq_block)
    msel = jnp.asarray(_hgrn_sum_matrix(c), BF16)
    nl = lb_logits.shape[0]
    kern = functools.partial(_hgrn_kernel, layer=layer, chunk=c, n_chunks=seq_block // c)

    def part(p):
        return pl.BlockSpec((1, seq_block, HEAD), lambda bi, hi, si, p=p: (bi, si, p * nh + hi))

    return pl.pallas_call(
        kern,
        grid=(b, nh, s // seq_block),
        in_specs=[part(0), part(1), part(2), part(3),
                  pl.BlockSpec((nl, HEAD), lambda bi, hi, si: (0, hi)),
                  pl.BlockSpec((1, HEAD), lambda bi, hi, si: (0, 0)),
                  pl.BlockSpec(msel.shape, lambda bi, hi, si: (0, 0))],
        out_specs=pl.BlockSpec((1, seq_block, HEAD), lambda bi, hi, si: (bi, si, hi)),
        out_shape=jax.ShapeDtypeStruct((b, s, d), BF16),
        scratch_shapes=[pltpu.VMEM((HEAD, HEAD), F32)],
        compiler_params=_cparams(("parallel", "parallel", "arbitrary")),
        name="hgrn_scan",
    )(y, y, y, y, lb_logits, out_gain.reshape(1, HEAD), msel)


def _peer_prep_kernel(h_ref, g_ref, wq_ref, keys_ref, xn_ref, s1_ref, s2_ref):
    xn = (_rms(h_ref[...]) * g_ref[...]).astype(BF16)
    xn_ref[...] = xn
    q = jnp.dot(xn, wq_ref[...], preferred_element_type=F32)
    k0 = keys_ref[0].astype(BF16)
    k1 = keys_ref[1].astype(BF16)
    for hh in range(PEER_HEADS):
        base = hh * 2 * HEAD
        q1 = q[:, base:base + HEAD].astype(BF16)
        q2 = q[:, base + HEAD:base + 2 * HEAD].astype(BF16)
        s1_ref[hh] = lax.dot_general(k0, q1, NT_DIMS, preferred_element_type=F32)
        s2_ref[hh] = lax.dot_general(k1, q2, NT_DIMS, preferred_element_type=F32)


def peer_prep(h, gain, wq, keys, *, tm):
    t, d = h.shape
    nq = wq.shape[1]
    return pl.pallas_call(
        _peer_prep_kernel,
        grid=(t // tm,),
        in_specs=[pl.BlockSpec((tm, d), lambda i: (i, 0)),
                  pl.BlockSpec((1, d), lambda i: (0, 0)),
                  pl.BlockSpec((d, nq), lambda i: (0, 0)),
                  pl.BlockSpec(keys.shape, lambda i: (0, 0, 0))],
        out_specs=[pl.BlockSpec((tm, d), lambda i: (i, 0)),
                   pl.BlockSpec((PEER_HEADS, N_KEYS, tm), lambda i: (0, 0, i)),
                   pl.BlockSpec((PEER_HEADS, N_KEYS, tm), lambda i: (0, 0, i))],
        out_shape=[jax.ShapeDtypeStruct((t, d), BF16),
                   jax.ShapeDtypeStruct((PEER_HEADS, N_KEYS, t), F32),
                   jax.ShapeDtypeStruct((PEER_HEADS, N_KEYS, t), F32)],
        compiler_params=_cparams(("parallel",)),
        name="peer_prep",
    )(h, gain.reshape(1, d), wq, keys)


def _top_values(x, n, dst_ref):
    cur = x
    for r in range(n):
        m = jnp.max(cur, axis=0, keepdims=True)
        dst_ref[r:r + 1, :] = m
        cur = jnp.where(cur == m, -jnp.inf, cur)


def _peer_topk_kernel(s1_ref, s2_ref, thr_ref, ar_ref, bt_ref, a_ref, b_ref):
    k = PEER_TOPK
    sub = 8
    rid = lax.broadcasted_iota(jnp.int32, (sub, a_ref.shape[1]), 0)

    def head(hh, carry):
        s1 = s1_ref[hh]
        s2 = s2_ref[hh]
        _top_values(s1, k, a_ref)
        _top_values(s2, k, b_ref)
        a_lo, a_hi = a_ref[0:sub, :], a_ref[sub:k, :]
        b_hi = b_ref[sub:k, :]
        a0 = a_ref[0:1, :]
        slabs = [a_lo + b_ref[0:1, :], a_hi + b_ref[0:1, :], b_hi + a0]
        for j in range(1, sub):
            lim = k // (j + 1)
            slabs.append(jnp.where(rid < lim, a_lo + b_ref[j:j + 1, :], -jnp.inf))
        cur = slabs
        tau = None
        for r in range(k):
            m = functools.reduce(jnp.maximum, cur)
            m = jnp.max(m, axis=0, keepdims=True)
            tau = m
            if r + 1 < k:
                cur = [jnp.where(x == m, -jnp.inf, x) for x in cur]
        m1 = a0
        m2 = b_ref[0:1, :]
        top = m1 + m2
        z = None
        for x in slabs:
            zz = jnp.sum(jnp.where(x >= tau, jnp.exp(x - top), 0.0), axis=0, keepdims=True)
            z = zz if z is None else z + zz
        thr = jnp.full(s1.shape, jnp.inf, F32)
        for j in range(k):
            bj = b_ref[j:j + 1, :]
            thr = jnp.where(s1 + bj >= tau, bj, thr)
        thr_ref[hh] = thr
        ar_ref[hh] = jnp.exp(s1 - m1) / z
        bt_ref[hh] = jnp.exp(s2 - m2)
        return carry

    lax.fori_loop(0, PEER_HEADS, head, 0)


def peer_topk(s1t, s2t, *, tt):
    nh, nk, t = s1t.shape
    spec = pl.BlockSpec((nh, nk, tt), lambda i: (0, 0, i))
    shp = jax.ShapeDtypeStruct((nh, nk, t), F32)
    return pl.pallas_call(
        _peer_topk_kernel,
        grid=(t // tt,),
        in_specs=[spec, spec],
        out_specs=[spec, spec, spec],
        out_shape=[shp, shp, shp],
        scratch_shapes=[pltpu.VMEM((PEER_TOPK, tt), F32), pltpu.VMEM((PEER_TOPK, tt), F32)],
        compiler_params=_cparams(("parallel",)),
        name="peer_topk",
    )(s1t, s2t)


GELU_C0 = math.sqrt(2.0 / math.pi)
GELU_C1 = GELU_C0 * 0.044715


def _gelu(z):
    return 0.5 * z * (1.0 + jnp.tanh(z * (GELU_C0 + GELU_C1 * (z * z))))


def _peer_main_kernel(h_ref, xn_ref, u_ref, vt_ref, thr_ref, ar_ref, s2_ref, bt_ref, o_ref,
                      acc_ref, zt_ref, a_ref, *, lane_tile):
    j = pl.program_id(1)
    et, tt = zt_ref.shape
    n_e1 = et // N_KEYS

    @pl.when(j == 0)
    def _():
        acc_ref[...] = jnp.zeros_like(acc_ref)

    zt_ref[...] = lax.dot_general(u_ref[...], xn_ref[...], NT_DIMS, preferred_element_type=F32)

    def lane_block(li, carry):
        ln = pl.ds(pl.multiple_of(li * lane_tile, lane_tile), lane_tile)
        for c in range(n_e1):
            rows = slice(c * N_KEYS, (c + 1) * N_KEYS)
            g = jnp.zeros((N_KEYS, lane_tile), F32)
            for hh in range(PEER_HEADS):
                thr = thr_ref[hh, c:c + 1, ln]
                ar = ar_ref[hh, c:c + 1, ln]
                g = g + jnp.where(s2_ref[hh, :, ln] >= thr, bt_ref[hh, :, ln] * ar, 0.0)
            a_ref[rows, ln] = (_gelu(zt_ref[rows, ln]) * g).astype(a_ref.dtype)
        return carry

    lax.fori_loop(0, tt // lane_tile, lane_block, 0)
    acc_ref[...] += jnp.dot(vt_ref[...], a_ref[...], preferred_element_type=F32)

    @pl.when(j == pl.num_programs(1) - 1)
    def _():
        o_ref[...] = h_ref[...] + acc_ref[...].T


def peer_main(h, xn, u, vt, thr, ar, s2t, bt, *, tt, et, lane_tile):
    t, d = h.shape
    ne = u.shape[0]
    n_e1 = et // N_KEYS
    kern = functools.partial(_peer_main_kernel, lane_tile=lane_tile)
    row_spec = pl.BlockSpec((PEER_HEADS, n_e1, tt), lambda i, j: (0, j, i))
    tok_spec = pl.BlockSpec((PEER_HEADS, N_KEYS, tt), lambda i, j: (0, 0, i))
    return pl.pallas_call(
        kern,
        grid=(t // tt, ne // et),
        in_specs=[pl.BlockSpec((tt, d), lambda i, j: (i, 0)),
                  pl.BlockSpec((tt, d), lambda i, j: (i, 0)),
                  pl.BlockSpec((et, d), lambda i, j: (j, 0)),
                  pl.BlockSpec((d, et), lambda i, j: (0, j)),
                  row_spec, row_spec, tok_spec, tok_spec],
        out_specs=pl.BlockSpec((tt, d), lambda i, j: (i, 0)),
        out_shape=jax.ShapeDtypeStruct((t, d), F32),
        scratch_shapes=[pltpu.VMEM((d, tt), F32), pltpu.VMEM((et, tt), F32),
                        pltpu.VMEM((et, tt), BF16)],
        compiler_params=_cparams(("parallel", "arbitrary")),
        name="peer_main",
    )(h, xn, u, vt, thr, ar, s2t, bt)


def peer_layer(h, gain, wq, keys, u, v):
    t = h.shape[0]
    xn, s1t, s2t = peer_prep(h, gain, wq.astype(BF16), keys, tm=min(512, t))
    thr, ar, bt = peer_topk(s1t, s2t, tt=min(256, t))
    return peer_main(h, xn, u.astype(BF16), v.astype(BF16).T, thr, ar, s2t, bt,
                     tt=min(512, t), et=1024, lane_tile=256)


def _rope_tables(s):
    half = ROT_DIM // 2
    inv = ROPE_THETA ** (-jnp.arange(half, dtype=F32) * 2.0 / ROT_DIM)
    ang = jnp.arange(s).astype(F32)[:, None] * inv[None, :]
    cos, sin = jnp.cos(ang), jnp.sin(ang)
    pad = HEAD - ROT_DIM
    ones = jnp.ones((s, pad), F32)
    zeros = jnp.zeros((s, pad), F32)
    zh = jnp.zeros((s, half), F32)
    ct = jnp.concatenate([cos, cos, ones], axis=1)
    at = jnp.concatenate([-sin, zh, zeros], axis=1)
    bt = jnp.concatenate([zh, sin, zeros], axis=1)
    return ct, at, bt


def _attn_proj_kernel(h_ref, kvg_ref, ag_ref, wkv_ref, wq_ref, kn_ref, qn_ref,
                      ct_ref, at_ref, bt_ref, q_ref, k_ref, v_ref):
    y = _rms(h_ref[...])
    kv = jnp.dot((y * kvg_ref[...]).astype(BF16), wkv_ref[...], preferred_element_type=F32)
    qq = jnp.dot((y * ag_ref[...]).astype(BF16), wq_ref[...], preferred_element_type=F32)
    ct, at, bt = ct_ref[...], at_ref[...], bt_ref[...]
    half = ROT_DIM // 2

    def head_norm_rope(x, gain):
        n = _rms(x) * gain
        return n * ct + pltpu.roll(n, HEAD - half, 1) * at + pltpu.roll(n, half, 1) * bt

    kw = k_ref.shape[1]
    for hd in range(kw // HEAD):
        cs = slice(hd * HEAD, (hd + 1) * HEAD)
        k_ref[:, cs] = head_norm_rope(kv[:, cs], kn_ref[...]).astype(k_ref.dtype)
    v_ref[...] = kv[:, kw:].astype(v_ref.dtype)
    for hd in range(q_ref.shape[1] // HEAD):
        cs = slice(hd * HEAD, (hd + 1) * HEAD)
        q_ref[:, cs] = head_norm_rope(qq[:, cs], qn_ref[...]).astype(q_ref.dtype)


def attn_proj(h, kv_gain, a_gain, wkv, wq, k_norm, q_norm, seq, *, tm):
    t, d = h.shape
    kw = wkv.shape[1] // 2
    qw = wq.shape[1]
    ct, at, bt = _rope_tables(seq)
    ns = seq // tm
    tab = pl.BlockSpec((tm, HEAD), lambda i: (i % ns, 0))
    vec = lambda n: pl.BlockSpec((1, n), lambda i: (0, 0))
    return pl.pallas_call(
        _attn_proj_kernel,
        grid=(t // tm,),
        in_specs=[pl.BlockSpec((tm, d), lambda i: (i, 0)), vec(d), vec(d),
                  pl.BlockSpec(wkv.shape, lambda i: (0, 0)),
                  pl.BlockSpec(wq.shape, lambda i: (0, 0)),
                  vec(HEAD), vec(HEAD), tab, tab, tab],
        out_specs=[pl.BlockSpec((tm, qw), lambda i: (i, 0)),
                   pl.BlockSpec((tm, kw), lambda i: (i, 0)),
                   pl.BlockSpec((tm, kw), lambda i: (i, 0))],
        out_shape=[jax.ShapeDtypeStruct((t, qw), BF16),
                   jax.ShapeDtypeStruct((t, kw), BF16),
                   jax.ShapeDtypeStruct((t, kw), BF16)],
        compiler_params=_cparams(("parallel",)),
        name="attn_proj",
    )(h, kv_gain.reshape(1, d), a_gain.reshape(1, d), wkv, wq,
      k_norm.reshape(1, HEAD), q_norm.reshape(1, HEAD), ct, at, bt)


def _attn_kernel(q0_ref, q1_ref, q2_ref, k0_ref, k1_ref, k2_ref, v0_ref, v1_ref, v2_ref, o_ref):
    qb = pl.program_id(2)
    t0 = qb * Q_BLOCK
    rows = Q_PER_KV * Q_BLOCK
    scale = HEAD ** -0.5
    scores, vals = [], []
    for (w, dil), q_ref, k_ref, v_ref in zip(DIL_GROUPS, (q0_ref, q1_ref, q2_ref),
                                             (k0_ref, k1_ref, k2_ref),
                                             (v0_ref, v1_ref, v2_ref)):
        span = w + Q_BLOCK
        ks = pl.multiple_of(jnp.maximum(t0 - w, 0), Q_BLOCK)
        kblk = k_ref[0, pl.ds(ks, span), :]
        q2 = jnp.concatenate([q_ref[0, :, r * HEAD:(r + 1) * HEAD] for r in range(Q_PER_KV)],
                             axis=0)
        s = lax.dot_general(q2, kblk, NT_DIMS, preferred_element_type=F32) * scale
        tq = t0 + (lax.broadcasted_iota(jnp.int32, (rows, span), 0) & (Q_BLOCK - 1))
        kp = ks + lax.broadcasted_iota(jnp.int32, (rows, span), 1)
        diff = tq - kp
        valid = (diff >= 0) & (diff <= w) & ((diff & (dil - 1)) == 0)
        scores.append(jnp.where(valid, s, -jnp.inf))
        vals.append(v_ref[0, pl.ds(ks, span), :])
    m = functools.reduce(jnp.maximum, [jnp.max(s, axis=-1, keepdims=True) for s in scores])
    den = jnp.zeros((rows, 1), F32)
    out = jnp.zeros((rows, HEAD), F32)
    for s, vblk in zip(scores, vals):
        p = jnp.exp(s - m)
        den = den + jnp.sum(p, axis=-1, keepdims=True)
        out = out + jnp.dot(p.astype(BF16), vblk, preferred_element_type=F32)
    out = out / den
    for r in range(Q_PER_KV):
        o_ref[0, :, r * HEAD:(r + 1) * HEAD] = out[r * Q_BLOCK:(r + 1) * Q_BLOCK].astype(o_ref.dtype)


def dilated_attention(q, k, v):
    b, s, _ = q.shape
    qcols = Q_PER_KV * HEAD

    def qspec(g):
        return pl.BlockSpec((1, Q_BLOCK, qcols), lambda bi, kv, qb, g=g: (bi, qb, g * KV_PER_GROUP + kv))

    def kvspec(g):
        return pl.BlockSpec((1, s, HEAD), lambda bi, kv, qb, g=g: (bi, 0, g * KV_PER_GROUP + kv))

    ng = len(DIL_GROUPS)
    return pl.pallas_call(
        _attn_kernel,
        grid=(b, KV_PER_GROUP, s // Q_BLOCK),
        in_specs=[qspec(g) for g in range(ng)] + [kvspec(g) for g in range(ng)] * 2,
        out_specs=pl.BlockSpec((1, Q_BLOCK, qcols), lambda bi, kv, qb: (bi, qb, kv)),
        out_shape=jax.ShapeDtypeStruct((b, s, KV_PER_GROUP * qcols), BF16),
        compiler_params=_cparams(("parallel", "parallel", "arbitrary")),
        name="dilated_attention",
    )(q, q, q, k, k, k, v, v, v)


def kernel(x, hgrn_norm, hgrn_w_in, hgrn_lb_logits, hgrn_out_norm, hgrn_w_out, kv_norm, w_kv,
           k_norm, attn_norm, w_q, q_norm, w_o, ffn_norm, peer_w_q, peer_sub_keys, peer_u, peer_v):
    b, s, d = x.shape
    t = b * s
    depth = ffn_norm.shape[0]
    n_a = hgrn_norm.shape[0]
    h = x.reshape(t, d)
    tm = min(1024, t)
    for layer in range(depth):
        if layer < n_a:
            y = norm_matmul(h, hgrn_norm[layer], hgrn_w_in[layer].astype(BF16), tm=tm, tn=512)
            o = hgrn_scan(y.reshape(b, s, 4 * d), hgrn_lb_logits, hgrn_out_norm[layer],
                          layer=layer, seq_block=min(1024, s))
            h = matmul_residual(o.reshape(t, d), hgrn_w_out[layer].astype(BF16), h, tm=tm)
        else:
            j = layer - n_a
            if j == 0:
                pass
            qr, kr, vr = attn_proj(h, kv_norm, attn_norm[j], w_kv.astype(BF16),
                                   w_q[j].astype(BF16), k_norm, q_norm[j], s, tm=min(512, s))
            if j == 0:
                k_sh, v_sh = kr.reshape(b, s, -1), vr.reshape(b, s, -1)
            ao = dilated_attention(qr.reshape(b, s, -1), k_sh, v_sh)
            h = matmul_residual(ao.reshape(t, -1), w_o[j].astype(BF16), h, tm=tm)
        h = peer_layer(h, ffn_norm[layer], peer_w_q[layer], peer_sub_keys[layer],
                       peer_u[layer], peer_v[layer])
    return h.reshape(b, s, d)
```

```python
import functools
import math

import numpy as np
import jax
import jax.numpy as jnp
from jax import lax
from jax.experimental import pallas as pl
from jax.experimental.pallas import tpu as pltpu

F32 = jnp.float32
BF16 = jnp.bfloat16
EPS = 1e-6

LANES = 128
HEAD = 128
HG_CHUNK = 128
DIL_GROUPS = ((128, 1), (512, 4), (2048, 16))
KV_PER_GROUP = 2
Q_PER_KV = 2
ROT_DIM = HEAD // 4
ROPE_THETA = 500000.0
Q_BLOCK = 128
PEER_HEADS = 8
PEER_TOPK = 16
N_KEYS = 128
VMEM_LIMIT = 56 * 1024 * 1024

NT_DIMS = (((1,), (1,)), ((), ()))


def _cparams(sem):
    return pltpu.CompilerParams(dimension_semantics=sem, vmem_limit_bytes=VMEM_LIMIT)


def _rms(x):
    return x * lax.rsqrt(jnp.mean(x * x, axis=-1, keepdims=True) + EPS)


def _sigmoid(x):
    return 1.0 / (1.0 + jnp.exp(-x))


def _norm_matmul_kernel(x_ref, g_ref, w_ref, o_ref, xn_ref):
    @pl.when(pl.program_id(1) == 0)
    def _():
        xn_ref[...] = (_rms(x_ref[...]) * g_ref[...]).astype(xn_ref.dtype)

    o_ref[...] = jnp.dot(xn_ref[...], w_ref[...],
                         preferred_element_type=F32).astype(o_ref.dtype)


def norm_matmul(x, gain, w, *, tm, tn, out_dtype=F32):
    t, d = x.shape
    n = w.shape[1]
    return pl.pallas_call(
        _norm_matmul_kernel,
        grid=(t // tm, n // tn),
        in_specs=[pl.BlockSpec((tm, d), lambda i, j: (i, 0)),
                  pl.BlockSpec((1, d), lambda i, j: (0, 0)),
                  pl.BlockSpec((d, tn), lambda i, j: (0, j))],
        out_specs=pl.BlockSpec((tm, tn), lambda i, j: (i, j)),
        out_shape=jax.ShapeDtypeStruct((t, n), out_dtype),
        scratch_shapes=[pltpu.VMEM((tm, d), BF16)],
        compiler_params=_cparams(("parallel", "arbitrary")),
        name="norm_matmul",
    )(x, gain.reshape(1, d), w)


def _matmul_res_kernel(a_ref, w_ref, r_ref, o_ref):
    o_ref[...] = r_ref[...] + jnp.dot(a_ref[...], w_ref[...], preferred_element_type=F32)


def matmul_residual(a, w, res, *, tm):
    t, k = a.shape
    n = w.shape[1]
    return pl.pallas_call(
        _matmul_res_kernel,
        grid=(t // tm,),
        in_specs=[pl.BlockSpec((tm, k), lambda i: (i, 0)),
                  pl.BlockSpec((k, n), lambda i: (0, 0)),
                  pl.BlockSpec((tm, n), lambda i: (i, 0))],
        out_specs=pl.BlockSpec((tm, n), lambda i: (i, 0)),
        out_shape=jax.ShapeDtypeStruct((t, n), F32),
        compiler_params=_cparams(("parallel",)),
        name="matmul_residual",
    )(a, w, res)


def _hgrn_levels(c):
    return int(math.log2(c))


def _hgrn_sum_matrix(c):
    nlev = _hgrn_levels(c)
    m = np.zeros(((nlev + 2) * c, c), np.float32)
    for l in range(nlev):
        h = c >> (l + 1)
        for r in range(c):
            mid = (r // (2 * h)) * 2 * h + h
            if r >= mid:
                m[l * c + r, mid:r + 1] = 1.0
            else:
                m[l * c + r, r + 1:mid] = 1.0
    for r in range(c):
        m[nlev * c + r, :r + 1] = 1.0
        m[(nlev + 1) * c + r, r + 1:] = 1.0
    return m


def _hgrn_kernel(q_ref, f_ref, i_ref, g_ref, lbl_ref, og_ref, m_ref, o_ref, st_ref, *,
                 layer, chunk, n_chunks):
    c = chunk
    nlev = _hgrn_levels(c)

    @pl.when(pl.program_id(2) == 0)
    def _():
        st_ref[...] = jnp.zeros_like(st_ref)

    lg = lbl_ref[...]
    e = jnp.exp(lg - jnp.max(lg, axis=0, keepdims=True))
    lb = jnp.sum(e[:layer + 1], axis=0, keepdims=True) / jnp.sum(e, axis=0, keepdims=True)
    og = og_ref[...]
    row = lax.broadcasted_iota(jnp.int32, (c, c), 0)
    col = lax.broadcasted_iota(jnp.int32, (c, c), 1)

    def body(ci, carry):
        sl = pl.ds(pl.multiple_of(ci * c, c), c)
        q = q_ref[0, sl, :]
        fr = f_ref[0, sl, :]
        v = i_ref[0, sl, :]
        gt = g_ref[0, sl, :]
        qa = q * _sigmoid(q)
        f = lb + (1.0 - lb) * _sigmoid(fr)
        k = 1.0 - f
        logf = jnp.log(f)
        hi = logf.astype(BF16)
        r1 = logf - hi.astype(F32)
        mid = r1.astype(BF16)
        lo = (r1 - mid.astype(F32)).astype(BF16)
        e3 = jnp.dot(m_ref[...], jnp.concatenate([hi, mid, lo], axis=1),
                     preferred_element_type=F32)
        ex = e3[:, :HEAD] + e3[:, HEAD:2 * HEAD] + e3[:, 2 * HEAD:]

        kb = k.astype(BF16)
        s = jnp.where(row == col,
                      lax.dot_general(qa.astype(BF16), kb, NT_DIMS, preferred_element_type=F32),
                      0.0)
        for l in range(nlev):
            sh = nlev - 1 - l
            x = jnp.exp(ex[l * c:(l + 1) * c])
            p = lax.dot_general((qa * x).astype(BF16), (k * x).astype(BF16), NT_DIMS,
                                preferred_element_type=F32)
            mask = (((row >> (sh + 1)) == (col >> (sh + 1)))
                    & (((row >> sh) & 1) == 1) & (((col >> sh) & 1) == 0))
            s = jnp.where(mask, p, s)
        vb = v.astype(BF16)
        intra = jnp.dot(s.astype(BF16), vb, preferred_element_type=F32)
        b_incl = ex[nlev * c:(nlev + 1) * c]
        st = st_ref[...]
        inter = lax.dot_general((qa * jnp.exp(b_incl)).astype(BF16), st.astype(BF16), NT_DIMS,
                                preferred_element_type=F32)
        o = inter + intra
        y = _rms(o) * og * (gt * _sigmoid(gt))
        o_ref[0, sl, :] = y.astype(o_ref.dtype)

        k2 = (k * jnp.exp(ex[(nlev + 1) * c:(nlev + 2) * c])).astype(BF16)
        upd = jnp.dot(v.T.astype(BF16), k2, preferred_element_type=F32)
        st_ref[...] = st * jnp.exp(b_incl[c - 1:c, :]) + upd
        return carry

    lax.fori_loop(0, n_chunks, body, 0, unroll=4)


def hgrn_scan(y, lb_logits, out_gain, *, layer, seq_block):
    b, s, d4 = y.shape
    d = d4 // 4
    nh = d // HEAD
    c = min(HG_CHUNK, seq_block)
    msel = jnp.asarray(_hgrn_sum_matrix(c), BF16)
    nl = lb_logits.shape[0]
    kern = functools.partial(_hgrn_kernel, layer=layer, chunk=c, n_chunks=seq_block // c)

    def part(p):
        return pl.BlockSpec((1, seq_block, HEAD), lambda bi, hi, si, p=p: (bi, si, p * nh + hi))

    return pl.pallas_call(
        kern,
        grid=(b, nh, s // seq_block),
        in_specs=[part(0), part(1), part(2), part(3),
                  pl.BlockSpec((nl, HEAD), lambda bi, hi, si: (0, hi)),
                  pl.BlockSpec((1, HEAD), lambda bi, hi, si: (0, 0)),
                  pl.BlockSpec(msel.shape, lambda bi, hi, si: (0, 0))],
        out_specs=pl.BlockSpec((1, seq_block, HEAD), lambda bi, hi, si: (bi, si, hi)),
        out_shape=jax.ShapeDtypeStruct((b, s, d), BF16),
        scratch_shapes=[pltpu.VMEM((HEAD, HEAD), F32)],
        compiler_params=_cparams(("parallel", "parallel", "arbitrary")),
        name="hgrn_scan",
    )(y, y, y, y, lb_logits, out_gain.reshape(1, HEAD), msel)


def _peer_prep_kernel(h_ref, g_ref, wq_ref, keys_ref, xn_ref, s1_ref, s2_ref):
    xn = (_rms(h_ref[...]) * g_ref[...]).astype(BF16)
    xn_ref[...] = xn
    q = jnp.dot(xn, wq_ref[...], preferred_element_type=F32)
    k0 = keys_ref[0].astype(BF16)
    k1 = keys_ref[1].astype(BF16)
    for hh in range(PEER_HEADS):
        base = hh * 2 * HEAD
        q1 = q[:, base:base + HEAD].astype(BF16)
        q2 = q[:, base + HEAD:base + 2 * HEAD].astype(BF16)
        s1_ref[hh] = lax.dot_general(k0, q1, NT_DIMS, preferred_element_type=F32)
        s2_ref[hh] = lax.dot_general(k1, q2, NT_DIMS, preferred_element_type=F32)


def peer_prep(h, gain, wq, keys, *, tm):
    t, d = h.shape
    nq = wq.shape[1]
    return pl.pallas_call(
        _peer_prep_kernel,
        grid=(t // tm,),
        in_specs=[pl.BlockSpec((tm, d), lambda i: (i, 0)),
                  pl.BlockSpec((1, d), lambda i: (0, 0)),
                  pl.BlockSpec((d, nq), lambda i: (0, 0)),
                  pl.BlockSpec(keys.shape, lambda i: (0, 0, 0))],
        out_specs=[pl.BlockSpec((tm, d), lambda i: (i, 0)),
                   pl.BlockSpec((PEER_HEADS, N_KEYS, tm), lambda i: (0, 0, i)),
                   pl.BlockSpec((PEER_HEADS, N_KEYS, tm), lambda i: (0, 0, i))],
        out_shape=[jax.ShapeDtypeStruct((t, d), BF16),
                   jax.ShapeDtypeStruct((PEER_HEADS, N_KEYS, t), F32),
                   jax.ShapeDtypeStruct((PEER_HEADS, N_KEYS, t), F32)],
        compiler_params=_cparams(("parallel",)),
        name="peer_prep",
    )(h, gain.reshape(1, d), wq, keys)


def _top_values(x, n, dst_ref):
    cur = x
    for r in range(n):
        m = jnp.max(cur, axis=0, keepdims=True)
        dst_ref[r:r + 1, :] = m
        cur = jnp.where(cur == m, -jnp.inf, cur)


def _peer_topk_kernel(s1_ref, s2_ref, thr_ref, ar_ref, s2o_ref, bt_ref, a_ref, b_ref):
    k = PEER_TOPK
    sub = 8
    rid = lax.broadcasted_iota(jnp.int32, (sub, a_ref.shape[1]), 0)

    def head(hh, carry):
        s1 = s1_ref[hh]
        s2 = s2_ref[hh]
        _top_values(s1, k, a_ref)
        _top_values(s2, k, b_ref)
        a_lo, a_hi = a_ref[0:sub, :], a_ref[sub:k, :]
        b_hi = b_ref[sub:k, :]
        a0 = a_ref[0:1, :]
        slabs = [a_lo + b_ref[0:1, :], a_hi + b_ref[0:1, :], b_hi + a0]
        for j in range(1, sub):
            lim = k // (j + 1)
            slabs.append(jnp.where(rid < lim, a_lo + b_ref[j:j + 1, :], -jnp.inf))
        cur = slabs
        tau = None
        for r in range(k):
            m = functools.reduce(jnp.maximum, cur)
            m = jnp.max(m, axis=0, keepdims=True)
            tau = m
            if r + 1 < k:
                cur = [jnp.where(x == m, -jnp.inf, x) for x in cur]
        m1 = a0
        m2 = b_ref[0:1, :]
        top = m1 + m2
        z = None
        for x in slabs:
            zz = jnp.sum(jnp.where(x >= tau, jnp.exp(x - top), 0.0), axis=0, keepdims=True)
            z = zz if z is None else z + zz
        thr = jnp.full(s1.shape, jnp.inf, F32)
        for j in range(k):
            bj = b_ref[j:j + 1, :]
            thr = jnp.where(s1 + bj >= tau, bj, thr)
        thr_ref[hh] = thr
        ar_ref[hh] = jnp.exp(s1 - m1) / z
        bt = jnp.exp(s2 - m2)
        for lb in range(s2.shape[1] // LANES):
            ln = slice(lb * LANES, (lb + 1) * LANES)
            s2o_ref[hh, lb] = s2[:, ln]
            bt_ref[hh, lb] = bt[:, ln]
        return carry

    lax.fori_loop(0, PEER_HEADS, head, 0)


def peer_topk(s1t, s2t, *, tt):
    nh, nk, t = s1t.shape
    spec = pl.BlockSpec((nh, nk, tt), lambda i: (0, 0, i))
    shp = jax.ShapeDtypeStruct((nh, nk, t), F32)
    tspec = pl.BlockSpec((nh, tt // LANES, nk, LANES), lambda i: (0, i, 0, 0))
    tshp = jax.ShapeDtypeStruct((nh, t // LANES, nk, LANES), F32)
    return pl.pallas_call(
        _peer_topk_kernel,
        grid=(t // tt,),
        in_specs=[spec, spec],
        out_specs=[spec, spec, tspec, tspec],
        out_shape=[shp, shp, tshp, tshp],
        scratch_shapes=[pltpu.VMEM((PEER_TOPK, tt), F32), pltpu.VMEM((PEER_TOPK, tt), F32)],
        compiler_params=_cparams(("parallel",)),
        name="peer_topk",
    )(s1t, s2t)


GELU_C0 = math.sqrt(2.0 / math.pi)
GELU_C1 = GELU_C0 * 0.044715


def _gelu(z):
    return 0.5 * z * (1.0 + jnp.tanh(z * (GELU_C0 + GELU_C1 * (z * z))))


def _peer_main_kernel(h_ref, xn_ref, u_ref, vt_ref, thr_ref, ar_ref, s2_ref, bt_ref, o_ref,
                      acc_ref, zt_ref, a_ref, *, lane_blocks_per_iter):
    j = pl.program_id(1)
    et, tt = zt_ref.shape
    n_e1 = et // N_KEYS

    @pl.when(j == 0)
    def _():
        acc_ref[...] = jnp.zeros_like(acc_ref)

    zt_ref[...] = lax.dot_general(u_ref[...], xn_ref[...], NT_DIMS, preferred_element_type=F32)

    def lane_group(li, carry):
        for sub in range(lane_blocks_per_iter):
            lb = li * lane_blocks_per_iter + sub
            ln = pl.ds(pl.multiple_of(lb * LANES, LANES), LANES)
            for c in range(n_e1):
                rows = slice(c * N_KEYS, (c + 1) * N_KEYS)
                g = jnp.zeros((N_KEYS, LANES), F32)
                for hh in range(PEER_HEADS):
                    thr = thr_ref[hh, c:c + 1, ln]
                    ar = ar_ref[hh, c:c + 1, ln]
                    g = g + jnp.where(s2_ref[hh, lb] >= thr, bt_ref[hh, lb] * ar, 0.0)
                a_ref[rows, ln] = (_gelu(zt_ref[rows, ln]) * g).astype(a_ref.dtype)
        return carry

    lax.fori_loop(0, tt // (LANES * lane_blocks_per_iter), lane_group, 0)
    acc_ref[...] += jnp.dot(vt_ref[...], a_ref[...], preferred_element_type=F32)

    @pl.when(j == pl.num_programs(1) - 1)
    def _():
        o_ref[...] = h_ref[...] + acc_ref[...].T


def peer_main(h, xn, u, vt, thr, ar, s2_tiles, bt_tiles, *, tt, et, lane_blocks_per_iter):
    t, d = h.shape
    ne = u.shape[0]
    n_e1 = et // N_KEYS
    kern = functools.partial(_peer_main_kernel, lane_blocks_per_iter=lane_blocks_per_iter)
    row_spec = pl.BlockSpec((PEER_HEADS, n_e1, tt), lambda i, j: (0, j, i))
    tok_spec = pl.BlockSpec((PEER_HEADS, tt // LANES, N_KEYS, LANES), lambda i, j: (0, i, 0, 0))
    return pl.pallas_call(
        kern,
        grid=(t // tt, ne // et),
        in_specs=[pl.BlockSpec((tt, d), lambda i, j: (i, 0)),
                  pl.BlockSpec((tt, d), lambda i, j: (i, 0)),
                  pl.BlockSpec((et, d), lambda i, j: (j, 0)),
                  pl.BlockSpec((d, et), lambda i, j: (0, j)),
                  row_spec, row_spec, tok_spec, tok_spec],
        out_specs=pl.BlockSpec((tt, d), lambda i, j: (i, 0)),
        out_shape=jax.ShapeDtypeStruct((t, d), F32),
        scratch_shapes=[pltpu.VMEM((d, tt), F32), pltpu.VMEM((et, tt), F32),
                        pltpu.VMEM((et, tt), BF16)],
        compiler_params=_cparams(("parallel", "arbitrary")),
        name="peer_main",
    )(h, xn, u, vt, thr, ar, s2_tiles, bt_tiles)


def peer_layer(h, gain, wq, keys, u, v):
    t = h.shape[0]
    xn, s1t, s2t = peer_prep(h, gain, wq.astype(BF16), keys, tm=min(512, t))
    thr, ar, s2_tiles, bt_tiles = peer_topk(s1t, s2t, tt=min(256, t))
    return peer_main(h, xn, u.astype(BF16), v.astype(BF16).T, thr, ar, s2_tiles, bt_tiles,
                     tt=min(512, t), et=1024, lane_blocks_per_iter=2)


def _rope_tables(s):
    half = ROT_DIM // 2
    inv = ROPE_THETA ** (-jnp.arange(half, dtype=F32) * 2.0 / ROT_DIM)
    ang = jnp.arange(s).astype(F32)[:, None] * inv[None, :]
    cos, sin = jnp.cos(ang), jnp.sin(ang)
    pad = HEAD - ROT_DIM
    ones = jnp.ones((s, pad), F32)
    zeros = jnp.zeros((s, pad), F32)
    zh = jnp.zeros((s, half), F32)
    ct = jnp.concatenate([cos, cos, ones], axis=1)
    at = jnp.concatenate([-sin, zh, zeros], axis=1)
    bt = jnp.concatenate([zh, sin, zeros], axis=1)
    return ct, at, bt


def _attn_proj_kernel(h_ref, kvg_ref, ag_ref, wkv_ref, wq_ref, kn_ref, qn_ref,
                      ct_ref, at_ref, bt_ref, q_ref, k_ref, v_ref):
    y = _rms(h_ref[...])
    kv = jnp.dot((y * kvg_ref[...]).astype(BF16), wkv_ref[...], preferred_element_type=F32)
    qq = jnp.dot((y * ag_ref[...]).astype(BF16), wq_ref[...], preferred_element_type=F32)
    ct, at, bt = ct_ref[...], at_ref[...], bt_ref[...]
    half = ROT_DIM // 2

    def head_norm_rope(x, gain):
        n = _rms(x) * gain
        return n * ct + pltpu.roll(n, HEAD - half, 1) * at + pltpu.roll(n, half, 1) * bt

    kw = k_ref.shape[1]
    for hd in range(kw // HEAD):
        cs = slice(hd * HEAD, (hd + 1) * HEAD)
        k_ref[:, cs] = head_norm_rope(kv[:, cs], kn_ref[...]).astype(k_ref.dtype)
    v_ref[...] = kv[:, kw:].astype(v_ref.dtype)
    for hd in range(q_ref.shape[1] // HEAD):
        cs = slice(hd * HEAD, (hd + 1) * HEAD)
        q_ref[:, cs] = head_norm_rope(qq[:, cs], qn_ref[...]).astype(q_ref.dtype)


def attn_proj(h, kv_gain, a_gain, wkv, wq, k_norm, q_norm, seq, *, tm):
    t, d = h.shape
    kw = wkv.shape[1] // 2
    qw = wq.shape[1]
    ct, at, bt = _rope_tables(seq)
    ns = seq // tm
    tab = pl.BlockSpec((tm, HEAD), lambda i: (i % ns, 0))
    vec = lambda n: pl.BlockSpec((1, n), lambda i: (0, 0))
    return pl.pallas_call(
        _attn_proj_kernel,
        grid=(t // tm,),
        in_specs=[pl.BlockSpec((tm, d), lambda i: (i, 0)), vec(d), vec(d),
                  pl.BlockSpec(wkv.shape, lambda i: (0, 0)),
                  pl.BlockSpec(wq.shape, lambda i: (0, 0)),
                  vec(HEAD), vec(HEAD), tab, tab, tab],
        out_specs=[pl.BlockSpec((tm, qw), lambda i: (i, 0)),
                   pl.BlockSpec((tm, kw), lambda i: (i, 0)),
                   pl.BlockSpec((tm, kw), lambda i: (i, 0))],
        out_shape=[jax.ShapeDtypeStruct((t, qw), BF16),
                   jax.ShapeDtypeStruct((t, kw), BF16),
                   jax.ShapeDtypeStruct((t, kw), BF16)],
        compiler_params=_cparams(("parallel",)),
        name="attn_proj",
    )(h, kv_gain.reshape(1, d), a_gain.reshape(1, d), wkv, wq,
      k_norm.reshape(1, HEAD), q_norm.reshape(1, HEAD), ct, at, bt)


ATTN_EXP2_SCALE = HEAD ** -0.5 * math.log2(math.e)
MAX_WINDOW = max(w for w, _ in DIL_GROUPS)


def _attn_window_bias(w, dil):
    rows = Q_PER_KV * Q_BLOCK
    tq = w + (np.arange(rows)[:, None] % Q_BLOCK)
    diff = tq - np.arange(w + Q_BLOCK)[None, :]
    valid = (diff >= 0) & (diff <= w) & (diff % dil == 0)
    return np.where(valid, 0.0, -np.inf).astype(np.float32)


def _attn_kernel(q0_ref, q1_ref, q2_ref, k0_ref, k1_ref, k2_ref, v0_ref, v1_ref, v2_ref,
                 b0_ref, b1_ref, b2_ref, o_ref):
    qb = pl.program_id(2)
    t0 = qb * Q_BLOCK
    rows = Q_PER_KV * Q_BLOCK
    groups = list(zip(DIL_GROUPS, (q0_ref, q1_ref, q2_ref), (k0_ref, k1_ref, k2_ref),
                      (v0_ref, v1_ref, v2_ref), (b0_ref, b1_ref, b2_ref)))

    def raw_scores(q_ref, k_ref, ks, span):
        q2 = jnp.concatenate([q_ref[0, :, r * HEAD:(r + 1) * HEAD] for r in range(Q_PER_KV)],
                             axis=0)
        return lax.dot_general(q2, k_ref[0, pl.ds(ks, span), :], NT_DIMS,
                               preferred_element_type=F32)

    def finish(scores, starts):
        m = functools.reduce(jnp.maximum, [jnp.max(s, axis=-1, keepdims=True) for s in scores])
        den = jnp.zeros((rows, 1), F32)
        out = jnp.zeros((rows, HEAD), F32)
        for s, ks, ((w, _), _, _, v_ref, _) in zip(scores, starts, groups):
            p = jnp.exp2((s - m) * ATTN_EXP2_SCALE)
            den = den + jnp.sum(p, axis=-1, keepdims=True)
            out = out + jnp.dot(p.astype(BF16), v_ref[0, pl.ds(ks, w + Q_BLOCK), :],
                                preferred_element_type=F32)
        out = out / den
        for r in range(Q_PER_KV):
            o_ref[0, :, r * HEAD:(r + 1) * HEAD] = (
                out[r * Q_BLOCK:(r + 1) * Q_BLOCK].astype(o_ref.dtype))

    @pl.when(t0 >= MAX_WINDOW)
    def _():
        scores, starts = [], []
        for (w, dil), q_ref, k_ref, _, b_ref in groups:
            ks = pl.multiple_of(t0 - w, Q_BLOCK)
            scores.append(raw_scores(q_ref, k_ref, ks, w + Q_BLOCK) + b_ref[...])
            starts.append(ks)
        finish(scores, starts)

    @pl.when(t0 < MAX_WINDOW)
    def _():
        scores, starts = [], []
        for (w, dil), q_ref, k_ref, _, _ in groups:
            span = w + Q_BLOCK
            ks = pl.multiple_of(jnp.maximum(t0 - w, 0), Q_BLOCK)
            tq = t0 + (lax.broadcasted_iota(jnp.int32, (rows, span), 0) & (Q_BLOCK - 1))
            diff = tq - (ks + lax.broadcasted_iota(jnp.int32, (rows, span), 1))
            valid = (diff >= 0) & (diff <= w) & ((diff & (dil - 1)) == 0)
            scores.append(jnp.where(valid, raw_scores(q_ref, k_ref, ks, span), -jnp.inf))
            starts.append(ks)
        finish(scores, starts)


def dilated_attention(q, k, v):
    b, s, _ = q.shape
    qcols = Q_PER_KV * HEAD

    def qspec(g):
        return pl.BlockSpec((1, Q_BLOCK, qcols), lambda bi, kv, qb, g=g: (bi, qb, g * KV_PER_GROUP + kv))

    def kvspec(g):
        return pl.BlockSpec((1, s, HEAD), lambda bi, kv, qb, g=g: (bi, 0, g * KV_PER_GROUP + kv))

    ng = len(DIL_GROUPS)
    biases = [jnp.asarray(_attn_window_bias(w, dil)) for w, dil in DIL_GROUPS]
    return pl.pallas_call(
        _attn_kernel,
        grid=(b, KV_PER_GROUP, s // Q_BLOCK),
        in_specs=([qspec(g) for g in range(ng)] + [kvspec(g) for g in range(ng)] * 2
                  + [pl.BlockSpec(bias.shape, lambda bi, kv, qb: (0, 0)) for bias in biases]),
        out_specs=pl.BlockSpec((1, Q_BLOCK, qcols), lambda bi, kv, qb: (bi, qb, kv)),
        out_shape=jax.ShapeDtypeStruct((b, s, KV_PER_GROUP * qcols), BF16),
        compiler_params=_cparams(("parallel", "parallel", "arbitrary")),
        name="dilated_attention",
    )(q, q, q, k, k, k, v, v, v, *biases)


def kernel(x, hgrn_norm, hgrn_w_in, hgrn_lb_logits, hgrn_out_norm, hgrn_w_out, kv_norm, w_kv,
           k_norm, attn_norm, w_q, q_norm, w_o, ffn_norm, peer_w_q, peer_sub_keys, peer_u, peer_v):
    b, s, d = x.shape
    t = b * s
    depth = ffn_norm.shape[0]
    n_a = hgrn_norm.shape[0]
    h = x.reshape(t, d)
    tm = min(1024, t)
    for layer in range(depth):
        if layer < n_a:
            y = norm_matmul(h, hgrn_norm[layer], hgrn_w_in[layer].astype(BF16), tm=tm, tn=512)
            o = hgrn_scan(y.reshape(b, s, 4 * d), hgrn_lb_logits, hgrn_out_norm[layer],
                          layer=layer, seq_block=min(1024, s))
            h = matmul_residual(o.reshape(t, d), hgrn_w_out[layer].astype(BF16), h, tm=tm)
        else:
            j = layer - n_a
            qr, kr, vr = attn_proj(h, kv_norm, attn_norm[j], w_kv.astype(BF16),
                                   w_q[j].astype(BF16), k_norm, q_norm[j], s, tm=min(512, s))
            if j == 0:
                k_sh, v_sh = kr.reshape(b, s, -1), vr.reshape(b, s, -1)
            ao = dilated_attention(qr.reshape(b, s, -1), k_sh, v_sh)
            h = matmul_residual(ao.reshape(t, -1), w_o[j].astype(BF16), h, tm=tm)
        h = peer_layer(h, ffn_norm[layer], peer_w_q[layer], peer_sub_keys[layer],
                       peer_u[layer], peer_v[layer])
    return h.reshape(b, s, d)
```

```python
import functools
import math

import numpy as np
import jax
import jax.numpy as jnp
from jax import lax
from jax.experimental import pallas as pl
from jax.experimental.pallas import tpu as pltpu

F32 = jnp.float32
BF16 = jnp.bfloat16
EPS = 1e-6

LANES = 128
HEAD = 128
HG_CHUNK = 128
DIL_GROUPS = ((128, 1), (512, 4), (2048, 16))
KV_PER_GROUP = 2
Q_PER_KV = 2
ROT_DIM = HEAD // 4
ROPE_THETA = 500000.0
Q_BLOCK = 128
PEER_HEADS = 8
PEER_TOPK = 16
N_KEYS = 128
VMEM_LIMIT = 56 * 1024 * 1024

NT_DIMS = (((1,), (1,)), ((), ()))


def _cparams(sem):
    return pltpu.CompilerParams(dimension_semantics=sem, vmem_limit_bytes=VMEM_LIMIT)


def _rms(x):
    return x * lax.rsqrt(jnp.mean(x * x, axis=-1, keepdims=True) + EPS)


def _sigmoid(x):
    return 1.0 / (1.0 + jnp.exp(-x))


def _norm_matmul_kernel(x_ref, g_ref, w_ref, o_ref, xn_ref):
    @pl.when(pl.program_id(1) == 0)
    def _():
        xn_ref[...] = (_rms(x_ref[...]) * g_ref[...]).astype(xn_ref.dtype)

    o_ref[...] = jnp.dot(xn_ref[...], w_ref[...],
                         preferred_element_type=F32).astype(o_ref.dtype)


def norm_matmul(x, gain, w, *, tm, tn, out_dtype=F32):
    t, d = x.shape
    n = w.shape[1]
    return pl.pallas_call(
        _norm_matmul_kernel,
        grid=(t // tm, n // tn),
        in_specs=[pl.BlockSpec((tm, d), lambda i, j: (i, 0)),
                  pl.BlockSpec((1, d), lambda i, j: (0, 0)),
                  pl.BlockSpec((d, tn), lambda i, j: (0, j))],
        out_specs=pl.BlockSpec((tm, tn), lambda i, j: (i, j)),
        out_shape=jax.ShapeDtypeStruct((t, n), out_dtype),
        scratch_shapes=[pltpu.VMEM((tm, d), BF16)],
        compiler_params=_cparams(("parallel", "arbitrary")),
        name="norm_matmul",
    )(x, gain.reshape(1, d), w)


def _matmul_res_kernel(a_ref, w_ref, r_ref, o_ref):
    o_ref[...] = r_ref[...] + jnp.dot(a_ref[...], w_ref[...], preferred_element_type=F32)


def matmul_residual(a, w, res, *, tm):
    t, k = a.shape
    n = w.shape[1]
    return pl.pallas_call(
        _matmul_res_kernel,
        grid=(t // tm,),
        in_specs=[pl.BlockSpec((tm, k), lambda i: (i, 0)),
                  pl.BlockSpec((k, n), lambda i: (0, 0)),
                  pl.BlockSpec((tm, n), lambda i: (i, 0))],
        out_specs=pl.BlockSpec((tm, n), lambda i: (i, 0)),
        out_shape=jax.ShapeDtypeStruct((t, n), F32),
        compiler_params=_cparams(("parallel",)),
        name="matmul_residual",
    )(a, w, res)


def _hgrn_levels(c):
    return int(math.log2(c))


def _hgrn_sum_matrix(c):
    nlev = _hgrn_levels(c)
    m = np.zeros(((nlev + 2) * c, c), np.float32)
    for l in range(nlev):
        h = c >> (l + 1)
        for r in range(c):
            mid = (r // (2 * h)) * 2 * h + h
            if r >= mid:
                m[l * c + r, mid:r + 1] = 1.0
            else:
                m[l * c + r, r + 1:mid] = 1.0
    for r in range(c):
        m[nlev * c + r, :r + 1] = 1.0
        m[(nlev + 1) * c + r, r + 1:] = 1.0
    return m


def _hgrn_kernel(q_ref, f_ref, i_ref, g_ref, lbl_ref, og_ref, m_ref, o_ref, st_ref, *,
                 layer, chunk, n_chunks):
    c = chunk
    nlev = _hgrn_levels(c)

    @pl.when(pl.program_id(2) == 0)
    def _():
        st_ref[...] = jnp.zeros_like(st_ref)

    lg = lbl_ref[...]
    e = jnp.exp(lg - jnp.max(lg, axis=0, keepdims=True))
    lb = jnp.sum(e[:layer + 1], axis=0, keepdims=True) / jnp.sum(e, axis=0, keepdims=True)
    og = og_ref[...]
    row = lax.broadcasted_iota(jnp.int32, (c, c), 0)
    col = lax.broadcasted_iota(jnp.int32, (c, c), 1)

    def body(ci, carry):
        sl = pl.ds(pl.multiple_of(ci * c, c), c)
        q = q_ref[0, sl, :]
        fr = f_ref[0, sl, :]
        v = i_ref[0, sl, :]
        gt = g_ref[0, sl, :]
        qa = q * _sigmoid(q)
        f = lb + (1.0 - lb) * _sigmoid(fr)
        k = 1.0 - f
        logf = jnp.log(f)
        hi = logf.astype(BF16)
        r1 = logf - hi.astype(F32)
        mid = r1.astype(BF16)
        lo = (r1 - mid.astype(F32)).astype(BF16)
        e3 = jnp.dot(m_ref[...], jnp.concatenate([hi, mid, lo], axis=1),
                     preferred_element_type=F32)
        ex = e3[:, :HEAD] + e3[:, HEAD:2 * HEAD] + e3[:, 2 * HEAD:]

        kb = k.astype(BF16)
        s = jnp.where(row == col,
                      lax.dot_general(qa.astype(BF16), kb, NT_DIMS, preferred_element_type=F32),
                      0.0)
        for l in range(nlev):
            sh = nlev - 1 - l
            x = jnp.exp(ex[l * c:(l + 1) * c])
            p = lax.dot_general((qa * x).astype(BF16), (k * x).astype(BF16), NT_DIMS,
                                preferred_element_type=F32)
            mask = (((row >> (sh + 1)) == (col >> (sh + 1)))
                    & (((row >> sh) & 1) == 1) & (((col >> sh) & 1) == 0))
            s = jnp.where(mask, p, s)
        vb = v.astype(BF16)
        intra = jnp.dot(s.astype(BF16), vb, preferred_element_type=F32)
        b_incl = ex[nlev * c:(nlev + 1) * c]
        st = st_ref[...]
        inter = lax.dot_general((qa * jnp.exp(b_incl)).astype(BF16), st.astype(BF16), NT_DIMS,
                                preferred_element_type=F32)
        o = inter + intra
        y = _rms(o) * og * (gt * _sigmoid(gt))
        o_ref[0, sl, :] = y.astype(o_ref.dtype)

        k2 = (k * jnp.exp(ex[(nlev + 1) * c:(nlev + 2) * c])).astype(BF16)
        upd = jnp.dot(v.T.astype(BF16), k2, preferred_element_type=F32)
        st_ref[...] = st * jnp.exp(b_incl[c - 1:c, :]) + upd
        return carry

    lax.fori_loop(0, n_chunks, body, 0, unroll=4)


def hgrn_scan(y, lb_logits, out_gain, *, layer, seq_block):
    b, s, d4 = y.shape
    d = d4 // 4
    nh = d // HEAD
    c = min(HG_CHUNK, seq_block)
    msel = jnp.asarray(_hgrn_sum_matrix(c), BF16)
    nl = lb_logits.shape[0]
    kern = functools.partial(_hgrn_kernel, layer=layer, chunk=c, n_chunks=seq_block // c)

    def part(p):
        return pl.BlockSpec((1, seq_block, HEAD), lambda bi, hi, si, p=p: (bi, si, p * nh + hi))

    return pl.pallas_call(
        kern,
        grid=(b, nh, s // seq_block),
        in_specs=[part(0), part(1), part(2), part(3),
                  pl.BlockSpec((nl, HEAD), lambda bi, hi, si: (0, hi)),
                  pl.BlockSpec((1, HEAD), lambda bi, hi, si: (0, 0)),
                  pl.BlockSpec(msel.shape, lambda bi, hi, si: (0, 0))],
        out_specs=pl.BlockSpec((1, seq_block, HEAD), lambda bi, hi, si: (bi, si, hi)),
        out_shape=jax.ShapeDtypeStruct((b, s, d), BF16),
        scratch_shapes=[pltpu.VMEM((HEAD, HEAD), F32)],
        compiler_params=_cparams(("parallel", "parallel", "arbitrary")),
        name="hgrn_scan",
    )(y, y, y, y, lb_logits, out_gain.reshape(1, HEAD), msel)


def _peer_prep_kernel(h_ref, g_ref, wq_ref, keys_ref, xn_ref, s1_ref, s2_ref):
    xn = (_rms(h_ref[...]) * g_ref[...]).astype(BF16)
    xn_ref[...] = xn
    q = jnp.dot(xn, wq_ref[...], preferred_element_type=F32)
    k0 = keys_ref[0].astype(BF16)
    k1 = keys_ref[1].astype(BF16)
    for hh in range(PEER_HEADS):
        base = hh * 2 * HEAD
        q1 = q[:, base:base + HEAD].astype(BF16)
        q2 = q[:, base + HEAD:base + 2 * HEAD].astype(BF16)
        s1_ref[hh] = lax.dot_general(k0, q1, NT_DIMS, preferred_element_type=F32)
        s2_ref[hh] = lax.dot_general(k1, q2, NT_DIMS, preferred_element_type=F32)


def peer_prep(h, gain, wq, keys, *, tm):
    t, d = h.shape
    nq = wq.shape[1]
    return pl.pallas_call(
        _peer_prep_kernel,
        grid=(t // tm,),
        in_specs=[pl.BlockSpec((tm, d), lambda i: (i, 0)),
                  pl.BlockSpec((1, d), lambda i: (0, 0)),
                  pl.BlockSpec((d, nq), lambda i: (0, 0)),
                  pl.BlockSpec(keys.shape, lambda i: (0, 0, 0))],
        out_specs=[pl.BlockSpec((tm, d), lambda i: (i, 0)),
                   pl.BlockSpec((PEER_HEADS, N_KEYS, tm), lambda i: (0, 0, i)),
                   pl.BlockSpec((PEER_HEADS, N_KEYS, tm), lambda i: (0, 0, i))],
        out_shape=[jax.ShapeDtypeStruct((t, d), BF16),
                   jax.ShapeDtypeStruct((PEER_HEADS, N_KEYS, t), F32),
                   jax.ShapeDtypeStruct((PEER_HEADS, N_KEYS, t), F32)],
        compiler_params=_cparams(("parallel",)),
        name="peer_prep",
    )(h, gain.reshape(1, d), wq, keys)


def _top_values(x, n, dst_ref, with_rank=False):
    cur = x
    rank = jnp.full(x.shape, float(n), F32) if with_rank else None
    for r in range(n):
        m = jnp.max(cur, axis=0, keepdims=True)
        dst_ref[r:r + 1, :] = m
        hit = cur == m
        if with_rank:
            rank = jnp.where(hit, float(r), rank)
        cur = jnp.where(hit, -jnp.inf, cur)
    return rank


def _peer_topk_kernel(s1_ref, s2_ref, cnt_ref, ar_ref, rk_ref, bt_ref, a_ref, b_ref):
    k = PEER_TOPK
    sub = 8
    rid = lax.broadcasted_iota(jnp.int32, (sub, a_ref.shape[1]), 0)

    def head(hh, carry):
        s1 = s1_ref[hh]
        s2 = s2_ref[hh]
        _top_values(s1, k, a_ref)
        rank2 = _top_values(s2, k, b_ref, with_rank=True)
        a_lo, a_hi = a_ref[0:sub, :], a_ref[sub:k, :]
        b_hi = b_ref[sub:k, :]
        a0 = a_ref[0:1, :]
        slabs = [a_lo + b_ref[0:1, :], a_hi + b_ref[0:1, :], b_hi + a0]
        for j in range(1, sub):
            lim = k // (j + 1)
            slabs.append(jnp.where(rid < lim, a_lo + b_ref[j:j + 1, :], -jnp.inf))
        cur = slabs
        tau = None
        for r in range(k):
            m = functools.reduce(jnp.maximum, cur)
            m = jnp.max(m, axis=0, keepdims=True)
            tau = m
            if r + 1 < k:
                cur = [jnp.where(x == m, -jnp.inf, x) for x in cur]
        m1 = a0
        m2 = b_ref[0:1, :]
        top = m1 + m2
        z = None
        for x in slabs:
            zz = jnp.sum(jnp.where(x >= tau, jnp.exp(x - top), 0.0), axis=0, keepdims=True)
            z = zz if z is None else z + zz
        cnt = jnp.zeros(s1.shape, F32)
        for j in range(k):
            cnt = cnt + jnp.where(s1 + b_ref[j:j + 1, :] >= tau, 1.0, 0.0)
        cnt_ref[hh] = cnt
        ar_ref[hh] = jnp.exp(s1 - m1) / z
        bt = jnp.exp(s2 - m2)
        for lb in range(s2.shape[1] // LANES):
            ln = slice(lb * LANES, (lb + 1) * LANES)
            rk_ref[hh, lb] = pltpu.bitcast(rank2[:, ln].astype(BF16), jnp.uint32)
            bt_ref[hh, lb] = pltpu.bitcast(bt[:, ln].astype(BF16), jnp.uint32)
        return carry

    lax.fori_loop(0, PEER_HEADS, head, 0)


def peer_topk(s1t, s2t, *, tt):
    nh, nk, t = s1t.shape
    spec = pl.BlockSpec((nh, nk, tt), lambda i: (0, 0, i))
    shp = jax.ShapeDtypeStruct((nh, nk, t), F32)
    tspec = pl.BlockSpec((nh, tt // LANES, nk // 2, LANES), lambda i: (0, i, 0, 0))
    tshp = jax.ShapeDtypeStruct((nh, t // LANES, nk // 2, LANES), jnp.uint32)
    return pl.pallas_call(
        _peer_topk_kernel,
        grid=(t // tt,),
        in_specs=[spec, spec],
        out_specs=[spec, spec, tspec, tspec],
        out_shape=[shp, shp, tshp, tshp],
        scratch_shapes=[pltpu.VMEM((PEER_TOPK, tt), F32), pltpu.VMEM((PEER_TOPK, tt), F32)],
        compiler_params=_cparams(("parallel",)),
        name="peer_topk",
    )(s1t, s2t)


GELU_C0 = math.sqrt(2.0 / math.pi)
GELU_C1 = GELU_C0 * 0.044715


def _gelu(z):
    return 0.5 * z * (1.0 + jnp.tanh(z * (GELU_C0 + GELU_C1 * (z * z))))


def _peer_main_kernel(h_ref, xn_ref, u_ref, vt_ref, cnt_ref, ar_ref, rk_ref, bt_ref, o_ref,
                      acc_ref, zt_ref, a_ref, *, lane_blocks_per_iter):
    j = pl.program_id(1)
    et, tt = zt_ref.shape
    n_e1 = et // N_KEYS

    @pl.when(j == 0)
    def _():
        acc_ref[...] = jnp.zeros_like(acc_ref)

    zt_ref[...] = lax.dot_general(u_ref[...], xn_ref[...], NT_DIMS, preferred_element_type=F32)

    def lane_group(li, carry):
        for sub in range(lane_blocks_per_iter):
            lb = li * lane_blocks_per_iter + sub
            ln = pl.ds(pl.multiple_of(lb * LANES, LANES), LANES)
            for c in range(n_e1):
                rows = slice(c * N_KEYS, (c + 1) * N_KEYS)
                g = jnp.zeros((N_KEYS, LANES), BF16)
                for hh in range(PEER_HEADS):
                    cnt = cnt_ref[hh, c:c + 1, ln].astype(BF16)
                    ar = ar_ref[hh, c:c + 1, ln].astype(BF16)
                    rk = pltpu.bitcast(rk_ref[hh, lb], BF16)
                    bt = pltpu.bitcast(bt_ref[hh, lb], BF16)
                    g = g + jnp.where(rk < cnt, bt * ar, 0)
                a_ref[rows, ln] = _gelu(zt_ref[rows, ln]).astype(BF16) * g
        return carry

    lax.fori_loop(0, tt // (LANES * lane_blocks_per_iter), lane_group, 0)
    acc_ref[...] += jnp.dot(vt_ref[...], a_ref[...], preferred_element_type=F32)

    @pl.when(j == pl.num_programs(1) - 1)
    def _():
        o_ref[...] = h_ref[...] + acc_ref[...].T


def peer_main(h, xn, u, vt, cnt, ar, rank_tiles, bt_tiles, *, tt, et, lane_blocks_per_iter):
    t, d = h.shape
    ne = u.shape[0]
    n_e1 = et // N_KEYS
    kern = functools.partial(_peer_main_kernel, lane_blocks_per_iter=lane_blocks_per_iter)
    row_spec = pl.BlockSpec((PEER_HEADS, n_e1, tt), lambda i, j: (0, j, i))
    tok_spec = pl.BlockSpec((PEER_HEADS, tt // LANES, N_KEYS // 2, LANES),
                            lambda i, j: (0, i, 0, 0))
    return pl.pallas_call(
        kern,
        grid=(t // tt, ne // et),
        in_specs=[pl.BlockSpec((tt, d), lambda i, j: (i, 0)),
                  pl.BlockSpec((tt, d), lambda i, j: (i, 0)),
                  pl.BlockSpec((et, d), lambda i, j: (j, 0)),
                  pl.BlockSpec((d, et), lambda i, j: (0, j)),
                  row_spec, row_spec, tok_spec, tok_spec],
        out_specs=pl.BlockSpec((tt, d), lambda i, j: (i, 0)),
        out_shape=jax.ShapeDtypeStruct((t, d), F32),
        scratch_shapes=[pltpu.VMEM((d, tt), F32), pltpu.VMEM((et, tt), F32),
                        pltpu.VMEM((et, tt), BF16)],
        compiler_params=_cparams(("parallel", "arbitrary")),
        name="peer_main",
    )(h, xn, u, vt, cnt, ar, rank_tiles, bt_tiles)


def peer_layer(h, gain, wq, keys, u, v):
    t = h.shape[0]
    xn, s1t, s2t = peer_prep(h, gain, wq.astype(BF16), keys, tm=min(512, t))
    cnt, ar, rank_tiles, bt_tiles = peer_topk(s1t, s2t, tt=min(256, t))
    return peer_main(h, xn, u.astype(BF16), v.astype(BF16).T, cnt, ar, rank_tiles, bt_tiles,
                     tt=min(512, t), et=1024, lane_blocks_per_iter=2)


def _rope_tables(s):
    half = ROT_DIM // 2
    inv = ROPE_THETA ** (-jnp.arange(half, dtype=F32) * 2.0 / ROT_DIM)
    ang = jnp.arange(s).astype(F32)[:, None] * inv[None, :]
    cos, sin = jnp.cos(ang), jnp.sin(ang)
    pad = HEAD - ROT_DIM
    ones = jnp.ones((s, pad), F32)
    zeros = jnp.zeros((s, pad), F32)
    zh = jnp.zeros((s, half), F32)
    ct = jnp.concatenate([cos, cos, ones], axis=1)
    at = jnp.concatenate([-sin, zh, zeros], axis=1)
    bt = jnp.concatenate([zh, sin, zeros], axis=1)
    return ct, at, bt


def _attn_proj_kernel(h_ref, kvg_ref, ag_ref, wkv_ref, wq_ref, kn_ref, qn_ref,
                      ct_ref, at_ref, bt_ref, q_ref, k_ref, v_ref):
    y = _rms(h_ref[...])
    kv = jnp.dot((y * kvg_ref[...]).astype(BF16), wkv_ref[...], preferred_element_type=F32)
    qq = jnp.dot((y * ag_ref[...]).astype(BF16), wq_ref[...], preferred_element_type=F32)
    ct, at, bt = ct_ref[...], at_ref[...], bt_ref[...]
    half = ROT_DIM // 2

    def head_norm_rope(x, gain):
        n = _rms(x) * gain
        return n * ct + pltpu.roll(n, HEAD - half, 1) * at + pltpu.roll(n, half, 1) * bt

    kw = k_ref.shape[1]
    for hd in range(kw // HEAD):
        cs = slice(hd * HEAD, (hd + 1) * HEAD)
        k_ref[:, cs] = head_norm_rope(kv[:, cs], kn_ref[...]).astype(k_ref.dtype)
    v_ref[...] = kv[:, kw:].astype(v_ref.dtype)
    for hd in range(q_ref.shape[1] // HEAD):
        cs = slice(hd * HEAD, (hd + 1) * HEAD)
        q_ref[:, cs] = head_norm_rope(qq[:, cs], qn_ref[...]).astype(q_ref.dtype)


def attn_proj(h, kv_gain, a_gain, wkv, wq, k_norm, q_norm, seq, *, tm):
    t, d = h.shape
    kw = wkv.shape[1] // 2
    qw = wq.shape[1]
    ct, at, bt = _rope_tables(seq)
    ns = seq // tm
    tab = pl.BlockSpec((tm, HEAD), lambda i: (i % ns, 0))
    vec = lambda n: pl.BlockSpec((1, n), lambda i: (0, 0))
    return pl.pallas_call(
        _attn_proj_kernel,
        grid=(t // tm,),
        in_specs=[pl.BlockSpec((tm, d), lambda i: (i, 0)), vec(d), vec(d),
                  pl.BlockSpec(wkv.shape, lambda i: (0, 0)),
                  pl.BlockSpec(wq.shape, lambda i: (0, 0)),
                  vec(HEAD), vec(HEAD), tab, tab, tab],
        out_specs=[pl.BlockSpec((tm, qw), lambda i: (i, 0)),
                   pl.BlockSpec((tm, kw), lambda i: (i, 0)),
                   pl.BlockSpec((tm, kw), lambda i: (i, 0))],
        out_shape=[jax.ShapeDtypeStruct((t, qw), BF16),
                   jax.ShapeDtypeStruct((t, kw), BF16),
                   jax.ShapeDtypeStruct((t, kw), BF16)],
        compiler_params=_cparams(("parallel",)),
        name="attn_proj",
    )(h, kv_gain.reshape(1, d), a_gain.reshape(1, d), wkv, wq,
      k_norm.reshape(1, HEAD), q_norm.reshape(1, HEAD), ct, at, bt)


ATTN_EXP2_SCALE = HEAD ** -0.5 * math.log2(math.e)
MAX_WINDOW = max(w for w, _ in DIL_GROUPS)


def _attn_window_bias(w, dil):
    rows = Q_PER_KV * Q_BLOCK
    tq = w + (np.arange(rows)[:, None] % Q_BLOCK)
    diff = tq - np.arange(w + Q_BLOCK)[None, :]
    valid = (diff >= 0) & (diff <= w) & (diff % dil == 0)
    return np.where(valid, 0.0, -np.inf).astype(np.float32)


def _attn_kernel(q0_ref, q1_ref, q2_ref, k0_ref, k1_ref, k2_ref, v0_ref, v1_ref, v2_ref,
                 b0_ref, b1_ref, b2_ref, o_ref):
    qb = pl.program_id(2)
    t0 = qb * Q_BLOCK
    rows = Q_PER_KV * Q_BLOCK
    groups = list(zip(DIL_GROUPS, (q0_ref, q1_ref, q2_ref), (k0_ref, k1_ref, k2_ref),
                      (v0_ref, v1_ref, v2_ref), (b0_ref, b1_ref, b2_ref)))

    def raw_scores(q_ref, k_ref, ks, span):
        q2 = jnp.concatenate([q_ref[0, :, r * HEAD:(r + 1) * HEAD] for r in range(Q_PER_KV)],
                             axis=0)
        return lax.dot_general(q2, k_ref[0, pl.ds(ks, span), :], NT_DIMS,
                               preferred_element_type=F32)

    def finish(scores, starts):
        m = functools.reduce(jnp.maximum, [jnp.max(s, axis=-1, keepdims=True) for s in scores])
        den = jnp.zeros((rows, 1), F32)
        out = jnp.zeros((rows, HEAD), F32)
        for s, ks, ((w, _), _, _, v_ref, _) in zip(scores, starts, groups):
            p = jnp.exp2((s - m) * ATTN_EXP2_SCALE)
            den = den + jnp.sum(p, axis=-1, keepdims=True)
            out = out + jnp.dot(p.astype(BF16), v_ref[0, pl.ds(ks, w + Q_BLOCK), :],
                                preferred_element_type=F32)
        out = out / den
        for r in range(Q_PER_KV):
            o_ref[0, :, r * HEAD:(r + 1) * HEAD] = (
                out[r * Q_BLOCK:(r + 1) * Q_BLOCK].astype(o_ref.dtype))

    @pl.when(t0 >= MAX_WINDOW)
    def _():
        scores, starts = [], []
        for (w, dil), q_ref, k_ref, _, b_ref in groups:
            ks = pl.multiple_of(t0 - w, Q_BLOCK)
            scores.append(raw_scores(q_ref, k_ref, ks, w + Q_BLOCK) + b_ref[...])
            starts.append(ks)
        finish(scores, starts)

    @pl.when(t0 < MAX_WINDOW)
    def _():
        scores, starts = [], []
        for (w, dil), q_ref, k_ref, _, _ in groups:
            span = w + Q_BLOCK
            ks = pl.multiple_of(jnp.maximum(t0 - w, 0), Q_BLOCK)
            tq = t0 + (lax.broadcasted_iota(jnp.int32, (rows, span), 0) & (Q_BLOCK - 1))
            diff = tq - (ks + lax.broadcasted_iota(jnp.int32, (rows, span), 1))
            valid = (diff >= 0) & (diff <= w) & ((diff & (dil - 1)) == 0)
            scores.append(jnp.where(valid, raw_scores(q_ref, k_ref, ks, span), -jnp.inf))
            starts.append(ks)
        finish(scores, starts)


def dilated_attention(q, k, v):
    b, s, _ = q.shape
    qcols = Q_PER_KV * HEAD

    def qspec(g):
        return pl.BlockSpec((1, Q_BLOCK, qcols), lambda bi, kv, qb, g=g: (bi, qb, g * KV_PER_GROUP + kv))

    def kvspec(g):
        return pl.BlockSpec((1, s, HEAD), lambda bi, kv, qb, g=g: (bi, 0, g * KV_PER_GROUP + kv))

    ng = len(DIL_GROUPS)
    biases = [jnp.asarray(_attn_window_bias(w, dil)) for w, dil in DIL_GROUPS]
    return pl.pallas_call(
        _attn_kernel,
        grid=(b, KV_PER_GROUP, s // Q_BLOCK),
        in_specs=([qspec(g) for g in range(ng)] + [kvspec(g) for g in range(ng)] * 2
                  + [pl.BlockSpec(bias.shape, lambda bi, kv, qb: (0, 0)) for bias in biases]),
        out_specs=pl.BlockSpec((1, Q_BLOCK, qcols), lambda bi, kv, qb: (bi, qb, kv)),
        out_shape=jax.ShapeDtypeStruct((b, s, KV_PER_GROUP * qcols), BF16),
        compiler_params=_cparams(("parallel", "parallel", "arbitrary")),
        name="dilated_attention",
    )(q, q, q, k, k, k, v, v, v, *biases)


def kernel(x, hgrn_norm, hgrn_w_in, hgrn_lb_logits, hgrn_out_norm, hgrn_w_out, kv_norm, w_kv,
           k_norm, attn_norm, w_q, q_norm, w_o, ffn_norm, peer_w_q, peer_sub_keys, peer_u, peer_v):
    b, s, d = x.shape
    t = b * s
    depth = ffn_norm.shape[0]
    n_a = hgrn_norm.shape[0]
    h = x.reshape(t, d)
    tm = min(1024, t)
    for layer in range(depth):
        if layer < n_a:
            y = norm_matmul(h, hgrn_norm[layer], hgrn_w_in[layer].astype(BF16), tm=tm, tn=512)
            o = hgrn_scan(y.reshape(b, s, 4 * d), hgrn_lb_logits, hgrn_out_norm[layer],
                          layer=layer, seq_block=min(1024, s))
            h = matmul_residual(o.reshape(t, d), hgrn_w_out[layer].astype(BF16), h, tm=tm)
        else:
            j = layer - n_a
            qr, kr, vr = attn_proj(h, kv_norm, attn_norm[j], w_kv.astype(BF16),
                                   w_q[j].astype(BF16), k_norm, q_norm[j], s, tm=min(512, s))
            if j == 0:
                k_sh, v_sh = kr.reshape(b, s, -1), vr.reshape(b, s, -1)
            ao = dilated_attention(qr.reshape(b, s, -1), k_sh, v_sh)
            h = matmul_residual(ao.reshape(t, -1), w_o[j].astype(BF16), h, tm=tm)
        h = peer_layer(h, ffn_norm[layer], peer_w_q[layer], peer_sub_keys[layer],
                       peer_u[layer], peer_v[layer])
    return h.reshape(b, s, d)
```

```python
import functools
import math

import numpy as np
import jax
import jax.numpy as jnp
from jax import lax
from jax.experimental import pallas as pl
from jax.experimental.pallas import tpu as pltpu

F32 = jnp.float32
BF16 = jnp.bfloat16
EPS = 1e-6

LANES = 128
HEAD = 128
HG_CHUNK = 128
DIL_GROUPS = ((128, 1), (512, 4), (2048, 16))
KV_PER_GROUP = 2
Q_PER_KV = 2
ROT_DIM = HEAD // 4
ROPE_THETA = 500000.0
Q_BLOCK = 128
PEER_HEADS = 8
PEER_TOPK = 16
N_KEYS = 128
VMEM_LIMIT = 56 * 1024 * 1024

NT_DIMS = (((1,), (1,)), ((), ()))


def _cparams(sem):
    return pltpu.CompilerParams(dimension_semantics=sem, vmem_limit_bytes=VMEM_LIMIT)


def _rms(x):
    return x * lax.rsqrt(jnp.mean(x * x, axis=-1, keepdims=True) + EPS)


def _sigmoid(x):
    return 1.0 / (1.0 + jnp.exp(-x))


def _norm_matmul_kernel(x_ref, g_ref, w_ref, o_ref, xn_ref):
    @pl.when(pl.program_id(1) == 0)
    def _():
        xn_ref[...] = (_rms(x_ref[...]) * g_ref[...]).astype(xn_ref.dtype)

    o_ref[...] = jnp.dot(xn_ref[...], w_ref[...],
                         preferred_element_type=F32).astype(o_ref.dtype)


def norm_matmul(x, gain, w, *, tm, tn, out_dtype=F32):
    t, d = x.shape
    n = w.shape[1]
    return pl.pallas_call(
        _norm_matmul_kernel,
        grid=(t // tm, n // tn),
        in_specs=[pl.BlockSpec((tm, d), lambda i, j: (i, 0)),
                  pl.BlockSpec((1, d), lambda i, j: (0, 0)),
                  pl.BlockSpec((d, tn), lambda i, j: (0, j))],
        out_specs=pl.BlockSpec((tm, tn), lambda i, j: (i, j)),
        out_shape=jax.ShapeDtypeStruct((t, n), out_dtype),
        scratch_shapes=[pltpu.VMEM((tm, d), BF16)],
        compiler_params=_cparams(("parallel", "arbitrary")),
        name="norm_matmul",
    )(x, gain.reshape(1, d), w)


def _matmul_res_kernel(a_ref, w_ref, r_ref, o_ref):
    o_ref[...] = r_ref[...] + jnp.dot(a_ref[...], w_ref[...], preferred_element_type=F32)


def matmul_residual(a, w, res, *, tm):
    t, k = a.shape
    n = w.shape[1]
    return pl.pallas_call(
        _matmul_res_kernel,
        grid=(t // tm,),
        in_specs=[pl.BlockSpec((tm, k), lambda i: (i, 0)),
                  pl.BlockSpec((k, n), lambda i: (0, 0)),
                  pl.BlockSpec((tm, n), lambda i: (i, 0))],
        out_specs=pl.BlockSpec((tm, n), lambda i: (i, 0)),
        out_shape=jax.ShapeDtypeStruct((t, n), F32),
        compiler_params=_cparams(("parallel",)),
        name="matmul_residual",
    )(a, w, res)


def _hgrn_levels(c):
    return int(math.log2(c))


def _hgrn_sum_matrix(c):
    nlev = _hgrn_levels(c)
    m = np.zeros(((nlev + 2) * c, c), np.float32)
    for l in range(nlev):
        h = c >> (l + 1)
        for r in range(c):
            mid = (r // (2 * h)) * 2 * h + h
            if r >= mid:
                m[l * c + r, mid:r + 1] = 1.0
            else:
                m[l * c + r, r + 1:mid] = 1.0
    for r in range(c):
        m[nlev * c + r, :r + 1] = 1.0
        m[(nlev + 1) * c + r, r + 1:] = 1.0
    return m


def _hgrn_kernel(q_ref, f_ref, i_ref, g_ref, lbl_ref, og_ref, m_ref, o_ref, st_ref, *,
                 layer, chunk, n_chunks):
    c = chunk
    nlev = _hgrn_levels(c)

    @pl.when(pl.program_id(2) == 0)
    def _():
        st_ref[...] = jnp.zeros_like(st_ref)

    lg = lbl_ref[...]
    e = jnp.exp(lg - jnp.max(lg, axis=0, keepdims=True))
    lb = jnp.sum(e[:layer + 1], axis=0, keepdims=True) / jnp.sum(e, axis=0, keepdims=True)
    og = og_ref[...]
    row = lax.broadcasted_iota(jnp.int32, (c, c), 0)
    col = lax.broadcasted_iota(jnp.int32, (c, c), 1)

    def body(ci, carry):
        sl = pl.ds(pl.multiple_of(ci * c, c), c)
        q = q_ref[0, sl, :]
        fr = f_ref[0, sl, :]
        v = i_ref[0, sl, :]
        gt = g_ref[0, sl, :]
        qa = q * _sigmoid(q)
        f = lb + (1.0 - lb) * _sigmoid(fr)
        k = 1.0 - f
        logf = jnp.log(f)
        hi = logf.astype(BF16)
        r1 = logf - hi.astype(F32)
        mid = r1.astype(BF16)
        lo = (r1 - mid.astype(F32)).astype(BF16)
        e3 = jnp.dot(m_ref[...], jnp.concatenate([hi, mid, lo], axis=1),
                     preferred_element_type=F32)
        ex = e3[:, :HEAD] + e3[:, HEAD:2 * HEAD] + e3[:, 2 * HEAD:]

        kb = k.astype(BF16)
        s = jnp.where(row == col,
                      lax.dot_general(qa.astype(BF16), kb, NT_DIMS, preferred_element_type=F32),
                      0.0)
        for l in range(nlev):
            sh = nlev - 1 - l
            x = jnp.exp(ex[l * c:(l + 1) * c])
            p = lax.dot_general((qa * x).astype(BF16), (k * x).astype(BF16), NT_DIMS,
                                preferred_element_type=F32)
            mask = (((row >> (sh + 1)) == (col >> (sh + 1)))
                    & (((row >> sh) & 1) == 1) & (((col >> sh) & 1) == 0))
            s = jnp.where(mask, p, s)
        vb = v.astype(BF16)
        intra = jnp.dot(s.astype(BF16), vb, preferred_element_type=F32)
        b_incl = ex[nlev * c:(nlev + 1) * c]
        st = st_ref[...]
        inter = lax.dot_general((qa * jnp.exp(b_incl)).astype(BF16), st.astype(BF16), NT_DIMS,
                                preferred_element_type=F32)
        o = inter + intra
        y = _rms(o) * og * (gt * _sigmoid(gt))
        o_ref[0, sl, :] = y.astype(o_ref.dtype)

        k2 = (k * jnp.exp(ex[(nlev + 1) * c:(nlev + 2) * c])).astype(BF16)
        upd = jnp.dot(v.T.astype(BF16), k2, preferred_element_type=F32)
        st_ref[...] = st * jnp.exp(b_incl[c - 1:c, :]) + upd
        return carry

    lax.fori_loop(0, n_chunks, body, 0, unroll=4)


def hgrn_scan(y, lb_logits, out_gain, *, layer, seq_block):
    b, s, d4 = y.shape
    d = d4 // 4
    nh = d // HEAD
    c = min(HG_CHUNK, seq_block)
    msel = jnp.asarray(_hgrn_sum_matrix(c), BF16)
    nl = lb_logits.shape[0]
    kern = functools.partial(_hgrn_kernel, layer=layer, chunk=c, n_chunks=seq_block // c)

    def part(p):
        return pl.BlockSpec((1, seq_block, HEAD), lambda bi, hi, si, p=p: (bi, si, p * nh + hi))

    return pl.pallas_call(
        kern,
        grid=(b, nh, s // seq_block),
        in_specs=[part(0), part(1), part(2), part(3),
                  pl.BlockSpec((nl, HEAD), lambda bi, hi, si: (0, hi)),
                  pl.BlockSpec((1, HEAD), lambda bi, hi, si: (0, 0)),
                  pl.BlockSpec(msel.shape, lambda bi, hi, si: (0, 0))],
        out_specs=pl.BlockSpec((1, seq_block, HEAD), lambda bi, hi, si: (bi, si, hi)),
        out_shape=jax.ShapeDtypeStruct((b, s, d), BF16),
        scratch_shapes=[pltpu.VMEM((HEAD, HEAD), F32)],
        compiler_params=_cparams(("parallel", "parallel", "arbitrary")),
        name="hgrn_scan",
    )(y, y, y, y, lb_logits, out_gain.reshape(1, HEAD), msel)


def _peer_prep_kernel(h_ref, g_ref, wq_ref, keys_ref, xn_ref, s1_ref, s2_ref):
    xn = (_rms(h_ref[...]) * g_ref[...]).astype(BF16)
    xn_ref[...] = xn
    q = jnp.dot(xn, wq_ref[...], preferred_element_type=F32)
    k0 = keys_ref[0].astype(BF16)
    k1 = keys_ref[1].astype(BF16)
    for hh in range(PEER_HEADS):
        base = hh * 2 * HEAD
        q1 = q[:, base:base + HEAD].astype(BF16)
        q2 = q[:, base + HEAD:base + 2 * HEAD].astype(BF16)
        s1_ref[hh] = lax.dot_general(k0, q1, NT_DIMS, preferred_element_type=F32)
        s2_ref[hh] = lax.dot_general(k1, q2, NT_DIMS, preferred_element_type=F32)


def peer_prep(h, gain, wq, keys, *, tm):
    t, d = h.shape
    nq = wq.shape[1]
    return pl.pallas_call(
        _peer_prep_kernel,
        grid=(t // tm,),
        in_specs=[pl.BlockSpec((tm, d), lambda i: (i, 0)),
                  pl.BlockSpec((1, d), lambda i: (0, 0)),
                  pl.BlockSpec((d, nq), lambda i: (0, 0)),
                  pl.BlockSpec(keys.shape, lambda i: (0, 0, 0))],
        out_specs=[pl.BlockSpec((tm, d), lambda i: (i, 0)),
                   pl.BlockSpec((PEER_HEADS, N_KEYS, tm), lambda i: (0, 0, i)),
                   pl.BlockSpec((PEER_HEADS, N_KEYS, tm), lambda i: (0, 0, i))],
        out_shape=[jax.ShapeDtypeStruct((t, d), BF16),
                   jax.ShapeDtypeStruct((PEER_HEADS, N_KEYS, t), F32),
                   jax.ShapeDtypeStruct((PEER_HEADS, N_KEYS, t), F32)],
        compiler_params=_cparams(("parallel",)),
        name="peer_prep",
    )(h, gain.reshape(1, d), wq, keys)


def _top_values(x, n, dst_ref, with_rank=False):
    cur = x
    rank = jnp.full(x.shape, float(n), F32) if with_rank else None
    for r in range(n):
        m = jnp.max(cur, axis=0, keepdims=True)
        dst_ref[r:r + 1, :] = m
        hit = cur == m
        if with_rank:
            rank = jnp.where(hit, float(r), rank)
        cur = jnp.where(hit, -jnp.inf, cur)
    return rank


def _peer_topk_kernel(s1_ref, s2_ref, cnt_ref, ar_ref, rk_ref, bt_ref, a_ref, b_ref):
    k = PEER_TOPK
    sub = 8
    rid = lax.broadcasted_iota(jnp.int32, (sub, a_ref.shape[1]), 0)

    def head(hh, carry):
        s1 = s1_ref[hh]
        s2 = s2_ref[hh]
        _top_values(s1, k, a_ref)
        rank2 = _top_values(s2, k, b_ref, with_rank=True)
        a_lo, a_hi = a_ref[0:sub, :], a_ref[sub:k, :]
        b_hi = b_ref[sub:k, :]
        a0 = a_ref[0:1, :]
        slabs = [a_lo + b_ref[0:1, :], a_hi + b_ref[0:1, :], b_hi + a0]
        for j in range(1, sub):
            lim = k // (j + 1)
            slabs.append(jnp.where(rid < lim, a_lo + b_ref[j:j + 1, :], -jnp.inf))
        cur = slabs
        tau = None
        for r in range(k):
            m = functools.reduce(jnp.maximum, cur)
            m = jnp.max(m, axis=0, keepdims=True)
            tau = m
            if r + 1 < k:
                cur = [jnp.where(x == m, -jnp.inf, x) for x in cur]
        m1 = a0
        m2 = b_ref[0:1, :]
        top = m1 + m2
        z = None
        for x in slabs:
            zz = jnp.sum(jnp.where(x >= tau, jnp.exp(x - top), 0.0), axis=0, keepdims=True)
            z = zz if z is None else z + zz
        cnt = jnp.zeros(s1.shape, F32)
        for j in range(k):
            cnt = cnt + jnp.where(s1 + b_ref[j:j + 1, :] >= tau, 1.0, 0.0)
        cnt_ref[hh] = cnt
        ar_ref[hh] = jnp.exp(s1 - m1) / z
        bt = jnp.exp(s2 - m2)
        for lb in range(s2.shape[1] // LANES):
            ln = slice(lb * LANES, (lb + 1) * LANES)
            rk_ref[hh, lb] = pltpu.bitcast(rank2[:, ln].astype(BF16), jnp.uint32)
            bt_ref[hh, lb] = pltpu.bitcast(bt[:, ln].astype(BF16), jnp.uint32)
        return carry

    lax.fori_loop(0, PEER_HEADS, head, 0)


def peer_topk(s1t, s2t, *, tt):
    nh, nk, t = s1t.shape
    spec = pl.BlockSpec((nh, nk, tt), lambda i: (0, 0, i))
    shp = jax.ShapeDtypeStruct((nh, nk, t), F32)
    tspec = pl.BlockSpec((nh, tt // LANES, nk // 2, LANES), lambda i: (0, i, 0, 0))
    tshp = jax.ShapeDtypeStruct((nh, t // LANES, nk // 2, LANES), jnp.uint32)
    return pl.pallas_call(
        _peer_topk_kernel,
        grid=(t // tt,),
        in_specs=[spec, spec],
        out_specs=[spec, spec, tspec, tspec],
        out_shape=[shp, shp, tshp, tshp],
        scratch_shapes=[pltpu.VMEM((PEER_TOPK, tt), F32), pltpu.VMEM((PEER_TOPK, tt), F32)],
        compiler_params=_cparams(("parallel",)),
        name="peer_topk",
    )(s1t, s2t)


GELU_C0 = math.sqrt(2.0 / math.pi)
GELU_C1 = GELU_C0 * 0.044715


def _gelu(z):
    return 0.5 * z * (1.0 + jnp.tanh(z * (GELU_C0 + GELU_C1 * (z * z))))


def _peer_main_kernel(h_ref, xn_ref, u_ref, vt_ref, cnt_ref, ar_ref, rk_ref, bt_ref, o_ref,
                      acc_ref, zt_ref, a_ref, *, n_blocks, blocks_per_tile, lane_blocks_per_iter):
    g_step = pl.program_id(0)
    et, tt = zt_ref.shape
    n_e1 = et // N_KEYS
    jb = jnp.maximum(g_step - 1, 0) % blocks_per_tile

    @pl.when(g_step == 0)
    def _():
        a_ref[...] = jnp.zeros_like(a_ref)

    @pl.when(jb == 0)
    def _():
        acc_ref[...] = jnp.zeros_like(acc_ref)

    acc_ref[...] += jnp.dot(vt_ref[...], a_ref[...], preferred_element_type=F32)
    zt_ref[...] = lax.dot_general(u_ref[...], xn_ref[...], NT_DIMS, preferred_element_type=F32)

    @pl.when((g_step >= 1) & (jb == blocks_per_tile - 1))
    def _():
        o_ref[...] = h_ref[...] + acc_ref[...].T

    def lane_group(li, carry):
        for sub in range(lane_blocks_per_iter):
            lb = li * lane_blocks_per_iter + sub
            ln = pl.ds(pl.multiple_of(lb * LANES, LANES), LANES)
            for c in range(n_e1):
                rows = slice(c * N_KEYS, (c + 1) * N_KEYS)
                g = jnp.zeros((N_KEYS, LANES), BF16)
                for hh in range(PEER_HEADS):
                    cnt = cnt_ref[hh, c:c + 1, ln].astype(BF16)
                    ar = ar_ref[hh, c:c + 1, ln].astype(BF16)
                    rk = pltpu.bitcast(rk_ref[hh, lb], BF16)
                    bt = pltpu.bitcast(bt_ref[hh, lb], BF16)
                    g = g + jnp.where(rk < cnt, bt * ar, 0)
                a_ref[rows, ln] = _gelu(zt_ref[rows, ln]).astype(BF16) * g
        return carry

    lax.fori_loop(0, tt // (LANES * lane_blocks_per_iter), lane_group, 0)


def peer_main(h, xn, u, vt, cnt, ar, rank_tiles, bt_tiles, *, tt, et, lane_blocks_per_iter):
    t, d = h.shape
    ne = u.shape[0]
    n_e1 = et // N_KEYS
    bpt = ne // et
    n_blocks = (t // tt) * bpt
    kern = functools.partial(_peer_main_kernel, n_blocks=n_blocks, blocks_per_tile=bpt,
                             lane_blocks_per_iter=lane_blocks_per_iter)

    def cur(g):
        return jnp.minimum(g, n_blocks - 1)

    def prev(g):
        return jnp.maximum(g - 1, 0)

    row_spec = pl.BlockSpec((PEER_HEADS, n_e1, tt), lambda g: (0, cur(g) % bpt, cur(g) // bpt))
    tok_spec = pl.BlockSpec((PEER_HEADS, tt // LANES, N_KEYS // 2, LANES),
                            lambda g: (0, cur(g) // bpt, 0, 0))
    return pl.pallas_call(
        kern,
        grid=(n_blocks + 1,),
        in_specs=[pl.BlockSpec((tt, d), lambda g: (prev(g) // bpt, 0)),
                  pl.BlockSpec((tt, d), lambda g: (cur(g) // bpt, 0)),
                  pl.BlockSpec((et, d), lambda g: (cur(g) % bpt, 0)),
                  pl.BlockSpec((d, et), lambda g: (0, prev(g) % bpt)),
                  row_spec, row_spec, tok_spec, tok_spec],
        out_specs=pl.BlockSpec((tt, d), lambda g: (prev(g) // bpt, 0)),
        out_shape=jax.ShapeDtypeStruct((t, d), F32),
        scratch_shapes=[pltpu.VMEM((d, tt), F32), pltpu.VMEM((et, tt), F32),
                        pltpu.VMEM((et, tt), BF16)],
        compiler_params=_cparams(("arbitrary",)),
        name="peer_main",
    )(h, xn, u, vt, cnt, ar, rank_tiles, bt_tiles)


def peer_layer(h, gain, wq, keys, u, v):
    t = h.shape[0]
    xn, s1t, s2t = peer_prep(h, gain, wq.astype(BF16), keys, tm=min(512, t))
    cnt, ar, rank_tiles, bt_tiles = peer_topk(s1t, s2t, tt=min(256, t))
    return peer_main(h, xn, u.astype(BF16), v.astype(BF16).T, cnt, ar, rank_tiles, bt_tiles,
                     tt=min(512, t), et=1024, lane_blocks_per_iter=2)


def _rope_tables(s):
    half = ROT_DIM // 2
    inv = ROPE_THETA ** (-jnp.arange(half, dtype=F32) * 2.0 / ROT_DIM)
    ang = jnp.arange(s).astype(F32)[:, None] * inv[None, :]
    cos, sin = jnp.cos(ang), jnp.sin(ang)
    pad = HEAD - ROT_DIM
    ones = jnp.ones((s, pad), F32)
    zeros = jnp.zeros((s, pad), F32)
    zh = jnp.zeros((s, half), F32)
    ct = jnp.concatenate([cos, cos, ones], axis=1)
    at = jnp.concatenate([-sin, zh, zeros], axis=1)
    bt = jnp.concatenate([zh, sin, zeros], axis=1)
    return ct, at, bt


def _attn_proj_kernel(h_ref, kvg_ref, ag_ref, wkv_ref, wq_ref, kn_ref, qn_ref,
                      ct_ref, at_ref, bt_ref, q_ref, k_ref, v_ref):
    y = _rms(h_ref[...])
    kv = jnp.dot((y * kvg_ref[...]).astype(BF16), wkv_ref[...], preferred_element_type=F32)
    qq = jnp.dot((y * ag_ref[...]).astype(BF16), wq_ref[...], preferred_element_type=F32)
    ct, at, bt = ct_ref[...], at_ref[...], bt_ref[...]
    half = ROT_DIM // 2

    def head_norm_rope(x, gain):
        n = _rms(x) * gain
        return n * ct + pltpu.roll(n, HEAD - half, 1) * at + pltpu.roll(n, half, 1) * bt

    kw = k_ref.shape[1]
    for hd in range(kw // HEAD):
        cs = slice(hd * HEAD, (hd + 1) * HEAD)
        k_ref[:, cs] = head_norm_rope(kv[:, cs], kn_ref[...]).astype(k_ref.dtype)
    v_ref[...] = kv[:, kw:].astype(v_ref.dtype)
    for hd in range(q_ref.shape[1] // HEAD):
        cs = slice(hd * HEAD, (hd + 1) * HEAD)
        q_ref[:, cs] = head_norm_rope(qq[:, cs], qn_ref[...]).astype(q_ref.dtype)


def attn_proj(h, kv_gain, a_gain, wkv, wq, k_norm, q_norm, seq, *, tm):
    t, d = h.shape
    kw = wkv.shape[1] // 2
    qw = wq.shape[1]
    ct, at, bt = _rope_tables(seq)
    ns = seq // tm
    tab = pl.BlockSpec((tm, HEAD), lambda i: (i % ns, 0))
    vec = lambda n: pl.BlockSpec((1, n), lambda i: (0, 0))
    return pl.pallas_call(
        _attn_proj_kernel,
        grid=(t // tm,),
        in_specs=[pl.BlockSpec((tm, d), lambda i: (i, 0)), vec(d), vec(d),
                  pl.BlockSpec(wkv.shape, lambda i: (0, 0)),
                  pl.BlockSpec(wq.shape, lambda i: (0, 0)),
                  vec(HEAD), vec(HEAD), tab, tab, tab],
        out_specs=[pl.BlockSpec((tm, qw), lambda i: (i, 0)),
                   pl.BlockSpec((tm, kw), lambda i: (i, 0)),
                   pl.BlockSpec((tm, kw), lambda i: (i, 0))],
        out_shape=[jax.ShapeDtypeStruct((t, qw), BF16),
                   jax.ShapeDtypeStruct((t, kw), BF16),
                   jax.ShapeDtypeStruct((t, kw), BF16)],
        compiler_params=_cparams(("parallel",)),
        name="attn_proj",
    )(h, kv_gain.reshape(1, d), a_gain.reshape(1, d), wkv, wq,
      k_norm.reshape(1, HEAD), q_norm.reshape(1, HEAD), ct, at, bt)


ATTN_EXP2_SCALE = HEAD ** -0.5 * math.log2(math.e)
MAX_WINDOW = max(w for w, _ in DIL_GROUPS)


def _attn_window_bias(w, dil):
    rows = Q_PER_KV * Q_BLOCK
    tq = w + (np.arange(rows)[:, None] % Q_BLOCK)
    diff = tq - np.arange(w + Q_BLOCK)[None, :]
    valid = (diff >= 0) & (diff <= w) & (diff % dil == 0)
    return np.where(valid, 0.0, -np.inf).astype(np.float32)


def _attn_kernel(q0_ref, q1_ref, q2_ref, k0_ref, k1_ref, k2_ref, v0_ref, v1_ref, v2_ref,
                 b0_ref, b1_ref, b2_ref, o_ref):
    qb = pl.program_id(2)
    t0 = qb * Q_BLOCK
    rows = Q_PER_KV * Q_BLOCK
    groups = list(zip(DIL_GROUPS, (q0_ref, q1_ref, q2_ref), (k0_ref, k1_ref, k2_ref),
                      (v0_ref, v1_ref, v2_ref), (b0_ref, b1_ref, b2_ref)))

    def raw_scores(q_ref, k_ref, ks, span):
        q2 = jnp.concatenate([q_ref[0, :, r * HEAD:(r + 1) * HEAD] for r in range(Q_PER_KV)],
                             axis=0)
        return lax.dot_general(q2, k_ref[0, pl.ds(ks, span), :], NT_DIMS,
                               preferred_element_type=F32)

    def finish(scores, starts):
        m = functools.reduce(jnp.maximum, [jnp.max(s, axis=-1, keepdims=True) for s in scores])
        den = jnp.zeros((rows, 1), F32)
        out = jnp.zeros((rows, HEAD), F32)
        for s, ks, ((w, _), _, _, v_ref, _) in zip(scores, starts, groups):
            p = jnp.exp2((s - m) * ATTN_EXP2_SCALE)
            den = den + jnp.sum(p, axis=-1, keepdims=True)
            out = out + jnp.dot(p.astype(BF16), v_ref[0, pl.ds(ks, w + Q_BLOCK), :],
                                preferred_element_type=F32)
        out = out / den
        for r in range(Q_PER_KV):
            o_ref[0, :, r * HEAD:(r + 1) * HEAD] = (
                out[r * Q_BLOCK:(r + 1) * Q_BLOCK].astype(o_ref.dtype))

    @pl.when(t0 >= MAX_WINDOW)
    def _():
        scores, starts = [], []
        for (w, dil), q_ref, k_ref, _, b_ref in groups:
            ks = pl.multiple_of(t0 - w, Q_BLOCK)
            scores.append(raw_scores(q_ref, k_ref, ks, w + Q_BLOCK) + b_ref[...])
            starts.append(ks)
        finish(scores, starts)

    @pl.when(t0 < MAX_WINDOW)
    def _():
        scores, starts = [], []
        for (w, dil), q_ref, k_ref, _, _ in groups:
            span = w + Q_BLOCK
            ks = pl.multiple_of(jnp.maximum(t0 - w, 0), Q_BLOCK)
            tq = t0 + (lax.broadcasted_iota(jnp.int32, (rows, span), 0) & (Q_BLOCK - 1))
            diff = tq - (ks + lax.broadcasted_iota(jnp.int32, (rows, span), 1))
            valid = (diff >= 0) & (diff <= w) & ((diff & (dil - 1)) == 0)
            scores.append(jnp.where(valid, raw_scores(q_ref, k_ref, ks, span), -jnp.inf))
            starts.append(ks)
        finish(scores, starts)


def dilated_attention(q, k, v):
    b, s, _ = q.shape
    qcols = Q_PER_KV * HEAD

    def qspec(g):
        return pl.BlockSpec((1, Q_BLOCK, qcols), lambda bi, kv, qb, g=g: (bi, qb, g * KV_PER_GROUP + kv))

    def kvspec(g):
        return pl.BlockSpec((1, s, HEAD), lambda bi, kv, qb, g=g: (bi, 0, g * KV_PER_GROUP + kv))

    ng = len(DIL_GROUPS)
    biases = [jnp.asarray(_attn_window_bias(w, dil)) for w, dil in DIL_GROUPS]
    return pl.pallas_call(
        _attn_kernel,
        grid=(b, KV_PER_GROUP, s // Q_BLOCK),
        in_specs=([qspec(g) for g in range(ng)] + [kvspec(g) for g in range(ng)] * 2
                  + [pl.BlockSpec(bias.shape, lambda bi, kv, qb: (0, 0)) for bias in biases]),
        out_specs=pl.BlockSpec((1, Q_BLOCK, qcols), lambda bi, kv, qb: (bi, qb, kv)),
        out_shape=jax.ShapeDtypeStruct((b, s, KV_PER_GROUP * qcols), BF16),
        compiler_params=_cparams(("parallel", "parallel", "arbitrary")),
        name="dilated_attention",
    )(q, q, q, k, k, k, v, v, v, *biases)


def kernel(x, hgrn_norm, hgrn_w_in, hgrn_lb_logits, hgrn_out_norm, hgrn_w_out, kv_norm, w_kv,
           k_norm, attn_norm, w_q, q_norm, w_o, ffn_norm, peer_w_q, peer_sub_keys, peer_u, peer_v):
    b, s, d = x.shape
    t = b * s
    depth = ffn_norm.shape[0]
    n_a = hgrn_norm.shape[0]
    h = x.reshape(t, d)
    tm = min(1024, t)
    for layer in range(depth):
        if layer < n_a:
            y = norm_matmul(h, hgrn_norm[layer], hgrn_w_in[layer].astype(BF16), tm=tm, tn=512)
            o = hgrn_scan(y.reshape(b, s, 4 * d), hgrn_lb_logits, hgrn_out_norm[layer],
                          layer=layer, seq_block=min(1024, s))
            h = matmul_residual(o.reshape(t, d), hgrn_w_out[layer].astype(BF16), h, tm=tm)
        else:
            j = layer - n_a
            qr, kr, vr = attn_proj(h, kv_norm, attn_norm[j], w_kv.astype(BF16),
                                   w_q[j].astype(BF16), k_norm, q_norm[j], s, tm=min(512, s))
            if j == 0:
                k_sh, v_sh = kr.reshape(b, s, -1), vr.reshape(b, s, -1)
            ao = dilated_attention(qr.reshape(b, s, -1), k_sh, v_sh)
            h = matmul_residual(ao.reshape(t, -1), w_o[j].astype(BF16), h, tm=tm)
        h = peer_layer(h, ffn_norm[layer], peer_w_q[layer], peer_sub_keys[layer],
                       peer_u[layer], peer_v[layer])
    return h.reshape(b, s, d)
```

```python
import functools
import math

import numpy as np
import jax
import jax.numpy as jnp
from jax import lax
from jax.experimental import pallas as pl
from jax.experimental.pallas import tpu as pltpu

F32 = jnp.float32
BF16 = jnp.bfloat16
EPS = 1e-6

LANES = 128
HEAD = 128
HG_CHUNK = 128
DIL_GROUPS = ((128, 1), (512, 4), (2048, 16))
KV_PER_GROUP = 2
Q_PER_KV = 2
ROT_DIM = HEAD // 4
ROPE_THETA = 500000.0
Q_BLOCK = 128
PEER_HEADS = 8
PEER_TOPK = 16
N_KEYS = 128
VMEM_LIMIT = 56 * 1024 * 1024

NT_DIMS = (((1,), (1,)), ((), ()))


def _cparams(sem):
    return pltpu.CompilerParams(dimension_semantics=sem, vmem_limit_bytes=VMEM_LIMIT)


def _rms(x):
    return x * lax.rsqrt(jnp.mean(x * x, axis=-1, keepdims=True) + EPS)


def _sigmoid(x):
    return 1.0 / (1.0 + jnp.exp(-x))


def _norm_matmul_kernel(x_ref, g_ref, w_ref, o_ref, xn_ref):
    @pl.when(pl.program_id(1) == 0)
    def _():
        xn_ref[...] = (_rms(x_ref[...]) * g_ref[...]).astype(xn_ref.dtype)

    o_ref[...] = jnp.dot(xn_ref[...], w_ref[...],
                         preferred_element_type=F32).astype(o_ref.dtype)


def norm_matmul(x, gain, w, *, tm, tn, out_dtype=F32):
    t, d = x.shape
    n = w.shape[1]
    return pl.pallas_call(
        _norm_matmul_kernel,
        grid=(t // tm, n // tn),
        in_specs=[pl.BlockSpec((tm, d), lambda i, j: (i, 0)),
                  pl.BlockSpec((1, d), lambda i, j: (0, 0)),
                  pl.BlockSpec((d, tn), lambda i, j: (0, j))],
        out_specs=pl.BlockSpec((tm, tn), lambda i, j: (i, j)),
        out_shape=jax.ShapeDtypeStruct((t, n), out_dtype),
        scratch_shapes=[pltpu.VMEM((tm, d), BF16)],
        compiler_params=_cparams(("parallel", "arbitrary")),
        name="norm_matmul",
    )(x, gain.reshape(1, d), w)


def _matmul_res_kernel(a_ref, w_ref, r_ref, o_ref):
    o_ref[...] = r_ref[...] + jnp.dot(a_ref[...], w_ref[...], preferred_element_type=F32)


def matmul_residual(a, w, res, *, tm):
    t, k = a.shape
    n = w.shape[1]
    return pl.pallas_call(
        _matmul_res_kernel,
        grid=(t // tm,),
        in_specs=[pl.BlockSpec((tm, k), lambda i: (i, 0)),
                  pl.BlockSpec((k, n), lambda i: (0, 0)),
                  pl.BlockSpec((tm, n), lambda i: (i, 0))],
        out_specs=pl.BlockSpec((tm, n), lambda i: (i, 0)),
        out_shape=jax.ShapeDtypeStruct((t, n), F32),
        compiler_params=_cparams(("parallel",)),
        name="matmul_residual",
    )(a, w, res)


def _hgrn_levels(c):
    return int(math.log2(c))


def _hgrn_sum_matrix(c):
    nlev = _hgrn_levels(c)
    m = np.zeros(((nlev + 2) * c, c), np.float32)
    for l in range(nlev):
        h = c >> (l + 1)
        for r in range(c):
            mid = (r // (2 * h)) * 2 * h + h
            if r >= mid:
                m[l * c + r, mid:r + 1] = 1.0
            else:
                m[l * c + r, r + 1:mid] = 1.0
    for r in range(c):
        m[nlev * c + r, :r + 1] = 1.0
        m[(nlev + 1) * c + r, r + 1:] = 1.0
    return m


def _hgrn_kernel(q_ref, f_ref, i_ref, g_ref, lbl_ref, og_ref, m_ref, o_ref, st_ref, *,
                 layer, chunk, n_chunks):
    c = chunk
    nlev = _hgrn_levels(c)

    @pl.when(pl.program_id(2) == 0)
    def _():
        st_ref[...] = jnp.zeros_like(st_ref)

    lg = lbl_ref[...]
    e = jnp.exp(lg - jnp.max(lg, axis=0, keepdims=True))
    lb = jnp.sum(e[:layer + 1], axis=0, keepdims=True) / jnp.sum(e, axis=0, keepdims=True)
    og = og_ref[...]
    row = lax.broadcasted_iota(jnp.int32, (c, c), 0)
    col = lax.broadcasted_iota(jnp.int32, (c, c), 1)

    def body(ci, carry):
        sl = pl.ds(pl.multiple_of(ci * c, c), c)
        q = q_ref[0, sl, :]
        fr = f_ref[0, sl, :]
        v = i_ref[0, sl, :]
        gt = g_ref[0, sl, :]
        qa = q * _sigmoid(q)
        f = lb + (1.0 - lb) * _sigmoid(fr)
        k = 1.0 - f
        logf = jnp.log(f)
        hi = logf.astype(BF16)
        lo = (logf - hi.astype(F32)).astype(BF16)
        e2 = jnp.dot(m_ref[...], jnp.concatenate([hi, lo], axis=1),
                     preferred_element_type=F32)
        ex = e2[:, :HEAD] + e2[:, HEAD:]

        kb = k.astype(BF16)
        s = jnp.where(row == col,
                      lax.dot_general(qa.astype(BF16), kb, NT_DIMS, preferred_element_type=F32),
                      0.0)
        for l in range(nlev):
            sh = nlev - 1 - l
            x = jnp.exp(ex[l * c:(l + 1) * c])
            p = lax.dot_general((qa * x).astype(BF16), (k * x).astype(BF16), NT_DIMS,
                                preferred_element_type=F32)
            mask = (((row >> (sh + 1)) == (col >> (sh + 1)))
                    & (((row >> sh) & 1) == 1) & (((col >> sh) & 1) == 0))
            s = jnp.where(mask, p, s)
        vb = v.astype(BF16)
        intra = jnp.dot(s.astype(BF16), vb, preferred_element_type=F32)
        b_incl = ex[nlev * c:(nlev + 1) * c]
        st = st_ref[...]
        inter = lax.dot_general((qa * jnp.exp(b_incl)).astype(BF16), st.astype(BF16), NT_DIMS,
                                preferred_element_type=F32)
        o = inter + intra
        y = _rms(o) * og * (gt * _sigmoid(gt))
        o_ref[0, sl, :] = y.astype(o_ref.dtype)

        k2 = (k * jnp.exp(ex[(nlev + 1) * c:(nlev + 2) * c])).astype(BF16)
        upd = jnp.dot(v.T.astype(BF16), k2, preferred_element_type=F32)
        st_ref[...] = st * jnp.exp(b_incl[c - 1:c, :]) + upd
        return carry

    lax.fori_loop(0, n_chunks, body, 0, unroll=4)


def hgrn_scan(y, lb_logits, out_gain, *, layer, seq_block):
    b, s, d4 = y.shape
    d = d4 // 4
    nh = d // HEAD
    c = min(HG_CHUNK, seq_block)
    msel = jnp.asarray(_hgrn_sum_matrix(c), BF16)
    nl = lb_logits.shape[0]
    kern = functools.partial(_hgrn_kernel, layer=layer, chunk=c, n_chunks=seq_block // c)

    def part(p):
        return pl.BlockSpec((1, seq_block, HEAD), lambda bi, hi, si, p=p: (bi, si, p * nh + hi))

    return pl.pallas_call(
        kern,
        grid=(b, nh, s // seq_block),
        in_specs=[part(0), part(1), part(2), part(3),
                  pl.BlockSpec((nl, HEAD), lambda bi, hi, si: (0, hi)),
                  pl.BlockSpec((1, HEAD), lambda bi, hi, si: (0, 0)),
                  pl.BlockSpec(msel.shape, lambda bi, hi, si: (0, 0))],
        out_specs=pl.BlockSpec((1, seq_block, HEAD), lambda bi, hi, si: (bi, si, hi)),
        out_shape=jax.ShapeDtypeStruct((b, s, d), BF16),
        scratch_shapes=[pltpu.VMEM((HEAD, HEAD), F32)],
        compiler_params=_cparams(("parallel", "parallel", "arbitrary")),
        name="hgrn_scan",
    )(y, y, y, y, lb_logits, out_gain.reshape(1, HEAD), msel)


def _peer_prep_kernel(h_ref, g_ref, wq_ref, keys_ref, xn_ref, s1_ref, s2_ref):
    xn = (_rms(h_ref[...]) * g_ref[...]).astype(BF16)
    xn_ref[...] = xn
    q = jnp.dot(xn, wq_ref[...], preferred_element_type=F32)
    k0 = keys_ref[0].astype(BF16)
    k1 = keys_ref[1].astype(BF16)
    for hh in range(PEER_HEADS):
        base = hh * 2 * HEAD
        q1 = q[:, base:base + HEAD].astype(BF16)
        q2 = q[:, base + HEAD:base + 2 * HEAD].astype(BF16)
        s1_ref[hh] = lax.dot_general(k0, q1, NT_DIMS, preferred_element_type=F32)
        s2_ref[hh] = lax.dot_general(k1, q2, NT_DIMS, preferred_element_type=F32)


def peer_prep(h, gain, wq, keys, *, tm):
    t, d = h.shape
    nq = wq.shape[1]
    return pl.pallas_call(
        _peer_prep_kernel,
        grid=(t // tm,),
        in_specs=[pl.BlockSpec((tm, d), lambda i: (i, 0)),
                  pl.BlockSpec((1, d), lambda i: (0, 0)),
                  pl.BlockSpec((d, nq), lambda i: (0, 0)),
                  pl.BlockSpec(keys.shape, lambda i: (0, 0, 0))],
        out_specs=[pl.BlockSpec((tm, d), lambda i: (i, 0)),
                   pl.BlockSpec((PEER_HEADS, N_KEYS, tm), lambda i: (0, 0, i)),
                   pl.BlockSpec((PEER_HEADS, N_KEYS, tm), lambda i: (0, 0, i))],
        out_shape=[jax.ShapeDtypeStruct((t, d), BF16),
                   jax.ShapeDtypeStruct((PEER_HEADS, N_KEYS, t), F32),
                   jax.ShapeDtypeStruct((PEER_HEADS, N_KEYS, t), F32)],
        compiler_params=_cparams(("parallel",)),
        name="peer_prep",
    )(h, gain.reshape(1, d), wq, keys)


def _top_values(x, n, dst_ref, with_rank=False):
    cur = x
    rank = jnp.full(x.shape, float(n), F32) if with_rank else None
    for r in range(n):
        m = jnp.max(cur, axis=0, keepdims=True)
        dst_ref[r:r + 1, :] = m
        hit = cur == m
        if with_rank:
            rank = jnp.where(hit, float(r), rank)
        cur = jnp.where(hit, -jnp.inf, cur)
    return rank


def _peer_topk_kernel(s1_ref, s2_ref, cnt_ref, ar_ref, rk_ref, bt_ref, a_ref, b_ref):
    k = PEER_TOPK
    sub = 8
    rid = lax.broadcasted_iota(jnp.int32, (sub, a_ref.shape[1]), 0)

    def head(hh, carry):
        s1 = s1_ref[hh]
        s2 = s2_ref[hh]
        _top_values(s1, k, a_ref)
        rank2 = _top_values(s2, k, b_ref, with_rank=True)
        a_lo, a_hi = a_ref[0:sub, :], a_ref[sub:k, :]
        b_hi = b_ref[sub:k, :]
        a0 = a_ref[0:1, :]
        slabs = [a_lo + b_ref[0:1, :], a_hi + b_ref[0:1, :], b_hi + a0]
        for j in range(1, sub):
            lim = k // (j + 1)
            slabs.append(jnp.where(rid < lim, a_lo + b_ref[j:j + 1, :], -jnp.inf))
        cur = slabs
        tau = None
        for r in range(k):
            m = functools.reduce(jnp.maximum, cur)
            m = jnp.max(m, axis=0, keepdims=True)
            tau = m
            if r + 1 < k:
                cur = [jnp.where(x == m, -jnp.inf, x) for x in cur]
        m1 = a0
        m2 = b_ref[0:1, :]
        top = m1 + m2
        z = None
        for x in slabs:
            zz = jnp.sum(jnp.where(x >= tau, jnp.exp(x - top), 0.0), axis=0, keepdims=True)
            z = zz if z is None else z + zz
        cnt = jnp.zeros(s1.shape, F32)
        for j in range(k):
            cnt = cnt + jnp.where(s1 + b_ref[j:j + 1, :] >= tau, 1.0, 0.0)
        cnt_ref[hh] = cnt
        ar_ref[hh] = jnp.exp(s1 - m1) / z
        bt = jnp.exp(s2 - m2)
        for lb in range(s2.shape[1] // LANES):
            ln = slice(lb * LANES, (lb + 1) * LANES)
            rk_ref[hh, lb] = pltpu.bitcast(rank2[:, ln].astype(BF16), jnp.uint32)
            bt_ref[hh, lb] = pltpu.bitcast(bt[:, ln].astype(BF16), jnp.uint32)
        return carry

    lax.fori_loop(0, PEER_HEADS, head, 0)


def peer_topk(s1t, s2t, *, tt):
    nh, nk, t = s1t.shape
    spec = pl.BlockSpec((nh, nk, tt), lambda i: (0, 0, i))
    shp = jax.ShapeDtypeStruct((nh, nk, t), F32)
    tspec = pl.BlockSpec((nh, tt // LANES, nk // 2, LANES), lambda i: (0, i, 0, 0))
    tshp = jax.ShapeDtypeStruct((nh, t // LANES, nk // 2, LANES), jnp.uint32)
    return pl.pallas_call(
        _peer_topk_kernel,
        grid=(t // tt,),
        in_specs=[spec, spec],
        out_specs=[spec, spec, tspec, tspec],
        out_shape=[shp, shp, tshp, tshp],
        scratch_shapes=[pltpu.VMEM((PEER_TOPK, tt), F32), pltpu.VMEM((PEER_TOPK, tt), F32)],
        compiler_params=_cparams(("parallel",)),
        name="peer_topk",
    )(s1t, s2t)


GELU_C0 = math.sqrt(2.0 / math.pi)
GELU_C1 = GELU_C0 * 0.044715


def _gelu(z):
    return 0.5 * z * (1.0 + jnp.tanh(z * (GELU_C0 + GELU_C1 * (z * z))))


def _peer_main_kernel(h_ref, xn_ref, u_ref, vt_ref, cnt_ref, ar_ref, rk_ref, bt_ref, o_ref,
                      acc_ref, zt_ref, a_ref, *, n_blocks, blocks_per_tile, lane_blocks_per_iter):
    g_step = pl.program_id(0)
    et, tt = zt_ref.shape
    n_e1 = et // N_KEYS
    jb = jnp.maximum(g_step - 1, 0) % blocks_per_tile

    @pl.when(g_step == 0)
    def _():
        a_ref[...] = jnp.zeros_like(a_ref)

    @pl.when(jb == 0)
    def _():
        acc_ref[...] = jnp.zeros_like(acc_ref)

    acc_ref[...] += jnp.dot(vt_ref[...], a_ref[...], preferred_element_type=F32)
    zt_ref[...] = lax.dot_general(u_ref[...], xn_ref[...], NT_DIMS, preferred_element_type=F32)

    @pl.when((g_step >= 1) & (jb == blocks_per_tile - 1))
    def _():
        o_ref[...] = h_ref[...] + acc_ref[...].T

    def lane_group(li, carry):
        for sub in range(lane_blocks_per_iter):
            lb = li * lane_blocks_per_iter + sub
            ln = pl.ds(pl.multiple_of(lb * LANES, LANES), LANES)
            for c in range(n_e1):
                rows = slice(c * N_KEYS, (c + 1) * N_KEYS)
                g = jnp.zeros((N_KEYS, LANES), BF16)
                for hh in range(PEER_HEADS):
                    cnt = cnt_ref[hh, c:c + 1, ln].astype(BF16)
                    ar = ar_ref[hh, c:c + 1, ln].astype(BF16)
                    rk = pltpu.bitcast(rk_ref[hh, lb], BF16)
                    bt = pltpu.bitcast(bt_ref[hh, lb], BF16)
                    g = g + jnp.where(rk < cnt, bt * ar, 0)
                a_ref[rows, ln] = _gelu(zt_ref[rows, ln].astype(BF16)) * g
        return carry

    lax.fori_loop(0, tt // (LANES * lane_blocks_per_iter), lane_group, 0)


def peer_main(h, xn, u, vt, cnt, ar, rank_tiles, bt_tiles, *, tt, et, lane_blocks_per_iter):
    t, d = h.shape
    ne = u.shape[0]
    n_e1 = et // N_KEYS
    bpt = ne // et
    n_blocks = (t // tt) * bpt
    kern = functools.partial(_peer_main_kernel, n_blocks=n_blocks, blocks_per_tile=bpt,
                             lane_blocks_per_iter=lane_blocks_per_iter)

    def cur(g):
        return jnp.minimum(g, n_blocks - 1)

    def prev(g):
        return jnp.maximum(g - 1, 0)

    row_spec = pl.BlockSpec((PEER_HEADS, n_e1, tt), lambda g: (0, cur(g) % bpt, cur(g) // bpt))
    tok_spec = pl.BlockSpec((PEER_HEADS, tt // LANES, N_KEYS // 2, LANES),
                            lambda g: (0, cur(g) // bpt, 0, 0))
    return pl.pallas_call(
        kern,
        grid=(n_blocks + 1,),
        in_specs=[pl.BlockSpec((tt, d), lambda g: (prev(g) // bpt, 0)),
                  pl.BlockSpec((tt, d), lambda g: (cur(g) // bpt, 0)),
                  pl.BlockSpec((et, d), lambda g: (cur(g) % bpt, 0)),
                  pl.BlockSpec((d, et), lambda g: (0, prev(g) % bpt)),
                  row_spec, row_spec, tok_spec, tok_spec],
        out_specs=pl.BlockSpec((tt, d), lambda g: (prev(g) // bpt, 0)),
        out_shape=jax.ShapeDtypeStruct((t, d), F32),
        scratch_shapes=[pltpu.VMEM((d, tt), F32), pltpu.VMEM((et, tt), F32),
                        pltpu.VMEM((et, tt), BF16)],
        compiler_params=_cparams(("arbitrary",)),
        name="peer_main",
    )(h, xn, u, vt, cnt, ar, rank_tiles, bt_tiles)


def peer_layer(h, gain, wq, keys, u, v):
    t = h.shape[0]
    xn, s1t, s2t = peer_prep(h, gain, wq.astype(BF16), keys, tm=min(512, t))
    cnt, ar, rank_tiles, bt_tiles = peer_topk(s1t, s2t, tt=min(256, t))
    return peer_main(h, xn, u.astype(BF16), v.astype(BF16).T, cnt, ar, rank_tiles, bt_tiles,
                     tt=min(512, t), et=2048, lane_blocks_per_iter=1)


def _rope_tables(s):
    half = ROT_DIM // 2
    inv = ROPE_THETA ** (-jnp.arange(half, dtype=F32) * 2.0 / ROT_DIM)
    ang = jnp.arange(s).astype(F32)[:, None] * inv[None, :]
    cos, sin = jnp.cos(ang), jnp.sin(ang)
    pad = HEAD - ROT_DIM
    ones = jnp.ones((s, pad), F32)
    zeros = jnp.zeros((s, pad), F32)
    zh = jnp.zeros((s, half), F32)
    ct = jnp.concatenate([cos, cos, ones], axis=1)
    at = jnp.concatenate([-sin, zh, zeros], axis=1)
    bt = jnp.concatenate([zh, sin, zeros], axis=1)
    return ct, at, bt


def _attn_proj_kernel(h_ref, kvg_ref, ag_ref, wkv_ref, wq_ref, kn_ref, qn_ref,
                      ct_ref, at_ref, bt_ref, q_ref, k_ref, v_ref):
    y = _rms(h_ref[...])
    kv = jnp.dot((y * kvg_ref[...]).astype(BF16), wkv_ref[...], preferred_element_type=F32)
    qq = jnp.dot((y * ag_ref[...]).astype(BF16), wq_ref[...], preferred_element_type=F32)
    ct, at, bt = ct_ref[...], at_ref[...], bt_ref[...]
    half = ROT_DIM // 2

    def head_norm_rope(x, gain):
        n = _rms(x) * gain
        return n * ct + pltpu.roll(n, HEAD - half, 1) * at + pltpu.roll(n, half, 1) * bt

    kw = k_ref.shape[1]
    for hd in range(kw // HEAD):
        cs = slice(hd * HEAD, (hd + 1) * HEAD)
        k_ref[:, cs] = head_norm_rope(kv[:, cs], kn_ref[...]).astype(k_ref.dtype)
    v_ref[...] = kv[:, kw:].astype(v_ref.dtype)
    for hd in range(q_ref.shape[1] // HEAD):
        cs = slice(hd * HEAD, (hd + 1) * HEAD)
        q_ref[:, cs] = head_norm_rope(qq[:, cs], qn_ref[...]).astype(q_ref.dtype)


def attn_proj(h, kv_gain, a_gain, wkv, wq, k_norm, q_norm, seq, *, tm):
    t, d = h.shape
    kw = wkv.shape[1] // 2
    qw = wq.shape[1]
    ct, at, bt = _rope_tables(seq)
    ns = seq // tm
    tab = pl.BlockSpec((tm, HEAD), lambda i: (i % ns, 0))
    vec = lambda n: pl.BlockSpec((1, n), lambda i: (0, 0))
    return pl.pallas_call(
        _attn_proj_kernel,
        grid=(t // tm,),
        in_specs=[pl.BlockSpec((tm, d), lambda i: (i, 0)), vec(d), vec(d),
                  pl.BlockSpec(wkv.shape, lambda i: (0, 0)),
                  pl.BlockSpec(wq.shape, lambda i: (0, 0)),
                  vec(HEAD), vec(HEAD), tab, tab, tab],
        out_specs=[pl.BlockSpec((tm, qw), lambda i: (i, 0)),
                   pl.BlockSpec((tm, kw), lambda i: (i, 0)),
                   pl.BlockSpec((tm, kw), lambda i: (i, 0))],
        out_shape=[jax.ShapeDtypeStruct((t, qw), BF16),
                   jax.ShapeDtypeStruct((t, kw), BF16),
                   jax.ShapeDtypeStruct((t, kw), BF16)],
        compiler_params=_cparams(("parallel",)),
        name="attn_proj",
    )(h, kv_gain.reshape(1, d), a_gain.reshape(1, d), wkv, wq,
      k_norm.reshape(1, HEAD), q_norm.reshape(1, HEAD), ct, at, bt)


ATTN_EXP2_SCALE = HEAD ** -0.5 * math.log2(math.e)
MAX_WINDOW = max(w for w, _ in DIL_GROUPS)


def _attn_window_bias(w, dil):
    rows = Q_PER_KV * Q_BLOCK
    tq = w + (np.arange(rows)[:, None] % Q_BLOCK)
    diff = tq - np.arange(w + Q_BLOCK)[None, :]
    valid = (diff >= 0) & (diff <= w) & (diff % dil == 0)
    return np.where(valid, 0.0, -np.inf).astype(np.float32)


def _attn_kernel(q0_ref, q1_ref, q2_ref, k0_ref, k1_ref, k2_ref, v0_ref, v1_ref, v2_ref,
                 b0_ref, b1_ref, b2_ref, o_ref):
    qb = pl.program_id(2)
    t0 = qb * Q_BLOCK
    rows = Q_PER_KV * Q_BLOCK
    groups = list(zip(DIL_GROUPS, (q0_ref, q1_ref, q2_ref), (k0_ref, k1_ref, k2_ref),
                      (v0_ref, v1_ref, v2_ref), (b0_ref, b1_ref, b2_ref)))

    def raw_scores(q_ref, k_ref, ks, span):
        q2 = jnp.concatenate([q_ref[0, :, r * HEAD:(r + 1) * HEAD] for r in range(Q_PER_KV)],
                             axis=0)
        return lax.dot_general(q2, k_ref[0, pl.ds(ks, span), :], NT_DIMS,
                               preferred_element_type=F32)

    def finish(scores, starts):
        m = functools.reduce(jnp.maximum, [jnp.max(s, axis=-1, keepdims=True) for s in scores])
        den = jnp.zeros((rows, 1), F32)
        out = jnp.zeros((rows, HEAD), F32)
        for s, ks, ((w, _), _, _, v_ref, _) in zip(scores, starts, groups):
            p = jnp.exp2((s - m) * ATTN_EXP2_SCALE)
            den = den + jnp.sum(p, axis=-1, keepdims=True)
            out = out + jnp.dot(p.astype(BF16), v_ref[0, pl.ds(ks, w + Q_BLOCK), :],
                                preferred_element_type=F32)
        out = out / den
        for r in range(Q_PER_KV):
            o_ref[0, :, r * HEAD:(r + 1) * HEAD] = (
                out[r * Q_BLOCK:(r + 1) * Q_BLOCK].astype(o_ref.dtype))

    @pl.when(t0 >= MAX_WINDOW)
    def _():
        scores, starts = [], []
        for (w, dil), q_ref, k_ref, _, b_ref in groups:
            ks = pl.multiple_of(t0 - w, Q_BLOCK)
            scores.append(raw_scores(q_ref, k_ref, ks, w + Q_BLOCK) + b_ref[...])
            starts.append(ks)
        finish(scores, starts)

    @pl.when(t0 < MAX_WINDOW)
    def _():
        scores, starts = [], []
        for (w, dil), q_ref, k_ref, _, _ in groups:
            span = w + Q_BLOCK
            ks = pl.multiple_of(jnp.maximum(t0 - w, 0), Q_BLOCK)
            tq = t0 + (lax.broadcasted_iota(jnp.int32, (rows, span), 0) & (Q_BLOCK - 1))
            diff = tq - (ks + lax.broadcasted_iota(jnp.int32, (rows, span), 1))
            valid = (diff >= 0) & (diff <= w) & ((diff & (dil - 1)) == 0)
            scores.append(jnp.where(valid, raw_scores(q_ref, k_ref, ks, span), -jnp.inf))
            starts.append(ks)
        finish(scores, starts)


def dilated_attention(q, k, v):
    b, s, _ = q.shape
    qcols = Q_PER_KV * HEAD

    def qspec(g):
        return pl.BlockSpec((1, Q_BLOCK, qcols), lambda bi, kv, qb, g=g: (bi, qb, g * KV_PER_GROUP + kv))

    def kvspec(g):
        return pl.BlockSpec((1, s, HEAD), lambda bi, kv, qb, g=g: (bi, 0, g * KV_PER_GROUP + kv))

    ng = len(DIL_GROUPS)
    biases = [jnp.asarray(_attn_window_bias(w, dil)) for w, dil in DIL_GROUPS]
    return pl.pallas_call(
        _attn_kernel,
        grid=(b, KV_PER_GROUP, s // Q_BLOCK),
        in_specs=([qspec(g) for g in range(ng)] + [kvspec(g) for g in range(ng)] * 2
                  + [pl.BlockSpec(bias.shape, lambda bi, kv, qb: (0, 0)) for bias in biases]),
        out_specs=pl.BlockSpec((1, Q_BLOCK, qcols), lambda bi, kv, qb: (bi, qb, kv)),
        out_shape=jax.ShapeDtypeStruct((b, s, KV_PER_GROUP * qcols), BF16),
        compiler_params=_cparams(("parallel", "parallel", "arbitrary")),
        name="dilated_attention",
    )(q, q, q, k, k, k, v, v, v, *biases)


def kernel(x, hgrn_norm, hgrn_w_in, hgrn_lb_logits, hgrn_out_norm, hgrn_w_out, kv_norm, w_kv,
           k_norm, attn_norm, w_q, q_norm, w_o, ffn_norm, peer_w_q, peer_sub_keys, peer_u, peer_v):
    b, s, d = x.shape
    t = b * s
    depth = ffn_norm.shape[0]
    n_a = hgrn_norm.shape[0]
    h = x.reshape(t, d)
    tm = min(1024, t)
    for layer in range(depth):
        if layer < n_a:
            y = norm_matmul(h, hgrn_norm[layer], hgrn_w_in[layer].astype(BF16), tm=tm, tn=512)
            o = hgrn_scan(y.reshape(b, s, 4 * d), hgrn_lb_logits, hgrn_out_norm[layer],
                          layer=layer, seq_block=min(1024, s))
            h = matmul_residual(o.reshape(t, d), hgrn_w_out[layer].astype(BF16), h, tm=tm)
        else:
            j = layer - n_a
            qr, kr, vr = attn_proj(h, kv_norm, attn_norm[j], w_kv.astype(BF16),
                                   w_q[j].astype(BF16), k_norm, q_norm[j], s, tm=min(512, s))
            if j == 0:
                k_sh, v_sh = kr.reshape(b, s, -1), vr.reshape(b, s, -1)
            ao = dilated_attention(qr.reshape(b, s, -1), k_sh, v_sh)
            h = matmul_residual(ao.reshape(t, -1), w_o[j].astype(BF16), h, tm=tm)
        h = peer_layer(h, ffn_norm[layer], peer_w_q[layer], peer_sub_keys[layer],
                       peer_u[layer], peer_v[layer])
    return h.reshape(b, s, d)
```

```python
import functools
import math

import numpy as np
import jax
import jax.numpy as jnp
from jax import lax
from jax.experimental import pallas as pl
from jax.experimental.pallas import tpu as pltpu

F32 = jnp.float32
BF16 = jnp.bfloat16
EPS = 1e-6

LANES = 128
HEAD = 128
HG_CHUNK = 128
DIL_GROUPS = ((128, 1), (512, 4), (2048, 16))
KV_PER_GROUP = 2
Q_PER_KV = 2
ROT_DIM = HEAD // 4
ROPE_THETA = 500000.0
Q_BLOCK = 128
PEER_HEADS = 8
PEER_TOPK = 16
N_KEYS = 128
VMEM_LIMIT = 56 * 1024 * 1024

NT_DIMS = (((1,), (1,)), ((), ()))


def _cparams(sem):
    return pltpu.CompilerParams(dimension_semantics=sem, vmem_limit_bytes=VMEM_LIMIT)


def _rms(x):
    return x * lax.rsqrt(jnp.mean(x * x, axis=-1, keepdims=True) + EPS)


def _sigmoid(x):
    return 1.0 / (1.0 + jnp.exp(-x))


def _norm_matmul_kernel(x_ref, g_ref, w_ref, o_ref, xn_ref):
    @pl.when(pl.program_id(1) == 0)
    def _():
        xn_ref[...] = (_rms(x_ref[...]) * g_ref[...]).astype(xn_ref.dtype)

    o_ref[...] = jnp.dot(xn_ref[...], w_ref[...],
                         preferred_element_type=F32).astype(o_ref.dtype)


def norm_matmul(x, gain, w, *, tm, tn, out_dtype=F32):
    t, d = x.shape
    n = w.shape[1]
    return pl.pallas_call(
        _norm_matmul_kernel,
        grid=(t // tm, n // tn),
        in_specs=[pl.BlockSpec((tm, d), lambda i, j: (i, 0)),
                  pl.BlockSpec((1, d), lambda i, j: (0, 0)),
                  pl.BlockSpec((d, tn), lambda i, j: (0, j))],
        out_specs=pl.BlockSpec((tm, tn), lambda i, j: (i, j)),
        out_shape=jax.ShapeDtypeStruct((t, n), out_dtype),
        scratch_shapes=[pltpu.VMEM((tm, d), BF16)],
        compiler_params=_cparams(("parallel", "arbitrary")),
        name="norm_matmul",
    )(x, gain.reshape(1, d), w)


def _matmul_res_kernel(a_ref, w_ref, r_ref, o_ref):
    o_ref[...] = r_ref[...] + jnp.dot(a_ref[...], w_ref[...], preferred_element_type=F32)


def matmul_residual(a, w, res, *, tm):
    t, k = a.shape
    n = w.shape[1]
    return pl.pallas_call(
        _matmul_res_kernel,
        grid=(t // tm,),
        in_specs=[pl.BlockSpec((tm, k), lambda i: (i, 0)),
                  pl.BlockSpec((k, n), lambda i: (0, 0)),
                  pl.BlockSpec((tm, n), lambda i: (i, 0))],
        out_specs=pl.BlockSpec((tm, n), lambda i: (i, 0)),
        out_shape=jax.ShapeDtypeStruct((t, n), F32),
        compiler_params=_cparams(("parallel",)),
        name="matmul_residual",
    )(a, w, res)


def _hgrn_levels(c):
    return int(math.log2(c))


def _hgrn_sum_matrix(c):
    nlev = _hgrn_levels(c)
    m = np.zeros(((nlev + 2) * c, c), np.float32)
    for l in range(nlev):
        h = c >> (l + 1)
        for r in range(c):
            mid = (r // (2 * h)) * 2 * h + h
            if r >= mid:
                m[l * c + r, mid:r + 1] = 1.0
            else:
                m[l * c + r, r + 1:mid] = 1.0
    for r in range(c):
        m[nlev * c + r, :r + 1] = 1.0
        m[(nlev + 1) * c + r, r + 1:] = 1.0
    return m


def _hgrn_kernel(q_ref, f_ref, i_ref, g_ref, lbl_ref, og_ref, m_ref, o_ref, st_ref, *,
                 layer, chunk, n_chunks):
    c = chunk
    nlev = _hgrn_levels(c)

    @pl.when(pl.program_id(2) == 0)
    def _():
        st_ref[...] = jnp.zeros_like(st_ref)

    lg = lbl_ref[...]
    e = jnp.exp(lg - jnp.max(lg, axis=0, keepdims=True))
    lb = jnp.sum(e[:layer + 1], axis=0, keepdims=True) / jnp.sum(e, axis=0, keepdims=True)
    og = og_ref[...]
    row = lax.broadcasted_iota(jnp.int32, (c, c), 0)
    col = lax.broadcasted_iota(jnp.int32, (c, c), 1)

    def body(ci, carry):
        sl = pl.ds(pl.multiple_of(ci * c, c), c)
        q = q_ref[0, sl, :]
        fr = f_ref[0, sl, :]
        v = i_ref[0, sl, :]
        gt = g_ref[0, sl, :]
        qa = q * _sigmoid(q)
        f = lb + (1.0 - lb) * _sigmoid(fr)
        k = 1.0 - f
        logf = jnp.log(f)
        hi = logf.astype(BF16)
        lo = (logf - hi.astype(F32)).astype(BF16)
        e2 = jnp.dot(m_ref[...], jnp.concatenate([hi, lo], axis=1),
                     preferred_element_type=F32)
        ex = e2[:, :HEAD] + e2[:, HEAD:]

        kb = k.astype(BF16)
        s = jnp.where(row == col,
                      lax.dot_general(qa.astype(BF16), kb, NT_DIMS, preferred_element_type=F32),
                      0.0)
        for l in range(nlev):
            sh = nlev - 1 - l
            x = jnp.exp(ex[l * c:(l + 1) * c])
            p = lax.dot_general((qa * x).astype(BF16), (k * x).astype(BF16), NT_DIMS,
                                preferred_element_type=F32)
            mask = (((row >> (sh + 1)) == (col >> (sh + 1)))
                    & (((row >> sh) & 1) == 1) & (((col >> sh) & 1) == 0))
            s = jnp.where(mask, p, s)
        vb = v.astype(BF16)
        intra = jnp.dot(s.astype(BF16), vb, preferred_element_type=F32)
        b_incl = ex[nlev * c:(nlev + 1) * c]
        st = st_ref[...]
        inter = lax.dot_general((qa * jnp.exp(b_incl)).astype(BF16), st.astype(BF16), NT_DIMS,
                                preferred_element_type=F32)
        o = inter + intra
        y = _rms(o) * og * (gt * _sigmoid(gt))
        o_ref[0, sl, :] = y.astype(o_ref.dtype)

        k2 = (k * jnp.exp(ex[(nlev + 1) * c:(nlev + 2) * c])).astype(BF16)
        upd = jnp.dot(v.T.astype(BF16), k2, preferred_element_type=F32)
        st_ref[...] = st * jnp.exp(b_incl[c - 1:c, :]) + upd
        return carry

    lax.fori_loop(0, n_chunks, body, 0, unroll=4)


def hgrn_scan(y, lb_logits, out_gain, *, layer, seq_block):
    b, s, d4 = y.shape
    d = d4 // 4
    nh = d // HEAD
    c = min(HG_CHUNK, seq_block)
    msel = jnp.asarray(_hgrn_sum_matrix(c), BF16)
    nl = lb_logits.shape[0]
    kern = functools.partial(_hgrn_kernel, layer=layer, chunk=c, n_chunks=seq_block // c)

    def part(p):
        return pl.BlockSpec((1, seq_block, HEAD), lambda bi, hi, si, p=p: (bi, si, p * nh + hi))

    return pl.pallas_call(
        kern,
        grid=(b, nh, s // seq_block),
        in_specs=[part(0), part(1), part(2), part(3),
                  pl.BlockSpec((nl, HEAD), lambda bi, hi, si: (0, hi)),
                  pl.BlockSpec((1, HEAD), lambda bi, hi, si: (0, 0)),
                  pl.BlockSpec(msel.shape, lambda bi, hi, si: (0, 0))],
        out_specs=pl.BlockSpec((1, seq_block, HEAD), lambda bi, hi, si: (bi, si, hi)),
        out_shape=jax.ShapeDtypeStruct((b, s, d), BF16),
        scratch_shapes=[pltpu.VMEM((HEAD, HEAD), F32)],
        compiler_params=_cparams(("parallel", "parallel", "arbitrary")),
        name="hgrn_scan",
    )(y, y, y, y, lb_logits, out_gain.reshape(1, HEAD), msel)


def _peer_prep_kernel(h_ref, g_ref, wq_ref, keys_ref, xn_ref, s1_ref, s2_ref):
    xn = (_rms(h_ref[...]) * g_ref[...]).astype(BF16)
    xn_ref[...] = xn
    q = jnp.dot(xn, wq_ref[...], preferred_element_type=F32)
    k0 = keys_ref[0].astype(BF16)
    k1 = keys_ref[1].astype(BF16)
    for hh in range(PEER_HEADS):
        base = hh * 2 * HEAD
        q1 = q[:, base:base + HEAD].astype(BF16)
        q2 = q[:, base + HEAD:base + 2 * HEAD].astype(BF16)
        s1_ref[hh] = lax.dot_general(k0, q1, NT_DIMS, preferred_element_type=F32)
        s2_ref[hh] = lax.dot_general(k1, q2, NT_DIMS, preferred_element_type=F32)


def peer_prep(h, gain, wq, keys, *, tm):
    t, d = h.shape
    nq = wq.shape[1]
    return pl.pallas_call(
        _peer_prep_kernel,
        grid=(t // tm,),
        in_specs=[pl.BlockSpec((tm, d), lambda i: (i, 0)),
                  pl.BlockSpec((1, d), lambda i: (0, 0)),
                  pl.BlockSpec((d, nq), lambda i: (0, 0)),
                  pl.BlockSpec(keys.shape, lambda i: (0, 0, 0))],
        out_specs=[pl.BlockSpec((tm, d), lambda i: (i, 0)),
                   pl.BlockSpec((PEER_HEADS, N_KEYS, tm), lambda i: (0, 0, i)),
                   pl.BlockSpec((PEER_HEADS, N_KEYS, tm), lambda i: (0, 0, i))],
        out_shape=[jax.ShapeDtypeStruct((t, d), BF16),
                   jax.ShapeDtypeStruct((PEER_HEADS, N_KEYS, t), F32),
                   jax.ShapeDtypeStruct((PEER_HEADS, N_KEYS, t), F32)],
        compiler_params=_cparams(("parallel",)),
        name="peer_prep",
    )(h, gain.reshape(1, d), wq, keys)


def _oddeven_merge(lo, hi, r):
    step = r * 2
    if step < hi - lo:
        yield from _oddeven_merge(lo, hi, step)
        yield from _oddeven_merge(lo + r, hi, step)
        yield from [(i, i + r) for i in range(lo + r, hi - r, step)]
    else:
        yield (lo, lo + r)


def _oddeven_sort(lo, hi):
    if hi - lo >= 1:
        mid = lo + (hi - lo) // 2
        yield from _oddeven_sort(lo, mid)
        yield from _oddeven_sort(mid + 1, hi)
        yield from _oddeven_merge(lo, hi, 1)


SORT16 = tuple(_oddeven_sort(0, PEER_TOPK - 1))
BITONIC16 = tuple((i, i + h) for h in (8, 4, 2, 1) for i in range(PEER_TOPK) if i % (2 * h) < h)
SUBLANES = 8


def _compare_exchange(v, pairs):
    v = list(v)
    for i, j in pairs:
        v[i], v[j] = jnp.maximum(v[i], v[j]), jnp.minimum(v[i], v[j])
    return v


def _merge_sublane_lists(v):
    n = len(v)
    for shift in (4, 2, 1):
        w = [pltpu.roll(x, shift, 0) for x in v]
        v = _compare_exchange([jnp.maximum(v[k], w[n - 1 - k]) for k in range(n)], BITONIC16)
    return v


def _top16(x):
    slabs = [x[SUBLANES * k:SUBLANES * (k + 1), :] for k in range(PEER_TOPK)]
    return _merge_sublane_lists(_compare_exchange(slabs, SORT16))


def _peer_topk_kernel(s1_ref, s2_ref, cnt_ref, ar_ref, rk_ref, bt_ref):
    k = PEER_TOPK
    lanes = s1_ref.shape[2]
    sub = lax.broadcasted_iota(jnp.int32, (SUBLANES, lanes), 0)

    def head(hh, carry):
        s1 = s1_ref[hh]
        s2 = s2_ref[hh]
        va = _top16(s1)
        vb = _top16(s2)
        bd = vb[0]
        for p in range(1, SUBLANES):
            bd = jnp.where(sub == p, vb[p], bd)
        cand = []
        for i in range(k):
            c = va[i] + bd
            lim = k // (i + 1)
            cand.append(c if lim >= SUBLANES else jnp.where(sub < lim, c, -jnp.inf))
        tail = [va[0] + vb[SUBLANES + q] for q in range(k - SUBLANES)]
        top16 = _merge_sublane_lists(cand)
        tau = top16[k - SUBLANES - 1]
        for q in range(k - SUBLANES):
            tau = jnp.minimum(tau, jnp.maximum(top16[k - 1 - q], tail[q]))
        top = va[0] + vb[0]
        zc = jnp.zeros((SUBLANES, lanes), F32)
        for c in cand:
            zc = zc + jnp.where(c >= tau, jnp.exp(c - top), 0.0)
        zt = jnp.zeros((SUBLANES, lanes), F32)
        for c in tail:
            zt = zt + jnp.where(c >= tau, jnp.exp(c - top), 0.0)
        z = jnp.sum(zc, axis=0, keepdims=True) + zt[0:1, :]
        tau_row = tau[0:1, :]
        b_rows = [x[0:1, :] for x in vb]
        cnt = jnp.zeros(s1.shape, F32)
        rank2 = jnp.zeros(s2.shape, F32)
        for j in range(k):
            cnt = cnt + jnp.where(s1 + b_rows[j] >= tau_row, 1.0, 0.0)
            rank2 = rank2 + jnp.where(b_rows[j] > s2, 1.0, 0.0)
        cnt_ref[hh] = cnt
        ar_ref[hh] = jnp.exp(s1 - va[0][0:1, :]) / z
        bt = jnp.exp(s2 - b_rows[0])
        for lb in range(lanes // LANES):
            ln = slice(lb * LANES, (lb + 1) * LANES)
            rk_ref[hh, lb] = pltpu.bitcast(rank2[:, ln].astype(BF16), jnp.uint32)
            bt_ref[hh, lb] = pltpu.bitcast(bt[:, ln].astype(BF16), jnp.uint32)
        return carry

    lax.fori_loop(0, PEER_HEADS, head, 0)


def peer_topk(s1t, s2t, *, tt):
    nh, nk, t = s1t.shape
    spec = pl.BlockSpec((nh, nk, tt), lambda i: (0, 0, i))
    shp = jax.ShapeDtypeStruct((nh, nk, t), F32)
    tspec = pl.BlockSpec((nh, tt // LANES, nk // 2, LANES), lambda i: (0, i, 0, 0))
    tshp = jax.ShapeDtypeStruct((nh, t // LANES, nk // 2, LANES), jnp.uint32)
    return pl.pallas_call(
        _peer_topk_kernel,
        grid=(t // tt,),
        in_specs=[spec, spec],
        out_specs=[spec, spec, tspec, tspec],
        out_shape=[shp, shp, tshp, tshp],
        compiler_params=_cparams(("parallel",)),
        name="peer_topk",
    )(s1t, s2t)


GELU_C0 = math.sqrt(2.0 / math.pi)
GELU_C1 = GELU_C0 * 0.044715


def _gelu(z):
    return 0.5 * z * (1.0 + jnp.tanh(z * (GELU_C0 + GELU_C1 * (z * z))))


def _peer_main_kernel(h_ref, xn_ref, u_ref, vt_ref, cnt_ref, ar_ref, rk_ref, bt_ref, o_ref,
                      acc_ref, zt_ref, a_ref, *, n_blocks, blocks_per_tile, lane_blocks_per_iter):
    g_step = pl.program_id(0)
    et, tt = zt_ref.shape
    n_e1 = et // N_KEYS
    jb = jnp.maximum(g_step - 1, 0) % blocks_per_tile

    @pl.when(g_step == 0)
    def _():
        a_ref[...] = jnp.zeros_like(a_ref)

    @pl.when(jb == 0)
    def _():
        acc_ref[...] = jnp.zeros_like(acc_ref)

    acc_ref[...] += jnp.dot(vt_ref[...], a_ref[...], preferred_element_type=F32)
    zt_ref[...] = lax.dot_general(u_ref[...], xn_ref[...], NT_DIMS, preferred_element_type=F32)

    @pl.when((g_step >= 1) & (jb == blocks_per_tile - 1))
    def _():
        o_ref[...] = h_ref[...] + acc_ref[...].T

    def lane_group(li, carry):
        for sub in range(lane_blocks_per_iter):
            lb = li * lane_blocks_per_iter + sub
            ln = pl.ds(pl.multiple_of(lb * LANES, LANES), LANES)
            for c in range(n_e1):
                rows = slice(c * N_KEYS, (c + 1) * N_KEYS)
                g = jnp.zeros((N_KEYS, LANES), BF16)
                for hh in range(PEER_HEADS):
                    cnt = cnt_ref[hh, c:c + 1, ln].astype(BF16)
                    ar = ar_ref[hh, c:c + 1, ln].astype(BF16)
                    rk = pltpu.bitcast(rk_ref[hh, lb], BF16)
                    bt = pltpu.bitcast(bt_ref[hh, lb], BF16)
                    g = g + jnp.where(rk < cnt, bt * ar, 0)
                a_ref[rows, ln] = _gelu(zt_ref[rows, ln].astype(BF16)) * g
        return carry

    lax.fori_loop(0, tt // (LANES * lane_blocks_per_iter), lane_group, 0)


def peer_main(h, xn, u, vt, cnt, ar, rank_tiles, bt_tiles, *, tt, et, lane_blocks_per_iter):
    t, d = h.shape
    ne = u.shape[0]
    n_e1 = et // N_KEYS
    bpt = ne // et
    n_blocks = (t // tt) * bpt
    kern = functools.partial(_peer_main_kernel, n_blocks=n_blocks, blocks_per_tile=bpt,
                             lane_blocks_per_iter=lane_blocks_per_iter)

    def cur(g):
        return jnp.minimum(g, n_blocks - 1)

    def prev(g):
        return jnp.maximum(g - 1, 0)

    row_spec = pl.BlockSpec((PEER_HEADS, n_e1, tt), lambda g: (0, cur(g) % bpt, cur(g) // bpt))
    tok_spec = pl.BlockSpec((PEER_HEADS, tt // LANES, N_KEYS // 2, LANES),
                            lambda g: (0, cur(g) // bpt, 0, 0))
    return pl.pallas_call(
        kern,
        grid=(n_blocks + 1,),
        in_specs=[pl.BlockSpec((tt, d), lambda g: (prev(g) // bpt, 0)),
                  pl.BlockSpec((tt, d), lambda g: (cur(g) // bpt, 0)),
                  pl.BlockSpec((et, d), lambda g: (cur(g) % bpt, 0)),
                  pl.BlockSpec((d, et), lambda g: (0, prev(g) % bpt)),
                  row_spec, row_spec, tok_spec, tok_spec],
        out_specs=pl.BlockSpec((tt, d), lambda g: (prev(g) // bpt, 0)),
        out_shape=jax.ShapeDtypeStruct((t, d), F32),
        scratch_shapes=[pltpu.VMEM((d, tt), F32), pltpu.VMEM((et, tt), F32),
                        pltpu.VMEM((et, tt), BF16)],
        compiler_params=_cparams(("arbitrary",)),
        name="peer_main",
    )(h, xn, u, vt, cnt, ar, rank_tiles, bt_tiles)


def peer_layer(h, gain, wq, keys, u, v):
    t = h.shape[0]
    xn, s1t, s2t = peer_prep(h, gain, wq.astype(BF16), keys, tm=min(512, t))
    cnt, ar, rank_tiles, bt_tiles = peer_topk(s1t, s2t, tt=min(256, t))
    return peer_main(h, xn, u.astype(BF16), v.astype(BF16).T, cnt, ar, rank_tiles, bt_tiles,
                     tt=min(512, t), et=2048, lane_blocks_per_iter=1)


def _rope_tables(s):
    half = ROT_DIM // 2
    inv = ROPE_THETA ** (-jnp.arange(half, dtype=F32) * 2.0 / ROT_DIM)
    ang = jnp.arange(s).astype(F32)[:, None] * inv[None, :]
    cos, sin = jnp.cos(ang), jnp.sin(ang)
    pad = HEAD - ROT_DIM
    ones = jnp.ones((s, pad), F32)
    zeros = jnp.zeros((s, pad), F32)
    zh = jnp.zeros((s, half), F32)
    ct = jnp.concatenate([cos, cos, ones], axis=1)
    at = jnp.concatenate([-sin, zh, zeros], axis=1)
    bt = jnp.concatenate([zh, sin, zeros], axis=1)
    return ct, at, bt


def _attn_proj_kernel(h_ref, kvg_ref, ag_ref, wkv_ref, wq_ref, kn_ref, qn_ref,
                      ct_ref, at_ref, bt_ref, q_ref, k_ref, v_ref):
    y = _rms(h_ref[...])
    kv = jnp.dot((y * kvg_ref[...]).astype(BF16), wkv_ref[...], preferred_element_type=F32)
    qq = jnp.dot((y * ag_ref[...]).astype(BF16), wq_ref[...], preferred_element_type=F32)
    ct, at, bt = ct_ref[...], at_ref[...], bt_ref[...]
    half = ROT_DIM // 2

    def head_norm_rope(x, gain):
        n = _rms(x) * gain
        return n * ct + pltpu.roll(n, HEAD - half, 1) * at + pltpu.roll(n, half, 1) * bt

    kw = k_ref.shape[1]
    for hd in range(kw // HEAD):
        cs = slice(hd * HEAD, (hd + 1) * HEAD)
        k_ref[:, cs] = head_norm_rope(kv[:, cs], kn_ref[...]).astype(k_ref.dtype)
    v_ref[...] = kv[:, kw:].astype(v_ref.dtype)
    for hd in range(q_ref.shape[1] // HEAD):
        cs = slice(hd * HEAD, (hd + 1) * HEAD)
        q_ref[:, cs] = head_norm_rope(qq[:, cs], qn_ref[...]).astype(q_ref.dtype)


def attn_proj(h, kv_gain, a_gain, wkv, wq, k_norm, q_norm, seq, *, tm):
    t, d = h.shape
    kw = wkv.shape[1] // 2
    qw = wq.shape[1]
    ct, at, bt = _rope_tables(seq)
    ns = seq // tm
    tab = pl.BlockSpec((tm, HEAD), lambda i: (i % ns, 0))
    vec = lambda n: pl.BlockSpec((1, n), lambda i: (0, 0))
    return pl.pallas_call(
        _attn_proj_kernel,
        grid=(t // tm,),
        in_specs=[pl.BlockSpec((tm, d), lambda i: (i, 0)), vec(d), vec(d),
                  pl.BlockSpec(wkv.shape, lambda i: (0, 0)),
                  pl.BlockSpec(wq.shape, lambda i: (0, 0)),
                  vec(HEAD), vec(HEAD), tab, tab, tab],
        out_specs=[pl.BlockSpec((tm, qw), lambda i: (i, 0)),
                   pl.BlockSpec((tm, kw), lambda i: (i, 0)),
                   pl.BlockSpec((tm, kw), lambda i: (i, 0))],
        out_shape=[jax.ShapeDtypeStruct((t, qw), BF16),
                   jax.ShapeDtypeStruct((t, kw), BF16),
                   jax.ShapeDtypeStruct((t, kw), BF16)],
        compiler_params=_cparams(("parallel",)),
        name="attn_proj",
    )(h, kv_gain.reshape(1, d), a_gain.reshape(1, d), wkv, wq,
      k_norm.reshape(1, HEAD), q_norm.reshape(1, HEAD), ct, at, bt)


ATTN_EXP2_SCALE = HEAD ** -0.5 * math.log2(math.e)
MAX_WINDOW = max(w for w, _ in DIL_GROUPS)


def _attn_window_bias(w, dil):
    rows = Q_PER_KV * Q_BLOCK
    tq = w + (np.arange(rows)[:, None] % Q_BLOCK)
    diff = tq - np.arange(w + Q_BLOCK)[None, :]
    valid = (diff >= 0) & (diff <= w) & (diff % dil == 0)
    return np.where(valid, 0.0, -np.inf).astype(np.float32)


def _attn_kernel(q0_ref, q1_ref, q2_ref, k0_ref, k1_ref, k2_ref, v0_ref, v1_ref, v2_ref,
                 b0_ref, b1_ref, b2_ref, o_ref):
    qb = pl.program_id(2)
    t0 = qb * Q_BLOCK
    rows = Q_PER_KV * Q_BLOCK
    groups = list(zip(DIL_GROUPS, (q0_ref, q1_ref, q2_ref), (k0_ref, k1_ref, k2_ref),
                      (v0_ref, v1_ref, v2_ref), (b0_ref, b1_ref, b2_ref)))

    def raw_scores(q_ref, k_ref, ks, span):
        q2 = jnp.concatenate([q_ref[0, :, r * HEAD:(r + 1) * HEAD] for r in range(Q_PER_KV)],
                             axis=0)
        return lax.dot_general(q2, k_ref[0, pl.ds(ks, span), :], NT_DIMS,
                               preferred_element_type=F32)

    def finish(scores, starts):
        m = functools.reduce(jnp.maximum, [jnp.max(s, axis=-1, keepdims=True) for s in scores])
        den = jnp.zeros((rows, 1), F32)
        out = jnp.zeros((rows, HEAD), F32)
        for s, ks, ((w, _), _, _, v_ref, _) in zip(scores, starts, groups):
            p = jnp.exp2((s - m) * ATTN_EXP2_SCALE)
            den = den + jnp.sum(p, axis=-1, keepdims=True)
            out = out + jnp.dot(p.astype(BF16), v_ref[0, pl.ds(ks, w + Q_BLOCK), :],
                                preferred_element_type=F32)
        out = out / den
        for r in range(Q_PER_KV):
            o_ref[0, :, r * HEAD:(r + 1) * HEAD] = (
                out[r * Q_BLOCK:(r + 1) * Q_BLOCK].astype(o_ref.dtype))

    @pl.when(t0 >= MAX_WINDOW)
    def _():
        scores, starts = [], []
        for (w, dil), q_ref, k_ref, _, b_ref in groups:
            ks = pl.multiple_of(t0 - w, Q_BLOCK)
            scores.append(raw_scores(q_ref, k_ref, ks, w + Q_BLOCK) + b_ref[...])
            starts.append(ks)
        finish(scores, starts)

    @pl.when(t0 < MAX_WINDOW)
    def _():
        scores, starts = [], []
        for (w, dil), q_ref, k_ref, _, _ in groups:
            span = w + Q_BLOCK
            ks = pl.multiple_of(jnp.maximum(t0 - w, 0), Q_BLOCK)
            tq = t0 + (lax.broadcasted_iota(jnp.int32, (rows, span), 0) & (Q_BLOCK - 1))
            diff = tq - (ks + lax.broadcasted_iota(jnp.int32, (rows, span), 1))
            valid = (diff >= 0) & (diff <= w) & ((diff & (dil - 1)) == 0)
            scores.append(jnp.where(valid, raw_scores(q_ref, k_ref, ks, span), -jnp.inf))
            starts.append(ks)
        finish(scores, starts)


def dilated_attention(q, k, v):
    b, s, _ = q.shape
    qcols = Q_PER_KV * HEAD

    def qspec(g):
        return pl.BlockSpec((1, Q_BLOCK, qcols), lambda bi, kv, qb, g=g: (bi, qb, g * KV_PER_GROUP + kv))

    def kvspec(g):
        return pl.BlockSpec((1, s, HEAD), lambda bi, kv, qb, g=g: (bi, 0, g * KV_PER_GROUP + kv))

    ng = len(DIL_GROUPS)
    biases = [jnp.asarray(_attn_window_bias(w, dil)) for w, dil in DIL_GROUPS]
    return pl.pallas_call(
        _attn_kernel,
        grid=(b, KV_PER_GROUP, s // Q_BLOCK),
        in_specs=([qspec(g) for g in range(ng)] + [kvspec(g) for g in range(ng)] * 2
                  + [pl.BlockSpec(bias.shape, lambda bi, kv, qb: (0, 0)) for bias in biases]),
        out_specs=pl.BlockSpec((1, Q_BLOCK, qcols), lambda bi, kv, qb: (bi, qb, kv)),
        out_shape=jax.ShapeDtypeStruct((b, s, KV_PER_GROUP * qcols), BF16),
        compiler_params=_cparams(("parallel", "parallel", "arbitrary")),
        name="dilated_attention",
    )(q, q, q, k, k, k, v, v, v, *biases)


def kernel(x, hgrn_norm, hgrn_w_in, hgrn_lb_logits, hgrn_out_norm, hgrn_w_out, kv_norm, w_kv,
           k_norm, attn_norm, w_q, q_norm, w_o, ffn_norm, peer_w_q, peer_sub_keys, peer_u, peer_v):
    b, s, d = x.shape
    t = b * s
    depth = ffn_norm.shape[0]
    n_a = hgrn_norm.shape[0]
    h = x.reshape(t, d)
    tm = min(1024, t)
    for layer in range(depth):
        if layer < n_a:
            y = norm_matmul(h, hgrn_norm[layer], hgrn_w_in[layer].astype(BF16), tm=tm, tn=512)
            o = hgrn_scan(y.reshape(b, s, 4 * d), hgrn_lb_logits, hgrn_out_norm[layer],
                          layer=layer, seq_block=min(1024, s))
            h = matmul_residual(o.reshape(t, d), hgrn_w_out[layer].astype(BF16), h, tm=tm)
        else:
            j = layer - n_a
            qr, kr, vr = attn_proj(h, kv_norm, attn_norm[j], w_kv.astype(BF16),
                                   w_q[j].astype(BF16), k_norm, q_norm[j], s, tm=min(512, s))
            if j == 0:
                k_sh, v_sh = kr.reshape(b, s, -1), vr.reshape(b, s, -1)
            ao = dilated_attention(qr.reshape(b, s, -1), k_sh, v_sh)
            h = matmul_residual(ao.reshape(t, -1), w_o[j].astype(BF16), h, tm=tm)
        h = peer_layer(h, ffn_norm[layer], peer_w_q[layer], peer_sub_keys[layer],
                       peer_u[layer], peer_v[layer])
    return h.reshape(b, s, d)
```

```python
import functools
import math

import numpy as np
import jax
import jax.numpy as jnp
from jax import lax
from jax.experimental import pallas as pl
from jax.experimental.pallas import tpu as pltpu

F32 = jnp.float32
BF16 = jnp.bfloat16
EPS = 1e-6

LANES = 128
HEAD = 128
HG_CHUNK = 128
HG_SAFE_EXPONENT = 80.0
DIL_GROUPS = ((128, 1), (512, 4), (2048, 16))
KV_PER_GROUP = 2
Q_PER_KV = 2
ROT_DIM = HEAD // 4
ROPE_THETA = 500000.0
Q_BLOCK = 128
PEER_HEADS = 8
PEER_TOPK = 16
N_KEYS = 128
VMEM_LIMIT = 56 * 1024 * 1024

NT_DIMS = (((1,), (1,)), ((), ()))


def _cparams(sem):
    return pltpu.CompilerParams(dimension_semantics=sem, vmem_limit_bytes=VMEM_LIMIT)


def _rms(x):
    return x * lax.rsqrt(jnp.mean(x * x, axis=-1, keepdims=True) + EPS)


def _sigmoid(x):
    return 1.0 / (1.0 + jnp.exp(-x))


def _norm_matmul_kernel(x_ref, g_ref, w_ref, o_ref, xn_ref):
    @pl.when(pl.program_id(1) == 0)
    def _():
        xn_ref[...] = (_rms(x_ref[...]) * g_ref[...]).astype(xn_ref.dtype)

    o_ref[...] = jnp.dot(xn_ref[...], w_ref[...],
                         preferred_element_type=F32).astype(o_ref.dtype)


def norm_matmul(x, gain, w, *, tm, tn, out_dtype=F32):
    t, d = x.shape
    n = w.shape[1]
    return pl.pallas_call(
        _norm_matmul_kernel,
        grid=(t // tm, n // tn),
        in_specs=[pl.BlockSpec((tm, d), lambda i, j: (i, 0)),
                  pl.BlockSpec((1, d), lambda i, j: (0, 0)),
                  pl.BlockSpec((d, tn), lambda i, j: (0, j))],
        out_specs=pl.BlockSpec((tm, tn), lambda i, j: (i, j)),
        out_shape=jax.ShapeDtypeStruct((t, n), out_dtype),
        scratch_shapes=[pltpu.VMEM((tm, d), BF16)],
        compiler_params=_cparams(("parallel", "arbitrary")),
        name="norm_matmul",
    )(x, gain.reshape(1, d), w)


def _matmul_res_kernel(a_ref, w_ref, r_ref, o_ref):
    o_ref[...] = r_ref[...] + jnp.dot(a_ref[...], w_ref[...], preferred_element_type=F32)


def matmul_residual(a, w, res, *, tm):
    t, k = a.shape
    n = w.shape[1]
    return pl.pallas_call(
        _matmul_res_kernel,
        grid=(t // tm,),
        in_specs=[pl.BlockSpec((tm, k), lambda i: (i, 0)),
                  pl.BlockSpec((k, n), lambda i: (0, 0)),
                  pl.BlockSpec((tm, n), lambda i: (i, 0))],
        out_specs=pl.BlockSpec((tm, n), lambda i: (i, 0)),
        out_shape=jax.ShapeDtypeStruct((t, n), F32),
        compiler_params=_cparams(("parallel",)),
        name="matmul_residual",
    )(a, w, res)


def _hgrn_levels(c):
    return int(math.log2(c))


def _hgrn_sum_matrix(c):
    nlev = _hgrn_levels(c)
    m = np.zeros(((nlev + 2) * c, c), np.float32)
    for l in range(nlev):
        h = c >> (l + 1)
        for r in range(c):
            mid = (r // (2 * h)) * 2 * h + h
            if r >= mid:
                m[l * c + r, mid:r + 1] = 1.0
            else:
                m[l * c + r, r + 1:mid] = 1.0
    for r in range(c):
        m[nlev * c + r, :r + 1] = 1.0
        m[(nlev + 1) * c + r, r + 1:] = 1.0
    return m


def _hgrn_kernel(q_ref, f_ref, i_ref, g_ref, lbl_ref, og_ref, m_ref, o_ref, st_ref, *,
                 layer, chunk, n_chunks):
    c = chunk
    nlev = _hgrn_levels(c)

    @pl.when(pl.program_id(2) == 0)
    def _():
        st_ref[...] = jnp.zeros_like(st_ref)

    lg = lbl_ref[...]
    e = jnp.exp(lg - jnp.max(lg, axis=0, keepdims=True))
    lb = jnp.sum(e[:layer + 1], axis=0, keepdims=True) / jnp.sum(e, axis=0, keepdims=True)
    og = og_ref[...]
    row = lax.broadcasted_iota(jnp.int32, (c, c), 0)
    col = lax.broadcasted_iota(jnp.int32, (c, c), 1)

    def body(ci, carry, *, mild):
        sl = pl.ds(pl.multiple_of(ci * c, c), c)
        q = q_ref[0, sl, :]
        fr = f_ref[0, sl, :]
        v = i_ref[0, sl, :]
        gt = g_ref[0, sl, :]
        qa = q * _sigmoid(q)
        f = lb + (1.0 - lb) * _sigmoid(fr)
        k = 1.0 - f
        logf = jnp.log(f)
        hi = logf.astype(BF16)
        lo = (logf - hi.astype(F32)).astype(BF16)
        g2 = jnp.concatenate([hi, lo], axis=1)
        if mild:
            e2 = jnp.dot(m_ref[nlev * c:(nlev + 1) * c, :], g2, preferred_element_type=F32)
            b_incl = e2[:, :HEAD] + e2[:, HEAD:]
            dmid = b_incl - b_incl[c // 2 - 1:c // 2, :]
            p = lax.dot_general((qa * jnp.exp(dmid)).astype(BF16),
                                (k * jnp.exp(-dmid)).astype(BF16), NT_DIMS,
                                preferred_element_type=F32)
            s = jnp.where(row >= col, p, 0.0)
            rev = b_incl[c - 1:c, :] - b_incl
        else:
            e2 = jnp.dot(m_ref[...], g2, preferred_element_type=F32)
            ex = e2[:, :HEAD] + e2[:, HEAD:]
            s = jnp.where(row == col,
                          lax.dot_general(qa.astype(BF16), k.astype(BF16), NT_DIMS,
                                          preferred_element_type=F32),
                          0.0)
            for l in range(nlev):
                sh = nlev - 1 - l
                x = jnp.exp(ex[l * c:(l + 1) * c])
                p = lax.dot_general((qa * x).astype(BF16), (k * x).astype(BF16), NT_DIMS,
                                    preferred_element_type=F32)
                mask = (((row >> (sh + 1)) == (col >> (sh + 1)))
                        & (((row >> sh) & 1) == 1) & (((col >> sh) & 1) == 0))
                s = jnp.where(mask, p, s)
            b_incl = ex[nlev * c:(nlev + 1) * c]
            rev = ex[(nlev + 1) * c:(nlev + 2) * c]
        vb = v.astype(BF16)
        intra = jnp.dot(s.astype(BF16), vb, preferred_element_type=F32)
        st = st_ref[...]
        inter = lax.dot_general((qa * jnp.exp(b_incl)).astype(BF16), st.astype(BF16), NT_DIMS,
                                preferred_element_type=F32)
        o = inter + intra
        y = _rms(o) * og * (gt * _sigmoid(gt))
        o_ref[0, sl, :] = y.astype(o_ref.dtype)

        k2 = (k * jnp.exp(rev)).astype(BF16)
        upd = jnp.dot(v.T.astype(BF16), k2, preferred_element_type=F32)
        st_ref[...] = st * jnp.exp(b_incl[c - 1:c, :]) + upd
        return carry

    mild = jnp.min(lb) >= math.exp(-HG_SAFE_EXPONENT / (c // 2))

    @pl.when(mild)
    def _():
        lax.fori_loop(0, n_chunks, functools.partial(body, mild=True), 0, unroll=4)

    @pl.when(jnp.logical_not(mild))
    def _():
        lax.fori_loop(0, n_chunks, functools.partial(body, mild=False), 0, unroll=4)


def hgrn_scan(y, lb_logits, out_gain, *, layer, seq_block):
    b, s, d4 = y.shape
    d = d4 // 4
    nh = d // HEAD
    c = min(HG_CHUNK, seq_block)
    msel = jnp.asarray(_hgrn_sum_matrix(c), BF16)
    nl = lb_logits.shape[0]
    kern = functools.partial(_hgrn_kernel, layer=layer, chunk=c, n_chunks=seq_block // c)

    def part(p):
        return pl.BlockSpec((1, seq_block, HEAD), lambda bi, hi, si, p=p: (bi, si, p * nh + hi))

    return pl.pallas_call(
        kern,
        grid=(b, nh, s // seq_block),
        in_specs=[part(0), part(1), part(2), part(3),
                  pl.BlockSpec((nl, HEAD), lambda bi, hi, si: (0, hi)),
                  pl.BlockSpec((1, HEAD), lambda bi, hi, si: (0, 0)),
                  pl.BlockSpec(msel.shape, lambda bi, hi, si: (0, 0))],
        out_specs=pl.BlockSpec((1, seq_block, HEAD), lambda bi, hi, si: (bi, si, hi)),
        out_shape=jax.ShapeDtypeStruct((b, s, d), BF16),
        scratch_shapes=[pltpu.VMEM((HEAD, HEAD), F32)],
        compiler_params=_cparams(("parallel", "parallel", "arbitrary")),
        name="hgrn_scan",
    )(y, y, y, y, lb_logits, out_gain.reshape(1, HEAD), msel)


def _peer_prep_kernel(h_ref, g_ref, wq_ref, keys_ref, xn_ref, s1_ref, s2_ref):
    xn = (_rms(h_ref[...]) * g_ref[...]).astype(BF16)
    xn_ref[...] = xn
    q = jnp.dot(xn, wq_ref[...], preferred_element_type=F32)
    k0 = keys_ref[0].astype(BF16)
    k1 = keys_ref[1].astype(BF16)
    for hh in range(PEER_HEADS):
        base = hh * 2 * HEAD
        q1 = q[:, base:base + HEAD].astype(BF16)
        q2 = q[:, base + HEAD:base + 2 * HEAD].astype(BF16)
        s1_ref[hh] = lax.dot_general(k0, q1, NT_DIMS, preferred_element_type=F32)
        s2_ref[hh] = lax.dot_general(k1, q2, NT_DIMS, preferred_element_type=F32)


def peer_prep(h, gain, wq, keys, *, tm):
    t, d = h.shape
    nq = wq.shape[1]
    return pl.pallas_call(
        _peer_prep_kernel,
        grid=(t // tm,),
        in_specs=[pl.BlockSpec((tm, d), lambda i: (i, 0)),
                  pl.BlockSpec((1, d), lambda i: (0, 0)),
                  pl.BlockSpec((d, nq), lambda i: (0, 0)),
                  pl.BlockSpec(keys.shape, lambda i: (0, 0, 0))],
        out_specs=[pl.BlockSpec((tm, d), lambda i: (i, 0)),
                   pl.BlockSpec((PEER_HEADS, N_KEYS, tm), lambda i: (0, 0, i)),
                   pl.BlockSpec((PEER_HEADS, N_KEYS, tm), lambda i: (0, 0, i))],
        out_shape=[jax.ShapeDtypeStruct((t, d), BF16),
                   jax.ShapeDtypeStruct((PEER_HEADS, N_KEYS, t), F32),
                   jax.ShapeDtypeStruct((PEER_HEADS, N_KEYS, t), F32)],
        compiler_params=_cparams(("parallel",)),
        name="peer_prep",
    )(h, gain.reshape(1, d), wq, keys)


def _oddeven_merge(lo, hi, r):
    step = r * 2
    if step < hi - lo:
        yield from _oddeven_merge(lo, hi, step)
        yield from _oddeven_merge(lo + r, hi, step)
        yield from [(i, i + r) for i in range(lo + r, hi - r, step)]
    else:
        yield (lo, lo + r)


def _oddeven_sort(lo, hi):
    if hi - lo >= 1:
        mid = lo + (hi - lo) // 2
        yield from _oddeven_sort(lo, mid)
        yield from _oddeven_sort(mid + 1, hi)
        yield from _oddeven_merge(lo, hi, 1)


SORT16 = tuple(_oddeven_sort(0, PEER_TOPK - 1))
BITONIC16 = tuple((i, i + h) for h in (8, 4, 2, 1) for i in range(PEER_TOPK) if i % (2 * h) < h)
SUBLANES = 8


def _compare_exchange(v, pairs):
    v = list(v)
    for i, j in pairs:
        v[i], v[j] = jnp.maximum(v[i], v[j]), jnp.minimum(v[i], v[j])
    return v


def _merge_sublane_lists(v):
    n = len(v)
    for shift in (4, 2, 1):
        w = [pltpu.roll(x, shift, 0) for x in v]
        v = _compare_exchange([jnp.maximum(v[k], w[n - 1 - k]) for k in range(n)], BITONIC16)
    return v


def _top16(x):
    slabs = [x[SUBLANES * k:SUBLANES * (k + 1), :] for k in range(PEER_TOPK)]
    return _merge_sublane_lists(_compare_exchange(slabs, SORT16))


def _peer_topk_kernel(s1_ref, s2_ref, cnt_ref, ar_ref, rk_ref, bt_ref):
    k = PEER_TOPK
    lanes = s1_ref.shape[2]
    sub = lax.broadcasted_iota(jnp.int32, (SUBLANES, lanes), 0)

    def head(hh, carry):
        s1 = s1_ref[hh]
        s2 = s2_ref[hh]
        va = _top16(s1)
        vb = _top16(s2)
        bd = vb[0]
        for p in range(1, SUBLANES):
            bd = jnp.where(sub == p, vb[p], bd)
        cand = []
        for i in range(k):
            c = va[i] + bd
            lim = k // (i + 1)
            cand.append(c if lim >= SUBLANES else jnp.where(sub < lim, c, -jnp.inf))
        tail = [va[0] + vb[SUBLANES + q] for q in range(k - SUBLANES)]
        top16 = _merge_sublane_lists(cand)
        tau = top16[k - SUBLANES - 1]
        for q in range(k - SUBLANES):
            tau = jnp.minimum(tau, jnp.maximum(top16[k - 1 - q], tail[q]))
        top = va[0] + vb[0]
        zc = jnp.zeros((SUBLANES, lanes), F32)
        for c in cand:
            zc = zc + jnp.where(c >= tau, jnp.exp(c - top), 0.0)
        zt = jnp.zeros((SUBLANES, lanes), F32)
        for c in tail:
            zt = zt + jnp.where(c >= tau, jnp.exp(c - top), 0.0)
        z = jnp.sum(zc, axis=0, keepdims=True) + zt[0:1, :]
        tau_row = tau[0:1, :]
        b_rows = [x[0:1, :] for x in vb]
        cnt = jnp.zeros(s1.shape, F32)
        rank2 = jnp.zeros(s2.shape, F32)
        for j in range(k):
            cnt = cnt + jnp.where(s1 + b_rows[j] >= tau_row, 1.0, 0.0)
            rank2 = rank2 + jnp.where(b_rows[j] > s2, 1.0, 0.0)
        cnt_ref[hh] = cnt
        ar_ref[hh] = jnp.exp(s1 - va[0][0:1, :]) / z
        bt = jnp.exp(s2 - b_rows[0])
        for lb in range(lanes // LANES):
            ln = slice(lb * LANES, (lb + 1) * LANES)
            rk_ref[hh, lb] = pltpu.bitcast(rank2[:, ln].astype(BF16), jnp.uint32)
            bt_ref[hh, lb] = pltpu.bitcast(bt[:, ln].astype(BF16), jnp.uint32)
        return carry

    lax.fori_loop(0, PEER_HEADS, head, 0)


def peer_topk(s1t, s2t, *, tt):
    nh, nk, t = s1t.shape
    spec = pl.BlockSpec((nh, nk, tt), lambda i: (0, 0, i))
    shp = jax.ShapeDtypeStruct((nh, nk, t), F32)
    tspec = pl.BlockSpec((nh, tt // LANES, nk // 2, LANES), lambda i: (0, i, 0, 0))
    tshp = jax.ShapeDtypeStruct((nh, t // LANES, nk // 2, LANES), jnp.uint32)
    return pl.pallas_call(
        _peer_topk_kernel,
        grid=(t // tt,),
        in_specs=[spec, spec],
        out_specs=[spec, spec, tspec, tspec],
        out_shape=[shp, shp, tshp, tshp],
        compiler_params=_cparams(("parallel",)),
        name="peer_topk",
    )(s1t, s2t)


GELU_C0 = math.sqrt(2.0 / math.pi)
GELU_C1 = GELU_C0 * 0.044715


def _gelu(z):
    return 0.5 * z * (1.0 + jnp.tanh(z * (GELU_C0 + GELU_C1 * (z * z))))


def _peer_main_kernel(h_ref, xn_ref, u_ref, vt_ref, cnt_ref, ar_ref, rk_ref, bt_ref, o_ref,
                      acc_ref, zt_ref, a_ref, *, n_blocks, blocks_per_tile, lane_blocks_per_iter):
    g_step = pl.program_id(0)
    et, tt = zt_ref.shape
    n_e1 = et // N_KEYS
    jb = jnp.maximum(g_step - 1, 0) % blocks_per_tile

    @pl.when(g_step == 0)
    def _():
        a_ref[...] = jnp.zeros_like(a_ref)

    @pl.when(jb == 0)
    def _():
        acc_ref[...] = jnp.zeros_like(acc_ref)

    acc_ref[...] += jnp.dot(vt_ref[...], a_ref[...], preferred_element_type=F32)
    zt_ref[...] = lax.dot_general(u_ref[...], xn_ref[...], NT_DIMS, preferred_element_type=F32)

    @pl.when((g_step >= 1) & (jb == blocks_per_tile - 1))
    def _():
        o_ref[...] = h_ref[...] + acc_ref[...].T

    def lane_group(li, carry):
        for sub in range(lane_blocks_per_iter):
            lb = li * lane_blocks_per_iter + sub
            ln = pl.ds(pl.multiple_of(lb * LANES, LANES), LANES)
            for c in range(n_e1):
                rows = slice(c * N_KEYS, (c + 1) * N_KEYS)
                g = jnp.zeros((N_KEYS, LANES), BF16)
                for hh in range(PEER_HEADS):
                    cnt = cnt_ref[hh, c:c + 1, ln].astype(BF16)
                    ar = ar_ref[hh, c:c + 1, ln].astype(BF16)
                    rk = pltpu.bitcast(rk_ref[hh, lb], BF16)
                    bt = pltpu.bitcast(bt_ref[hh, lb], BF16)
                    g = g + jnp.where(rk < cnt, bt * ar, 0)
                a_ref[rows, ln] = _gelu(zt_ref[rows, ln].astype(BF16)) * g
        return carry

    lax.fori_loop(0, tt // (LANES * lane_blocks_per_iter), lane_group, 0)


def peer_main(h, xn, u, vt, cnt, ar, rank_tiles, bt_tiles, *, tt, et, lane_blocks_per_iter):
    t, d = h.shape
    ne = u.shape[0]
    n_e1 = et // N_KEYS
    bpt = ne // et
    n_blocks = (t // tt) * bpt
    kern = functools.partial(_peer_main_kernel, n_blocks=n_blocks, blocks_per_tile=bpt,
                             lane_blocks_per_iter=lane_blocks_per_iter)

    def cur(g):
        return jnp.minimum(g, n_blocks - 1)

    def prev(g):
        return jnp.maximum(g - 1, 0)

    row_spec = pl.BlockSpec((PEER_HEADS, n_e1, tt), lambda g: (0, cur(g) % bpt, cur(g) // bpt))
    tok_spec = pl.BlockSpec((PEER_HEADS, tt // LANES, N_KEYS // 2, LANES),
                            lambda g: (0, cur(g) // bpt, 0, 0))
    return pl.pallas_call(
        kern,
        grid=(n_blocks + 1,),
        in_specs=[pl.BlockSpec((tt, d), lambda g: (prev(g) // bpt, 0)),
                  pl.BlockSpec((tt, d), lambda g: (cur(g) // bpt, 0)),
                  pl.BlockSpec((et, d), lambda g: (cur(g) % bpt, 0)),
                  pl.BlockSpec((d, et), lambda g: (0, prev(g) % bpt)),
                  row_spec, row_spec, tok_spec, tok_spec],
        out_specs=pl.BlockSpec((tt, d), lambda g: (prev(g) // bpt, 0)),
        out_shape=jax.ShapeDtypeStruct((t, d), F32),
        scratch_shapes=[pltpu.VMEM((d, tt), F32), pltpu.VMEM((et, tt), F32),
                        pltpu.VMEM((et, tt), BF16)],
        compiler_params=_cparams(("arbitrary",)),
        name="peer_main",
    )(h, xn, u, vt, cnt, ar, rank_tiles, bt_tiles)


def peer_layer(h, gain, wq, keys, u, v):
    t = h.shape[0]
    xn, s1t, s2t = peer_prep(h, gain, wq.astype(BF16), keys, tm=min(512, t))
    cnt, ar, rank_tiles, bt_tiles = peer_topk(s1t, s2t, tt=min(256, t))
    return peer_main(h, xn, u.astype(BF16), v.astype(BF16).T, cnt, ar, rank_tiles, bt_tiles,
                     tt=min(512, t), et=2048, lane_blocks_per_iter=1)


def _rope_tables(s):
    half = ROT_DIM // 2
    inv = ROPE_THETA ** (-jnp.arange(half, dtype=F32) * 2.0 / ROT_DIM)
    ang = jnp.arange(s).astype(F32)[:, None] * inv[None, :]
    cos, sin = jnp.cos(ang), jnp.sin(ang)
    pad = HEAD - ROT_DIM
    ones = jnp.ones((s, pad), F32)
    zeros = jnp.zeros((s, pad), F32)
    zh = jnp.zeros((s, half), F32)
    ct = jnp.concatenate([cos, cos, ones], axis=1)
    at = jnp.concatenate([-sin, zh, zeros], axis=1)
    bt = jnp.concatenate([zh, sin, zeros], axis=1)
    return ct, at, bt


def _attn_proj_kernel(h_ref, kvg_ref, ag_ref, wkv_ref, wq_ref, kn_ref, qn_ref,
                      ct_ref, at_ref, bt_ref, q_ref, k_ref, v_ref):
    y = _rms(h_ref[...])
    kv = jnp.dot((y * kvg_ref[...]).astype(BF16), wkv_ref[...], preferred_element_type=F32)
    qq = jnp.dot((y * ag_ref[...]).astype(BF16), wq_ref[...], preferred_element_type=F32)
    ct, at, bt = ct_ref[...], at_ref[...], bt_ref[...]
    half = ROT_DIM // 2

    def head_norm_rope(x, gain):
        n = _rms(x) * gain
        return n * ct + pltpu.roll(n, HEAD - half, 1) * at + pltpu.roll(n, half, 1) * bt

    kw = k_ref.shape[1]
    for hd in range(kw // HEAD):
        cs = slice(hd * HEAD, (hd + 1) * HEAD)
        k_ref[:, cs] = head_norm_rope(kv[:, cs], kn_ref[...]).astype(k_ref.dtype)
    v_ref[...] = kv[:, kw:].astype(v_ref.dtype)
    for hd in range(q_ref.shape[1] // HEAD):
        cs = slice(hd * HEAD, (hd + 1) * HEAD)
        q_ref[:, cs] = head_norm_rope(qq[:, cs], qn_ref[...]).astype(q_ref.dtype)


def attn_proj(h, kv_gain, a_gain, wkv, wq, k_norm, q_norm, seq, *, tm):
    t, d = h.shape
    kw = wkv.shape[1] // 2
    qw = wq.shape[1]
    ct, at, bt = _rope_tables(seq)
    ns = seq // tm
    tab = pl.BlockSpec((tm, HEAD), lambda i: (i % ns, 0))
    vec = lambda n: pl.BlockSpec((1, n), lambda i: (0, 0))
    return pl.pallas_call(
        _attn_proj_kernel,
        grid=(t // tm,),
        in_specs=[pl.BlockSpec((tm, d), lambda i: (i, 0)), vec(d), vec(d),
                  pl.BlockSpec(wkv.shape, lambda i: (0, 0)),
                  pl.BlockSpec(wq.shape, lambda i: (0, 0)),
                  vec(HEAD), vec(HEAD), tab, tab, tab],
        out_specs=[pl.BlockSpec((tm, qw), lambda i: (i, 0)),
                   pl.BlockSpec((tm, kw), lambda i: (i, 0)),
                   pl.BlockSpec((tm, kw), lambda i: (i, 0))],
        out_shape=[jax.ShapeDtypeStruct((t, qw), BF16),
                   jax.ShapeDtypeStruct((t, kw), BF16),
                   jax.ShapeDtypeStruct((t, kw), BF16)],
        compiler_params=_cparams(("parallel",)),
        name="attn_proj",
    )(h, kv_gain.reshape(1, d), a_gain.reshape(1, d), wkv, wq,
      k_norm.reshape(1, HEAD), q_norm.reshape(1, HEAD), ct, at, bt)


ATTN_EXP2_SCALE = HEAD ** -0.5 * math.log2(math.e)
MAX_WINDOW = max(w for w, _ in DIL_GROUPS)


def _attn_window_bias(w, dil):
    rows = Q_PER_KV * Q_BLOCK
    tq = w + (np.arange(rows)[:, None] % Q_BLOCK)
    diff = tq - np.arange(w + Q_BLOCK)[None, :]
    valid = (diff >= 0) & (diff <= w) & (diff % dil == 0)
    return np.where(valid, 0.0, -np.inf).astype(np.float32)


def _attn_kernel(q0_ref, q1_ref, q2_ref, k0_ref, k1_ref, k2_ref, v0_ref, v1_ref, v2_ref,
                 b0_ref, b1_ref, b2_ref, o_ref):
    qb = pl.program_id(2)
    t0 = qb * Q_BLOCK
    rows = Q_PER_KV * Q_BLOCK
    groups = list(zip(DIL_GROUPS, (q0_ref, q1_ref, q2_ref), (k0_ref, k1_ref, k2_ref),
                      (v0_ref, v1_ref, v2_ref), (b0_ref, b1_ref, b2_ref)))

    def raw_scores(q_ref, k_ref, ks, span):
        q2 = jnp.concatenate([q_ref[0, :, r * HEAD:(r + 1) * HEAD] for r in range(Q_PER_KV)],
                             axis=0)
        return lax.dot_general(q2, k_ref[0, pl.ds(ks, span), :], NT_DIMS,
                               preferred_element_type=F32)

    def finish(scores, starts):
        m = functools.reduce(jnp.maximum, [jnp.max(s, axis=-1, keepdims=True) for s in scores])
        den = jnp.zeros((rows, 1), F32)
        out = jnp.zeros((rows, HEAD), F32)
        for s, ks, ((w, _), _, _, v_ref, _) in zip(scores, starts, groups):
            p = jnp.exp2((s - m) * ATTN_EXP2_SCALE)
            den = den + jnp.sum(p, axis=-1, keepdims=True)
            out = out + jnp.dot(p.astype(BF16), v_ref[0, pl.ds(ks, w + Q_BLOCK), :],
                                preferred_element_type=F32)
        out = out / den
        for r in range(Q_PER_KV):
            o_ref[0, :, r * HEAD:(r + 1) * HEAD] = (
                out[r * Q_BLOCK:(r + 1) * Q_BLOCK].astype(o_ref.dtype))

    @pl.when(t0 >= MAX_WINDOW)
    def _():
        scores, starts = [], []
        for (w, dil), q_ref, k_ref, _, b_ref in groups:
            ks = pl.multiple_of(t0 - w, Q_BLOCK)
            scores.append(raw_scores(q_ref, k_ref, ks, w + Q_BLOCK) + b_ref[...])
            starts.append(ks)
        finish(scores, starts)

    @pl.when(t0 < MAX_WINDOW)
    def _():
        scores, starts = [], []
        for (w, dil), q_ref, k_ref, _, _ in groups:
            span = w + Q_BLOCK
            ks = pl.multiple_of(jnp.maximum(t0 - w, 0), Q_BLOCK)
            tq = t0 + (lax.broadcasted_iota(jnp.int32, (rows, span), 0) & (Q_BLOCK - 1))
            diff = tq - (ks + lax.broadcasted_iota(jnp.int32, (rows, span), 1))
            valid = (diff >= 0) & (diff <= w) & ((diff & (dil - 1)) == 0)
            scores.append(jnp.where(valid, raw_scores(q_ref, k_ref, ks, span), -jnp.inf))
            starts.append(ks)
        finish(scores, starts)


def dilated_attention(q, k, v):
    b, s, _ = q.shape
    qcols = Q_PER_KV * HEAD

    def qspec(g):
        return pl.BlockSpec((1, Q_BLOCK, qcols), lambda bi, kv, qb, g=g: (bi, qb, g * KV_PER_GROUP + kv))

    def kvspec(g):
        return pl.BlockSpec((1, s, HEAD), lambda bi, kv, qb, g=g: (bi, 0, g * KV_PER_GROUP + kv))

    ng = len(DIL_GROUPS)
    biases = [jnp.asarray(_attn_window_bias(w, dil)) for w, dil in DIL_GROUPS]
    return pl.pallas_call(
        _attn_kernel,
        grid=(b, KV_PER_GROUP, s // Q_BLOCK),
        in_specs=([qspec(g) for g in range(ng)] + [kvspec(g) for g in range(ng)] * 2
                  + [pl.BlockSpec(bias.shape, lambda bi, kv, qb: (0, 0)) for bias in biases]),
        out_specs=pl.BlockSpec((1, Q_BLOCK, qcols), lambda bi, kv, qb: (bi, qb, kv)),
        out_shape=jax.ShapeDtypeStruct((b, s, KV_PER_GROUP * qcols), BF16),
        compiler_params=_cparams(("parallel", "parallel", "arbitrary")),
        name="dilated_attention",
    )(q, q, q, k, k, k, v, v, v, *biases)


def kernel(x, hgrn_norm, hgrn_w_in, hgrn_lb_logits, hgrn_out_norm, hgrn_w_out, kv_norm, w_kv,
           k_norm, attn_norm, w_q, q_norm, w_o, ffn_norm, peer_w_q, peer_sub_keys, peer_u, peer_v):
    b, s, d = x.shape
    t = b * s
    depth = ffn_norm.shape[0]
    n_a = hgrn_norm.shape[0]
    h = x.reshape(t, d)
    tm = min(1024, t)
    for layer in range(depth):
        if layer < n_a:
            y = norm_matmul(h, hgrn_norm[layer], hgrn_w_in[layer].astype(BF16), tm=tm, tn=512)
            o = hgrn_scan(y.reshape(b, s, 4 * d), hgrn_lb_logits, hgrn_out_norm[layer],
                          layer=layer, seq_block=min(1024, s))
            h = matmul_residual(o.reshape(t, d), hgrn_w_out[layer].astype(BF16), h, tm=tm)
        else:
            j = layer - n_a
            qr, kr, vr = attn_proj(h, kv_norm, attn_norm[j], w_kv.astype(BF16),
                                   w_q[j].astype(BF16), k_norm, q_norm[j], s, tm=min(512, s))
            if j == 0:
                k_sh, v_sh = kr.reshape(b, s, -1), vr.reshape(b, s, -1)
            ao = dilated_attention(qr.reshape(b, s, -1), k_sh, v_sh)
            h = matmul_residual(ao.reshape(t, -1), w_o[j].astype(BF16), h, tm=tm)
        h = peer_layer(h, ffn_norm[layer], peer_w_q[layer], peer_sub_keys[layer],
                       peer_u[layer], peer_v[layer])
    return h.reshape(b, s, d)
```

```python
import functools
import math

import numpy as np
import jax
import jax.numpy as jnp
from jax import lax
from jax.experimental import pallas as pl
from jax.experimental.pallas import tpu as pltpu

F32 = jnp.float32
BF16 = jnp.bfloat16
EPS = 1e-6

LANES = 128
HEAD = 128
HG_CHUNK = 128
HG_SAFE_EXPONENT = 80.0
DIL_GROUPS = ((128, 1), (512, 4), (2048, 16))
KV_PER_GROUP = 2
Q_PER_KV = 2
ROT_DIM = HEAD // 4
ROPE_THETA = 500000.0
Q_BLOCK = 128
PEER_HEADS = 8
PEER_TOPK = 16
N_KEYS = 128
VMEM_LIMIT = 56 * 1024 * 1024

NT_DIMS = (((1,), (1,)), ((), ()))


def _cparams(sem):
    return pltpu.CompilerParams(dimension_semantics=sem, vmem_limit_bytes=VMEM_LIMIT)


def _rms(x):
    return x * lax.rsqrt(jnp.mean(x * x, axis=-1, keepdims=True) + EPS)


def _sigmoid(x):
    return 1.0 / (1.0 + jnp.exp(-x))


def _norm_matmul_kernel(x_ref, g_ref, w_ref, o_ref, xn_ref):
    @pl.when(pl.program_id(1) == 0)
    def _():
        xn_ref[...] = (_rms(x_ref[...]) * g_ref[...]).astype(xn_ref.dtype)

    o_ref[...] = jnp.dot(xn_ref[...], w_ref[...],
                         preferred_element_type=F32).astype(o_ref.dtype)


def norm_matmul(x, gain, w, *, tm, tn, out_dtype=F32):
    t, d = x.shape
    n = w.shape[1]
    return pl.pallas_call(
        _norm_matmul_kernel,
        grid=(t // tm, n // tn),
        in_specs=[pl.BlockSpec((tm, d), lambda i, j: (i, 0)),
                  pl.BlockSpec((1, d), lambda i, j: (0, 0)),
                  pl.BlockSpec((d, tn), lambda i, j: (0, j))],
        out_specs=pl.BlockSpec((tm, tn), lambda i, j: (i, j)),
        out_shape=jax.ShapeDtypeStruct((t, n), out_dtype),
        scratch_shapes=[pltpu.VMEM((tm, d), BF16)],
        compiler_params=_cparams(("parallel", "arbitrary")),
        name="norm_matmul",
    )(x, gain.reshape(1, d), w)


def _matmul_res_kernel(a_ref, w_ref, r_ref, o_ref):
    o_ref[...] = r_ref[...] + jnp.dot(a_ref[...], w_ref[...], preferred_element_type=F32)


def matmul_residual(a, w, res, *, tm):
    t, k = a.shape
    n = w.shape[1]
    return pl.pallas_call(
        _matmul_res_kernel,
        grid=(t // tm,),
        in_specs=[pl.BlockSpec((tm, k), lambda i: (i, 0)),
                  pl.BlockSpec((k, n), lambda i: (0, 0)),
                  pl.BlockSpec((tm, n), lambda i: (i, 0))],
        out_specs=pl.BlockSpec((tm, n), lambda i: (i, 0)),
        out_shape=jax.ShapeDtypeStruct((t, n), F32),
        compiler_params=_cparams(("parallel",)),
        name="matmul_residual",
    )(a, w, res)


def _hgrn_levels(c):
    return int(math.log2(c))


def _hgrn_sum_matrix(c):
    nlev = _hgrn_levels(c)
    m = np.zeros(((nlev + 2) * c, c), np.float32)
    for l in range(nlev):
        h = c >> (l + 1)
        for r in range(c):
            mid = (r // (2 * h)) * 2 * h + h
            if r >= mid:
                m[l * c + r, mid:r + 1] = 1.0
            else:
                m[l * c + r, r + 1:mid] = 1.0
    for r in range(c):
        m[nlev * c + r, :r + 1] = 1.0
        m[(nlev + 1) * c + r, r + 1:] = 1.0
    return m


def _hgrn_kernel(q_ref, f_ref, i_ref, g_ref, lbl_ref, og_ref, m_ref, o_ref, st_ref, *,
                 layer, chunk, n_chunks):
    c = chunk
    nlev = _hgrn_levels(c)

    @pl.when(pl.program_id(2) == 0)
    def _():
        st_ref[...] = jnp.zeros_like(st_ref)

    lg = lbl_ref[...]
    e = jnp.exp(lg - jnp.max(lg, axis=0, keepdims=True))
    lb = jnp.sum(e[:layer + 1], axis=0, keepdims=True) / jnp.sum(e, axis=0, keepdims=True)
    og = og_ref[...]
    row = lax.broadcasted_iota(jnp.int32, (c, c), 0)
    col = lax.broadcasted_iota(jnp.int32, (c, c), 1)

    def body(ci, carry, *, mild):
        sl = pl.ds(pl.multiple_of(ci * c, c), c)
        q = q_ref[0, sl, :]
        fr = f_ref[0, sl, :]
        v = i_ref[0, sl, :]
        gt = g_ref[0, sl, :]
        qa = q * _sigmoid(q)
        f = lb + (1.0 - lb) * _sigmoid(fr)
        k = 1.0 - f
        logf = jnp.log(f)
        hi = logf.astype(BF16)
        lo = (logf - hi.astype(F32)).astype(BF16)
        g2 = jnp.concatenate([hi, lo], axis=1)
        if mild:
            e2 = jnp.dot(m_ref[nlev * c:(nlev + 1) * c, :], g2, preferred_element_type=F32)
            b_incl = e2[:, :HEAD] + e2[:, HEAD:]
            dmid = b_incl - b_incl[c // 2 - 1:c // 2, :]
            p = lax.dot_general((qa * jnp.exp(dmid)).astype(BF16),
                                (k * jnp.exp(-dmid)).astype(BF16), NT_DIMS,
                                preferred_element_type=F32)
            s = jnp.where(row >= col, p, 0.0)
            rev = b_incl[c - 1:c, :] - b_incl
        else:
            e2 = jnp.dot(m_ref[...], g2, preferred_element_type=F32)
            ex = e2[:, :HEAD] + e2[:, HEAD:]
            s = jnp.where(row == col,
                          lax.dot_general(qa.astype(BF16), k.astype(BF16), NT_DIMS,
                                          preferred_element_type=F32),
                          0.0)
            for l in range(nlev):
                sh = nlev - 1 - l
                x = jnp.exp(ex[l * c:(l + 1) * c])
                p = lax.dot_general((qa * x).astype(BF16), (k * x).astype(BF16), NT_DIMS,
                                    preferred_element_type=F32)
                mask = (((row >> (sh + 1)) == (col >> (sh + 1)))
                        & (((row >> sh) & 1) == 1) & (((col >> sh) & 1) == 0))
                s = jnp.where(mask, p, s)
            b_incl = ex[nlev * c:(nlev + 1) * c]
            rev = ex[(nlev + 1) * c:(nlev + 2) * c]
        vb = v.astype(BF16)
        intra = jnp.dot(s.astype(BF16), vb, preferred_element_type=F32)
        st = st_ref[...]
        inter = lax.dot_general((qa * jnp.exp(b_incl)).astype(BF16), st.astype(BF16), NT_DIMS,
                                preferred_element_type=F32)
        o = inter + intra
        y = _rms(o) * og * (gt * _sigmoid(gt))
        o_ref[0, sl, :] = y.astype(o_ref.dtype)

        k2 = (k * jnp.exp(rev)).astype(BF16)
        upd = jnp.dot(v.T.astype(BF16), k2, preferred_element_type=F32)
        st_ref[...] = st * jnp.exp(b_incl[c - 1:c, :]) + upd
        return carry

    mild = jnp.min(lb) >= math.exp(-HG_SAFE_EXPONENT / (c // 2))

    @pl.when(mild)
    def _():
        lax.fori_loop(0, n_chunks, functools.partial(body, mild=True), 0, unroll=4)

    @pl.when(jnp.logical_not(mild))
    def _():
        lax.fori_loop(0, n_chunks, functools.partial(body, mild=False), 0, unroll=4)


def hgrn_scan(y, lb_logits, out_gain, *, layer, seq_block):
    b, s, d4 = y.shape
    d = d4 // 4
    nh = d // HEAD
    c = min(HG_CHUNK, seq_block)
    msel = jnp.asarray(_hgrn_sum_matrix(c), BF16)
    nl = lb_logits.shape[0]
    kern = functools.partial(_hgrn_kernel, layer=layer, chunk=c, n_chunks=seq_block // c)

    def part(p):
        return pl.BlockSpec((1, seq_block, HEAD), lambda bi, hi, si, p=p: (bi, si, p * nh + hi))

    return pl.pallas_call(
        kern,
        grid=(b, nh, s // seq_block),
        in_specs=[part(0), part(1), part(2), part(3),
                  pl.BlockSpec((nl, HEAD), lambda bi, hi, si: (0, hi)),
                  pl.BlockSpec((1, HEAD), lambda bi, hi, si: (0, 0)),
                  pl.BlockSpec(msel.shape, lambda bi, hi, si: (0, 0))],
        out_specs=pl.BlockSpec((1, seq_block, HEAD), lambda bi, hi, si: (bi, si, hi)),
        out_shape=jax.ShapeDtypeStruct((b, s, d), BF16),
        scratch_shapes=[pltpu.VMEM((HEAD, HEAD), F32)],
        compiler_params=_cparams(("parallel", "parallel", "arbitrary")),
        name="hgrn_scan",
    )(y, y, y, y, lb_logits, out_gain.reshape(1, HEAD), msel)


def _peer_prep_kernel(h_ref, g_ref, wq_ref, keys_ref, xn_ref, s1_ref, s2_ref):
    xn = (_rms(h_ref[...]) * g_ref[...]).astype(BF16)
    xn_ref[...] = xn
    q = jnp.dot(xn, wq_ref[...], preferred_element_type=F32)
    k0 = keys_ref[0].astype(BF16)
    k1 = keys_ref[1].astype(BF16)
    for hh in range(PEER_HEADS):
        base = hh * 2 * HEAD
        q1 = q[:, base:base + HEAD].astype(BF16)
        q2 = q[:, base + HEAD:base + 2 * HEAD].astype(BF16)
        s1_ref[hh] = lax.dot_general(k0, q1, NT_DIMS, preferred_element_type=F32)
        s2_ref[hh] = lax.dot_general(k1, q2, NT_DIMS, preferred_element_type=F32)


def peer_prep(h, gain, wq, keys, *, tm):
    t, d = h.shape
    nq = wq.shape[1]
    return pl.pallas_call(
        _peer_prep_kernel,
        grid=(t // tm,),
        in_specs=[pl.BlockSpec((tm, d), lambda i: (i, 0)),
                  pl.BlockSpec((1, d), lambda i: (0, 0)),
                  pl.BlockSpec((d, nq), lambda i: (0, 0)),
                  pl.BlockSpec(keys.shape, lambda i: (0, 0, 0))],
        out_specs=[pl.BlockSpec((tm, d), lambda i: (i, 0)),
                   pl.BlockSpec((PEER_HEADS, N_KEYS, tm), lambda i: (0, 0, i)),
                   pl.BlockSpec((PEER_HEADS, N_KEYS, tm), lambda i: (0, 0, i))],
        out_shape=[jax.ShapeDtypeStruct((t, d), BF16),
                   jax.ShapeDtypeStruct((PEER_HEADS, N_KEYS, t), F32),
                   jax.ShapeDtypeStruct((PEER_HEADS, N_KEYS, t), F32)],
        compiler_params=_cparams(("parallel",)),
        name="peer_prep",
    )(h, gain.reshape(1, d), wq, keys)


def _oddeven_merge(lo, hi, r):
    step = r * 2
    if step < hi - lo:
        yield from _oddeven_merge(lo, hi, step)
        yield from _oddeven_merge(lo + r, hi, step)
        yield from [(i, i + r) for i in range(lo + r, hi - r, step)]
    else:
        yield (lo, lo + r)


def _oddeven_sort(lo, hi):
    if hi - lo >= 1:
        mid = lo + (hi - lo) // 2
        yield from _oddeven_sort(lo, mid)
        yield from _oddeven_sort(mid + 1, hi)
        yield from _oddeven_merge(lo, hi, 1)


SORT16 = tuple(_oddeven_sort(0, PEER_TOPK - 1))
BITONIC16 = tuple((i, i + h) for h in (8, 4, 2, 1) for i in range(PEER_TOPK) if i % (2 * h) < h)
SUBLANES = 8


def _compare_exchange(v, pairs):
    v = list(v)
    for i, j in pairs:
        v[i], v[j] = jnp.maximum(v[i], v[j]), jnp.minimum(v[i], v[j])
    return v


def _merge_sublane_lists(v):
    n = len(v)
    for shift in (4, 2, 1):
        w = [pltpu.roll(x, shift, 0) for x in v]
        v = _compare_exchange([jnp.maximum(v[k], w[n - 1 - k]) for k in range(n)], BITONIC16)
    return v


def _top16(x):
    slabs = [x[SUBLANES * k:SUBLANES * (k + 1), :] for k in range(PEER_TOPK)]
    return _merge_sublane_lists(_compare_exchange(slabs, SORT16))


def _peer_topk_kernel(s1_ref, s2_ref, cnt_ref, ar_ref, rk_ref, bt_ref):
    k = PEER_TOPK
    lanes = s1_ref.shape[2]
    sub = lax.broadcasted_iota(jnp.int32, (SUBLANES, lanes), 0)

    def head(hh, carry):
        s1 = s1_ref[hh]
        s2 = s2_ref[hh]
        va = _top16(s1)
        vb = _top16(s2)
        bd = vb[0]
        for p in range(1, SUBLANES):
            bd = jnp.where(sub == p, vb[p], bd)
        cand = []
        for i in range(k):
            c = va[i] + bd
            lim = k // (i + 1)
            cand.append(c if lim >= SUBLANES else jnp.where(sub < lim, c, -jnp.inf))
        tail = [va[0] + vb[SUBLANES + q] for q in range(k - SUBLANES)]
        top16 = _merge_sublane_lists(cand)
        tau = top16[k - SUBLANES - 1]
        for q in range(k - SUBLANES):
            tau = jnp.minimum(tau, jnp.maximum(top16[k - 1 - q], tail[q]))
        top = va[0] + vb[0]
        zc = jnp.zeros((SUBLANES, lanes), F32)
        for c in cand:
            zc = zc + jnp.where(c >= tau, jnp.exp(c - top), 0.0)
        zt = jnp.zeros((SUBLANES, lanes), F32)
        for c in tail:
            zt = zt + jnp.where(c >= tau, jnp.exp(c - top), 0.0)
        z = jnp.sum(zc, axis=0, keepdims=True) + zt[0:1, :]
        tau_row = tau[0:1, :]
        b_rows = [x[0:1, :] for x in vb]
        cnt = jnp.zeros(s1.shape, F32)
        rank2 = jnp.zeros(s2.shape, F32)
        for j in range(k):
            cnt = cnt + jnp.where(s1 + b_rows[j] >= tau_row, 1.0, 0.0)
            rank2 = rank2 + jnp.where(b_rows[j] > s2, 1.0, 0.0)
        cnt_ref[hh] = cnt
        ar_ref[hh] = jnp.exp(s1 - va[0][0:1, :]) / z
        bt = jnp.exp(s2 - b_rows[0])
        for lb in range(lanes // LANES):
            ln = slice(lb * LANES, (lb + 1) * LANES)
            rk_ref[hh, lb] = pltpu.bitcast(rank2[:, ln].astype(BF16), jnp.uint32)
            bt_ref[hh, lb] = pltpu.bitcast(bt[:, ln].astype(BF16), jnp.uint32)
        return carry

    lax.fori_loop(0, PEER_HEADS, head, 0)


def peer_topk(s1t, s2t, *, tt):
    nh, nk, t = s1t.shape
    spec = pl.BlockSpec((nh, nk, tt), lambda i: (0, 0, i))
    shp = jax.ShapeDtypeStruct((nh, nk, t), F32)
    tspec = pl.BlockSpec((nh, tt // LANES, nk // 2, LANES), lambda i: (0, i, 0, 0))
    tshp = jax.ShapeDtypeStruct((nh, t // LANES, nk // 2, LANES), jnp.uint32)
    return pl.pallas_call(
        _peer_topk_kernel,
        grid=(t // tt,),
        in_specs=[spec, spec],
        out_specs=[spec, spec, tspec, tspec],
        out_shape=[shp, shp, tshp, tshp],
        compiler_params=_cparams(("parallel",)),
        name="peer_topk",
    )(s1t, s2t)


GELU_C0 = math.sqrt(2.0 / math.pi)
GELU_C1 = GELU_C0 * 0.044715


def _gelu(z):
    return 0.5 * z * (1.0 + jnp.tanh(z * (GELU_C0 + GELU_C1 * (z * z))))


def _peer_main_kernel(h_ref, xn_ref, u_ref, vt_ref, cnt_ref, ar_ref, rk_ref, bt_ref, o_ref,
                      acc_ref, zt_ref, a_ref, *, n_blocks, blocks_per_tile, lane_blocks_per_iter):
    g_step = pl.program_id(0)
    et, tt = zt_ref.shape
    n_e1 = et // N_KEYS
    jb = jnp.maximum(g_step - 1, 0) % blocks_per_tile

    @pl.when(g_step == 0)
    def _():
        a_ref[...] = jnp.zeros_like(a_ref)

    @pl.when(jb == 0)
    def _():
        acc_ref[...] = jnp.zeros_like(acc_ref)

    acc_ref[...] += jnp.dot(vt_ref[...], a_ref[...], preferred_element_type=F32)
    zt_ref[...] = lax.dot_general(u_ref[...], xn_ref[...], NT_DIMS, preferred_element_type=F32)

    @pl.when((g_step >= 1) & (jb == blocks_per_tile - 1))
    def _():
        o_ref[...] = h_ref[...] + acc_ref[...].T

    def lane_group(li, carry):
        for sub in range(lane_blocks_per_iter):
            lb = li * lane_blocks_per_iter + sub
            ln = pl.ds(pl.multiple_of(lb * LANES, LANES), LANES)
            for c in range(n_e1):
                rows = slice(c * N_KEYS, (c + 1) * N_KEYS)
                g = jnp.zeros((N_KEYS, LANES), BF16)
                for hh in range(PEER_HEADS):
                    cnt = cnt_ref[hh, c:c + 1, ln].astype(BF16)
                    ar = ar_ref[hh, c:c + 1, ln].astype(BF16)
                    rk = pltpu.bitcast(rk_ref[hh, lb], BF16)
                    bt = pltpu.bitcast(bt_ref[hh, lb], BF16)
                    g = g + jnp.where(rk < cnt, bt * ar, 0)
                a_ref[rows, ln] = _gelu(zt_ref[rows, ln].astype(BF16)) * g
        return carry

    lax.fori_loop(0, tt // (LANES * lane_blocks_per_iter), lane_group, 0)


def peer_main(h, xn, u, vt, cnt, ar, rank_tiles, bt_tiles, *, tt, et, lane_blocks_per_iter):
    t, d = h.shape
    ne = u.shape[0]
    n_e1 = et // N_KEYS
    bpt = ne // et
    n_blocks = (t // tt) * bpt
    kern = functools.partial(_peer_main_kernel, n_blocks=n_blocks, blocks_per_tile=bpt,
                             lane_blocks_per_iter=lane_blocks_per_iter)

    def cur(g):
        return jnp.minimum(g, n_blocks - 1)

    def prev(g):
        return jnp.maximum(g - 1, 0)

    row_spec = pl.BlockSpec((PEER_HEADS, n_e1, tt), lambda g: (0, cur(g) % bpt, cur(g) // bpt))
    tok_spec = pl.BlockSpec((PEER_HEADS, tt // LANES, N_KEYS // 2, LANES),
                            lambda g: (0, cur(g) // bpt, 0, 0))
    return pl.pallas_call(
        kern,
        grid=(n_blocks + 1,),
        in_specs=[pl.BlockSpec((tt, d), lambda g: (prev(g) // bpt, 0)),
                  pl.BlockSpec((tt, d), lambda g: (cur(g) // bpt, 0)),
                  pl.BlockSpec((et, d), lambda g: (cur(g) % bpt, 0)),
                  pl.BlockSpec((d, et), lambda g: (0, prev(g) % bpt)),
                  row_spec, row_spec, tok_spec, tok_spec],
        out_specs=pl.BlockSpec((tt, d), lambda g: (prev(g) // bpt, 0)),
        out_shape=jax.ShapeDtypeStruct((t, d), F32),
        scratch_shapes=[pltpu.VMEM((d, tt), F32), pltpu.VMEM((et, tt), F32),
                        pltpu.VMEM((et, tt), BF16)],
        compiler_params=_cparams(("arbitrary",)),
        name="peer_main",
    )(h, xn, u, vt, cnt, ar, rank_tiles, bt_tiles)


def peer_layer(h, gain, wq, keys, u, v):
    t = h.shape[0]
    xn, s1t, s2t = peer_prep(h, gain, wq.astype(BF16), keys, tm=min(512, t))
    cnt, ar, rank_tiles, bt_tiles = peer_topk(s1t, s2t, tt=min(256, t))
    return peer_main(h, xn, u.astype(BF16), v.astype(BF16).T, cnt, ar, rank_tiles, bt_tiles,
                     tt=min(512, t), et=2048, lane_blocks_per_iter=1)


def _rope_tables(s):
    half = ROT_DIM // 2
    inv = ROPE_THETA ** (-jnp.arange(half, dtype=F32) * 2.0 / ROT_DIM)
    ang = jnp.arange(s).astype(F32)[:, None] * inv[None, :]
    cos, sin = jnp.cos(ang), jnp.sin(ang)
    pad = HEAD - ROT_DIM
    ones = jnp.ones((s, pad), F32)
    zeros = jnp.zeros((s, pad), F32)
    zh = jnp.zeros((s, half), F32)
    ct = jnp.concatenate([cos, cos, ones], axis=1)
    at = jnp.concatenate([-sin, zh, zeros], axis=1)
    bt = jnp.concatenate([zh, sin, zeros], axis=1)
    return ct, at, bt


def _attn_proj_kernel(h_ref, kvg_ref, ag_ref, wkv_ref, wq_ref, kn_ref, qn_ref,
                      ct_ref, at_ref, bt_ref, q_ref, k_ref, v_ref):
    y = _rms(h_ref[...])
    kv = jnp.dot((y * kvg_ref[...]).astype(BF16), wkv_ref[...], preferred_element_type=F32)
    qq = jnp.dot((y * ag_ref[...]).astype(BF16), wq_ref[...], preferred_element_type=F32)
    ct, at, bt = ct_ref[...], at_ref[...], bt_ref[...]
    half = ROT_DIM // 2

    def head_norm_rope(x, gain):
        n = _rms(x) * gain
        return n * ct + pltpu.roll(n, HEAD - half, 1) * at + pltpu.roll(n, half, 1) * bt

    kw = k_ref.shape[1]
    for hd in range(kw // HEAD):
        cs = slice(hd * HEAD, (hd + 1) * HEAD)
        k_ref[:, cs] = head_norm_rope(kv[:, cs], kn_ref[...]).astype(k_ref.dtype)
    v_ref[...] = kv[:, kw:].astype(v_ref.dtype)
    for hd in range(q_ref.shape[1] // HEAD):
        cs = slice(hd * HEAD, (hd + 1) * HEAD)
        q_ref[:, cs] = head_norm_rope(qq[:, cs], qn_ref[...]).astype(q_ref.dtype)


def attn_proj(h, kv_gain, a_gain, wkv, wq, k_norm, q_norm, seq, *, tm):
    t, d = h.shape
    kw = wkv.shape[1] // 2
    qw = wq.shape[1]
    ct, at, bt = _rope_tables(seq)
    ns = seq // tm
    tab = pl.BlockSpec((tm, HEAD), lambda i: (i % ns, 0))
    vec = lambda n: pl.BlockSpec((1, n), lambda i: (0, 0))
    return pl.pallas_call(
        _attn_proj_kernel,
        grid=(t // tm,),
        in_specs=[pl.BlockSpec((tm, d), lambda i: (i, 0)), vec(d), vec(d),
                  pl.BlockSpec(wkv.shape, lambda i: (0, 0)),
                  pl.BlockSpec(wq.shape, lambda i: (0, 0)),
                  vec(HEAD), vec(HEAD), tab, tab, tab],
        out_specs=[pl.BlockSpec((tm, qw), lambda i: (i, 0)),
                   pl.BlockSpec((tm, kw), lambda i: (i, 0)),
                   pl.BlockSpec((tm, kw), lambda i: (i, 0))],
        out_shape=[jax.ShapeDtypeStruct((t, qw), BF16),
                   jax.ShapeDtypeStruct((t, kw), BF16),
                   jax.ShapeDtypeStruct((t, kw), BF16)],
        compiler_params=_cparams(("parallel",)),
        name="attn_proj",
    )(h, kv_gain.reshape(1, d), a_gain.reshape(1, d), wkv, wq,
      k_norm.reshape(1, HEAD), q_norm.reshape(1, HEAD), ct, at, bt)


ATTN_EXP2_SCALE = HEAD ** -0.5 * math.log2(math.e)
MAX_WINDOW = max(w for w, _ in DIL_GROUPS)


def _attn_window_bias(w, dil):
    rows = Q_PER_KV * Q_BLOCK
    tq = w + (np.arange(rows)[:, None] % Q_BLOCK)
    diff = tq - np.arange(w + Q_BLOCK)[None, :]
    valid = (diff >= 0) & (diff <= w) & (diff % dil == 0)
    return np.where(valid, 0.0, -np.inf).astype(np.float32)


def _attn_kernel(q0_ref, q1_ref, q2_ref, k0_ref, k1_ref, k2_ref, v0_ref, v1_ref, v2_ref,
                 b0_ref, b1_ref, b2_ref, o_ref):
    qb = pl.program_id(2)
    t0 = qb * Q_BLOCK
    rows = Q_PER_KV * Q_BLOCK
    groups = list(zip(DIL_GROUPS, (q0_ref, q1_ref, q2_ref), (k0_ref, k1_ref, k2_ref),
                      (v0_ref, v1_ref, v2_ref), (b0_ref, b1_ref, b2_ref)))

    def raw_scores(q_ref, k_ref, ks, span):
        q2 = jnp.concatenate([q_ref[0, :, r * HEAD:(r + 1) * HEAD] for r in range(Q_PER_KV)],
                             axis=0)
        return lax.dot_general(q2, k_ref[0, pl.ds(ks, span), :], NT_DIMS,
                               preferred_element_type=F32)

    def finish(scores, starts):
        m = functools.reduce(jnp.maximum, [jnp.max(s, axis=-1, keepdims=True) for s in scores])
        den = jnp.zeros((rows, 1), F32)
        out = jnp.zeros((rows, HEAD), F32)
        for s, ks, ((w, _), _, _, v_ref, _) in zip(scores, starts, groups):
            p = jnp.exp2((s - m) * ATTN_EXP2_SCALE)
            den = den + jnp.sum(p, axis=-1, keepdims=True)
            out = out + jnp.dot(p.astype(BF16), v_ref[0, pl.ds(ks, w + Q_BLOCK), :],
                                preferred_element_type=F32)
        out = out / den
        for r in range(Q_PER_KV):
            o_ref[0, :, r * HEAD:(r + 1) * HEAD] = (
                out[r * Q_BLOCK:(r + 1) * Q_BLOCK].astype(o_ref.dtype))

    @pl.when(t0 >= MAX_WINDOW)
    def _():
        scores, starts = [], []
        for (w, dil), q_ref, k_ref, _, b_ref in groups:
            ks = pl.multiple_of(t0 - w, Q_BLOCK)
            scores.append(raw_scores(q_ref, k_ref, ks, w + Q_BLOCK) + b_ref[...])
            starts.append(ks)
        finish(scores, starts)

    @pl.when(t0 < MAX_WINDOW)
    def _():
        scores, starts = [], []
        for (w, dil), q_ref, k_ref, _, _ in groups:
            span = w + Q_BLOCK
            ks = pl.multiple_of(jnp.maximum(t0 - w, 0), Q_BLOCK)
            tq = t0 + (lax.broadcasted_iota(jnp.int32, (rows, span), 0) & (Q_BLOCK - 1))
            diff = tq - (ks + lax.broadcasted_iota(jnp.int32, (rows, span), 1))
            valid = (diff >= 0) & (diff <= w) & ((diff & (dil - 1)) == 0)
            scores.append(jnp.where(valid, raw_scores(q_ref, k_ref, ks, span), -jnp.inf))
            starts.append(ks)
        finish(scores, starts)


def dilated_attention(q, k, v):
    b, s, _ = q.shape
    qcols = Q_PER_KV * HEAD

    def qspec(g):
        return pl.BlockSpec((1, Q_BLOCK, qcols), lambda bi, kv, qb, g=g: (bi, qb, g * KV_PER_GROUP + kv))

    def kvspec(g):
        return pl.BlockSpec((1, s, HEAD), lambda bi, kv, qb, g=g: (bi, 0, g * KV_PER_GROUP + kv))

    ng = len(DIL_GROUPS)
    biases = [jnp.asarray(_attn_window_bias(w, dil)) for w, dil in DIL_GROUPS]
    return pl.pallas_call(
        _attn_kernel,
        grid=(b, KV_PER_GROUP, s // Q_BLOCK),
        in_specs=([qspec(g) for g in range(ng)] + [kvspec(g) for g in range(ng)] * 2
                  + [pl.BlockSpec(bias.shape, lambda bi, kv, qb: (0, 0)) for bias in biases]),
        out_specs=pl.BlockSpec((1, Q_BLOCK, qcols), lambda bi, kv, qb: (bi, qb, kv)),
        out_shape=jax.ShapeDtypeStruct((b, s, KV_PER_GROUP * qcols), BF16),
        compiler_params=_cparams(("parallel", "parallel", "arbitrary")),
        name="dilated_attention",
    )(q, q, q, k, k, k, v, v, v, *biases)


ATTN_SUB = 128


def _band_bias(first):
    rq = np.arange(Q_PER_KV * ATTN_SUB)[:, None] % ATTN_SUB
    ck = np.arange(2 * ATTN_SUB)[None, :]
    diff = rq - ck if first else rq + ATTN_SUB - ck
    return np.where((diff >= 0) & (diff <= ATTN_SUB), 0.0, -np.inf).astype(np.float32)


def _attn_group_kernel(q_ref, k_ref, v_ref, bfirst_ref, bband_ref, o_ref, lse_ref, *, qb):
    nb = pl.program_id(3)
    scale = HEAD ** -0.5
    for sb in range(qb // ATTN_SUB):
        rows = slice(sb * ATTN_SUB, (sb + 1) * ATTN_SUB)
        n0 = nb * qb + sb * ATTN_SUB
        ks = pl.multiple_of(jnp.maximum(n0 - ATTN_SUB, 0), ATTN_SUB)
        q2 = jnp.concatenate([q_ref[0, rows, r * HEAD:(r + 1) * HEAD] for r in range(Q_PER_KV)],
                             axis=0)
        s = lax.dot_general(q2, k_ref[0, pl.ds(ks, 2 * ATTN_SUB), :], NT_DIMS,
                            preferred_element_type=F32)
        s = s + jnp.where(n0 == 0, bfirst_ref[...], bband_ref[...])
        m = jnp.max(s, axis=-1, keepdims=True)
        p = jnp.exp2((s - m) * ATTN_EXP2_SCALE)
        den = jnp.sum(p, axis=-1, keepdims=True)
        out = jnp.dot(p.astype(BF16), v_ref[0, pl.ds(ks, 2 * ATTN_SUB), :],
                      preferred_element_type=F32) / den
        lse = m * scale + jnp.log(den)
        for r in range(Q_PER_KV):
            part = slice(r * ATTN_SUB, (r + 1) * ATTN_SUB)
            o_ref[0, rows, r * HEAD:(r + 1) * HEAD] = out[part].astype(o_ref.dtype)
            lse_ref[0, rows, r * HEAD:(r + 1) * HEAD] = jnp.broadcast_to(lse[part],
                                                                          (ATTN_SUB, HEAD))


def attn_group(q, k, v, g, *, qb):
    b, s, qw = q.shape
    kw = k.shape[2]
    w, dil = DIL_GROUPS[g]
    assert w == ATTN_SUB * dil
    sub = s // dil
    qb = min(qb, sub)
    qcols = Q_PER_KV * HEAD
    ow = KV_PER_GROUP * qcols
    head_pair = lambda r, kv: g * KV_PER_GROUP + kv
    kern = functools.partial(_attn_group_kernel, qb=qb)
    const = lambda bi, r, kv, nb: (0, 0)
    o, lse = pl.pallas_call(
        kern,
        grid=(b, dil, KV_PER_GROUP, sub // qb),
        in_specs=[pl.BlockSpec((1, qb, qcols),
                               lambda bi, r, kv, nb: (bi, nb, r * (qw // qcols) + head_pair(r, kv))),
                  pl.BlockSpec((1, sub, HEAD),
                               lambda bi, r, kv, nb: (bi, 0, r * (kw // HEAD) + head_pair(r, kv))),
                  pl.BlockSpec((1, sub, HEAD),
                               lambda bi, r, kv, nb: (bi, 0, r * (kw // HEAD) + head_pair(r, kv))),
                  pl.BlockSpec((Q_PER_KV * ATTN_SUB, 2 * ATTN_SUB), const),
                  pl.BlockSpec((Q_PER_KV * ATTN_SUB, 2 * ATTN_SUB), const)],
        out_specs=[pl.BlockSpec((1, qb, qcols),
                                lambda bi, r, kv, nb: (bi, nb, r * KV_PER_GROUP + kv)),
                   pl.BlockSpec((1, qb, qcols),
                                lambda bi, r, kv, nb: (bi, nb, r * KV_PER_GROUP + kv))],
        out_shape=[jax.ShapeDtypeStruct((b, sub, dil * ow), BF16),
                   jax.ShapeDtypeStruct((b, sub, dil * ow), F32)],
        compiler_params=_cparams(("parallel", "parallel", "parallel", "arbitrary")),
        name="attn_group",
    )(q.reshape(b, sub, dil * qw), k.reshape(b, sub, dil * kw), v.reshape(b, sub, dil * kw),
      jnp.asarray(_band_bias(True)), jnp.asarray(_band_bias(False)))
    return o.reshape(b * s, ow), lse.reshape(b * s, ow)


def _combine_proj_kernel(o0_ref, o1_ref, o2_ref, l0_ref, l1_ref, l2_ref, w_ref, r_ref, out_ref):
    lses = [l0_ref[...], l1_ref[...], l2_ref[...]]
    m = functools.reduce(jnp.maximum, lses)
    ws = [jnp.exp(x - m) for x in lses]
    num = functools.reduce(lambda a, b: a + b,
                           [w * o[...].astype(F32) for w, o in zip(ws, (o0_ref, o1_ref, o2_ref))])
    comb = (num / functools.reduce(lambda a, b: a + b, ws)).astype(BF16)
    out_ref[...] = r_ref[...] + jnp.dot(comb, w_ref[...], preferred_element_type=F32)


def combine_proj(outs, lses, w, res, *, tm):
    t, k = outs[0].shape
    n = w.shape[1]
    row = lambda width: pl.BlockSpec((tm, width), lambda i: (i, 0))
    return pl.pallas_call(
        _combine_proj_kernel,
        grid=(t // tm,),
        in_specs=[row(k)] * 6 + [pl.BlockSpec((k, n), lambda i: (0, 0)), row(n)],
        out_specs=row(n),
        out_shape=jax.ShapeDtypeStruct((t, n), F32),
        compiler_params=_cparams(("parallel",)),
        name="combine_proj",
    )(*outs, *lses, w, res)


def kernel(x, hgrn_norm, hgrn_w_in, hgrn_lb_logits, hgrn_out_norm, hgrn_w_out, kv_norm, w_kv,
           k_norm, attn_norm, w_q, q_norm, w_o, ffn_norm, peer_w_q, peer_sub_keys, peer_u, peer_v):
    b, s, d = x.shape
    t = b * s
    depth = ffn_norm.shape[0]
    n_a = hgrn_norm.shape[0]
    h = x.reshape(t, d)
    tm = min(1024, t)
    for layer in range(depth):
        if layer < n_a:
            y = norm_matmul(h, hgrn_norm[layer], hgrn_w_in[layer].astype(BF16), tm=tm, tn=512)
            o = hgrn_scan(y.reshape(b, s, 4 * d), hgrn_lb_logits, hgrn_out_norm[layer],
                          layer=layer, seq_block=min(1024, s))
            h = matmul_residual(o.reshape(t, d), hgrn_w_out[layer].astype(BF16), h, tm=tm)
        else:
            j = layer - n_a
            qr, kr, vr = attn_proj(h, kv_norm, attn_norm[j], w_kv.astype(BF16),
                                   w_q[j].astype(BF16), k_norm, q_norm[j], s, tm=min(512, s))
            if j == 0:
                k_sh, v_sh = kr.reshape(b, s, -1), vr.reshape(b, s, -1)
            parts = [attn_group(qr.reshape(b, s, -1), k_sh, v_sh, g, qb=512)
                     for g in range(len(DIL_GROUPS))]
            h = combine_proj([o for o, _ in parts], [l for _, l in parts],
                             w_o[j].astype(BF16), h, tm=min(512, t))
        h = peer_layer(h, ffn_norm[layer], peer_w_q[layer], peer_sub_keys[layer],
                       peer_u[layer], peer_v[layer])
    return h.reshape(b, s, d)
```

```python
import functools
import math

import numpy as np
import jax
import jax.numpy as jnp
from jax import lax
from jax.experimental import pallas as pl
from jax.experimental.pallas import tpu as pltpu

F32 = jnp.float32
BF16 = jnp.bfloat16
EPS = 1e-6

LANES = 128
HEAD = 128
HG_CHUNK = 128
HG_SAFE_EXPONENT = 80.0
DIL_GROUPS = ((128, 1), (512, 4), (2048, 16))
KV_PER_GROUP = 2
Q_PER_KV = 2
ROT_DIM = HEAD // 4
ROPE_THETA = 500000.0
Q_BLOCK = 128
PEER_HEADS = 8
PEER_TOPK = 16
N_KEYS = 128
VMEM_LIMIT = 56 * 1024 * 1024

NT_DIMS = (((1,), (1,)), ((), ()))


def _cparams(sem):
    return pltpu.CompilerParams(dimension_semantics=sem, vmem_limit_bytes=VMEM_LIMIT)


def _rms(x):
    return x * lax.rsqrt(jnp.mean(x * x, axis=-1, keepdims=True) + EPS)


def _sigmoid(x):
    return 1.0 / (1.0 + jnp.exp(-x))


def _norm_matmul_kernel(x_ref, g_ref, w_ref, o_ref, xn_ref):
    @pl.when(pl.program_id(1) == 0)
    def _():
        xn_ref[...] = (_rms(x_ref[...]) * g_ref[...]).astype(xn_ref.dtype)

    o_ref[...] = jnp.dot(xn_ref[...], w_ref[...],
                         preferred_element_type=F32).astype(o_ref.dtype)


def norm_matmul(x, gain, w, *, tm, tn, out_dtype=F32):
    t, d = x.shape
    n = w.shape[1]
    return pl.pallas_call(
        _norm_matmul_kernel,
        grid=(t // tm, n // tn),
        in_specs=[pl.BlockSpec((tm, d), lambda i, j: (i, 0)),
                  pl.BlockSpec((1, d), lambda i, j: (0, 0)),
                  pl.BlockSpec((d, tn), lambda i, j: (0, j))],
        out_specs=pl.BlockSpec((tm, tn), lambda i, j: (i, j)),
        out_shape=jax.ShapeDtypeStruct((t, n), out_dtype),
        scratch_shapes=[pltpu.VMEM((tm, d), BF16)],
        compiler_params=_cparams(("parallel", "arbitrary")),
        name="norm_matmul",
    )(x, gain.reshape(1, d), w)


def _matmul_res_kernel(a_ref, w_ref, r_ref, o_ref):
    o_ref[...] = r_ref[...] + jnp.dot(a_ref[...], w_ref[...], preferred_element_type=F32)


def matmul_residual(a, w, res, *, tm):
    t, k = a.shape
    n = w.shape[1]
    return pl.pallas_call(
        _matmul_res_kernel,
        grid=(t // tm,),
        in_specs=[pl.BlockSpec((tm, k), lambda i: (i, 0)),
                  pl.BlockSpec((k, n), lambda i: (0, 0)),
                  pl.BlockSpec((tm, n), lambda i: (i, 0))],
        out_specs=pl.BlockSpec((tm, n), lambda i: (i, 0)),
        out_shape=jax.ShapeDtypeStruct((t, n), F32),
        compiler_params=_cparams(("parallel",)),
        name="matmul_residual",
    )(a, w, res)


def _hgrn_levels(c):
    return int(math.log2(c))


def _hgrn_sum_matrix(c):
    nlev = _hgrn_levels(c)
    m = np.zeros(((nlev + 2) * c, c), np.float32)
    for l in range(nlev):
        h = c >> (l + 1)
        for r in range(c):
            mid = (r // (2 * h)) * 2 * h + h
            if r >= mid:
                m[l * c + r, mid:r + 1] = 1.0
            else:
                m[l * c + r, r + 1:mid] = 1.0
    for r in range(c):
        m[nlev * c + r, :r + 1] = 1.0
        m[(nlev + 1) * c + r, r + 1:] = 1.0
    return m


def _hgrn_kernel(q_ref, f_ref, i_ref, g_ref, lbl_ref, og_ref, m_ref, o_ref, st_ref, *,
                 layer, chunk, n_chunks):
    c = chunk
    nlev = _hgrn_levels(c)

    @pl.when(pl.program_id(2) == 0)
    def _():
        st_ref[...] = jnp.zeros_like(st_ref)

    lg = lbl_ref[...]
    e = jnp.exp(lg - jnp.max(lg, axis=0, keepdims=True))
    lb = jnp.sum(e[:layer + 1], axis=0, keepdims=True) / jnp.sum(e, axis=0, keepdims=True)
    og = og_ref[...]
    row = lax.broadcasted_iota(jnp.int32, (c, c), 0)
    col = lax.broadcasted_iota(jnp.int32, (c, c), 1)

    def body(ci, carry, *, mild):
        sl = pl.ds(pl.multiple_of(ci * c, c), c)
        q = q_ref[0, sl, :]
        fr = f_ref[0, sl, :]
        v = i_ref[0, sl, :]
        gt = g_ref[0, sl, :]
        qa = q * _sigmoid(q)
        f = lb + (1.0 - lb) * _sigmoid(fr)
        k = 1.0 - f
        logf = jnp.log(f)
        hi = logf.astype(BF16)
        lo = (logf - hi.astype(F32)).astype(BF16)
        g2 = jnp.concatenate([hi, lo], axis=1)
        if mild:
            e2 = jnp.dot(m_ref[nlev * c:(nlev + 1) * c, :], g2, preferred_element_type=F32)
            b_incl = e2[:, :HEAD] + e2[:, HEAD:]
            dmid = b_incl - b_incl[c // 2 - 1:c // 2, :]
            p = lax.dot_general((qa * jnp.exp(dmid)).astype(BF16),
                                (k * jnp.exp(-dmid)).astype(BF16), NT_DIMS,
                                preferred_element_type=F32)
            s = jnp.where(row >= col, p, 0.0)
            rev = b_incl[c - 1:c, :] - b_incl
        else:
            e2 = jnp.dot(m_ref[...], g2, preferred_element_type=F32)
            ex = e2[:, :HEAD] + e2[:, HEAD:]
            s = jnp.where(row == col,
                          lax.dot_general(qa.astype(BF16), k.astype(BF16), NT_DIMS,
                                          preferred_element_type=F32),
                          0.0)
            for l in range(nlev):
                sh = nlev - 1 - l
                x = jnp.exp(ex[l * c:(l + 1) * c])
                p = lax.dot_general((qa * x).astype(BF16), (k * x).astype(BF16), NT_DIMS,
                                    preferred_element_type=F32)
                mask = (((row >> (sh + 1)) == (col >> (sh + 1)))
                        & (((row >> sh) & 1) == 1) & (((col >> sh) & 1) == 0))
                s = jnp.where(mask, p, s)
            b_incl = ex[nlev * c:(nlev + 1) * c]
            rev = ex[(nlev + 1) * c:(nlev + 2) * c]
        vb = v.astype(BF16)
        intra = jnp.dot(s.astype(BF16), vb, preferred_element_type=F32)
        st = st_ref[...]
        inter = lax.dot_general((qa * jnp.exp(b_incl)).astype(BF16), st.astype(BF16), NT_DIMS,
                                preferred_element_type=F32)
        o = inter + intra
        y = _rms(o) * og * (gt * _sigmoid(gt))
        o_ref[0, sl, :] = y.astype(o_ref.dtype)

        k2 = (k * jnp.exp(rev)).astype(BF16)
        upd = jnp.dot(v.T.astype(BF16), k2, preferred_element_type=F32)
        st_ref[...] = st * jnp.exp(b_incl[c - 1:c, :]) + upd
        return carry

    mild = jnp.min(lb) >= math.exp(-HG_SAFE_EXPONENT / (c // 2))

    @pl.when(mild)
    def _():
        lax.fori_loop(0, n_chunks, functools.partial(body, mild=True), 0, unroll=4)

    @pl.when(jnp.logical_not(mild))
    def _():
        lax.fori_loop(0, n_chunks, functools.partial(body, mild=False), 0, unroll=4)


def hgrn_scan(y, lb_logits, out_gain, *, layer, seq_block):
    b, s, d4 = y.shape
    d = d4 // 4
    nh = d // HEAD
    c = min(HG_CHUNK, seq_block)
    msel = jnp.asarray(_hgrn_sum_matrix(c), BF16)
    nl = lb_logits.shape[0]
    kern = functools.partial(_hgrn_kernel, layer=layer, chunk=c, n_chunks=seq_block // c)

    def part(p):
        return pl.BlockSpec((1, seq_block, HEAD), lambda bi, hi, si, p=p: (bi, si, p * nh + hi))

    return pl.pallas_call(
        kern,
        grid=(b, nh, s // seq_block),
        in_specs=[part(0), part(1), part(2), part(3),
                  pl.BlockSpec((nl, HEAD), lambda bi, hi, si: (0, hi)),
                  pl.BlockSpec((1, HEAD), lambda bi, hi, si: (0, 0)),
                  pl.BlockSpec(msel.shape, lambda bi, hi, si: (0, 0))],
        out_specs=pl.BlockSpec((1, seq_block, HEAD), lambda bi, hi, si: (bi, si, hi)),
        out_shape=jax.ShapeDtypeStruct((b, s, d), BF16),
        scratch_shapes=[pltpu.VMEM((HEAD, HEAD), F32)],
        compiler_params=_cparams(("parallel", "parallel", "arbitrary")),
        name="hgrn_scan",
    )(y, y, y, y, lb_logits, out_gain.reshape(1, HEAD), msel)


def _peer_prep_kernel(h_ref, g_ref, wq_ref, keys_ref, xn_ref, s1_ref, s2_ref):
    xn = (_rms(h_ref[...]) * g_ref[...]).astype(BF16)
    xn_ref[...] = xn
    q = jnp.dot(xn, wq_ref[...], preferred_element_type=F32)
    k0 = keys_ref[0].astype(BF16)
    k1 = keys_ref[1].astype(BF16)
    for hh in range(PEER_HEADS):
        base = hh * 2 * HEAD
        q1 = q[:, base:base + HEAD].astype(BF16)
        q2 = q[:, base + HEAD:base + 2 * HEAD].astype(BF16)
        s1_ref[hh] = lax.dot_general(k0, q1, NT_DIMS, preferred_element_type=F32)
        s2_ref[hh] = lax.dot_general(k1, q2, NT_DIMS, preferred_element_type=F32)


def peer_prep(h, gain, wq, keys, *, tm):
    t, d = h.shape
    nq = wq.shape[1]
    return pl.pallas_call(
        _peer_prep_kernel,
        grid=(t // tm,),
        in_specs=[pl.BlockSpec((tm, d), lambda i: (i, 0)),
                  pl.BlockSpec((1, d), lambda i: (0, 0)),
                  pl.BlockSpec((d, nq), lambda i: (0, 0)),
                  pl.BlockSpec(keys.shape, lambda i: (0, 0, 0))],
        out_specs=[pl.BlockSpec((tm, d), lambda i: (i, 0)),
                   pl.BlockSpec((PEER_HEADS, N_KEYS, tm), lambda i: (0, 0, i)),
                   pl.BlockSpec((PEER_HEADS, N_KEYS, tm), lambda i: (0, 0, i))],
        out_shape=[jax.ShapeDtypeStruct((t, d), BF16),
                   jax.ShapeDtypeStruct((PEER_HEADS, N_KEYS, t), F32),
                   jax.ShapeDtypeStruct((PEER_HEADS, N_KEYS, t), F32)],
        compiler_params=_cparams(("parallel",)),
        name="peer_prep",
    )(h, gain.reshape(1, d), wq, keys)


def _oddeven_merge(lo, hi, r):
    step = r * 2
    if step < hi - lo:
        yield from _oddeven_merge(lo, hi, step)
        yield from _oddeven_merge(lo + r, hi, step)
        yield from [(i, i + r) for i in range(lo + r, hi - r, step)]
    else:
        yield (lo, lo + r)


def _oddeven_sort(lo, hi):
    if hi - lo >= 1:
        mid = lo + (hi - lo) // 2
        yield from _oddeven_sort(lo, mid)
        yield from _oddeven_sort(mid + 1, hi)
        yield from _oddeven_merge(lo, hi, 1)


SORT16 = tuple(_oddeven_sort(0, PEER_TOPK - 1))
BITONIC16 = tuple((i, i + h) for h in (8, 4, 2, 1) for i in range(PEER_TOPK) if i % (2 * h) < h)
SUBLANES = 8


def _compare_exchange(v, pairs):
    v = list(v)
    for i, j in pairs:
        v[i], v[j] = jnp.maximum(v[i], v[j]), jnp.minimum(v[i], v[j])
    return v


def _merge_sublane_lists(v):
    n = len(v)
    for shift in (4, 2, 1):
        w = [pltpu.roll(x, shift, 0) for x in v]
        v = _compare_exchange([jnp.maximum(v[k], w[n - 1 - k]) for k in range(n)], BITONIC16)
    return v


def _top16(x):
    slabs = [x[SUBLANES * k:SUBLANES * (k + 1), :] for k in range(PEER_TOPK)]
    return _merge_sublane_lists(_compare_exchange(slabs, SORT16))


def _peer_topk_kernel(s1_ref, s2_ref, cnt_ref, ar_ref, rk_ref, bt_ref):
    k = PEER_TOPK
    lanes = s1_ref.shape[2]
    sub = lax.broadcasted_iota(jnp.int32, (SUBLANES, lanes), 0)

    def head(hh, carry):
        s1 = s1_ref[hh]
        s2 = s2_ref[hh]
        va = _top16(s1)
        vb = _top16(s2)
        bd = vb[0]
        for p in range(1, SUBLANES):
            bd = jnp.where(sub == p, vb[p], bd)
        cand = []
        for i in range(k):
            c = va[i] + bd
            lim = k // (i + 1)
            cand.append(c if lim >= SUBLANES else jnp.where(sub < lim, c, -jnp.inf))
        tail = [va[0] + vb[SUBLANES + q] for q in range(k - SUBLANES)]
        top16 = _merge_sublane_lists(cand)
        tau = top16[k - SUBLANES - 1]
        for q in range(k - SUBLANES):
            tau = jnp.minimum(tau, jnp.maximum(top16[k - 1 - q], tail[q]))
        top = va[0] + vb[0]
        zc = jnp.zeros((SUBLANES, lanes), F32)
        for c in cand:
            zc = zc + jnp.where(c >= tau, jnp.exp(c - top), 0.0)
        zt = jnp.zeros((SUBLANES, lanes), F32)
        for c in tail:
            zt = zt + jnp.where(c >= tau, jnp.exp(c - top), 0.0)
        z = jnp.sum(zc, axis=0, keepdims=True) + zt[0:1, :]
        tau_row = tau[0:1, :]
        b_rows = [x[0:1, :] for x in vb]
        cnt = jnp.zeros(s1.shape, F32)
        rank2 = jnp.zeros(s2.shape, F32)
        for j in range(k):
            cnt = cnt + jnp.where(s1 + b_rows[j] >= tau_row, 1.0, 0.0)
            rank2 = rank2 + jnp.where(b_rows[j] > s2, 1.0, 0.0)
        cnt_ref[hh] = cnt
        ar_ref[hh] = jnp.exp(s1 - va[0][0:1, :]) / z
        bt = jnp.exp(s2 - b_rows[0])
        for lb in range(lanes // LANES):
            ln = slice(lb * LANES, (lb + 1) * LANES)
            rk_ref[hh, lb] = pltpu.bitcast(rank2[:, ln].astype(BF16), jnp.uint32)
            bt_ref[hh, lb] = pltpu.bitcast(bt[:, ln].astype(BF16), jnp.uint32)
        return carry

    lax.fori_loop(0, PEER_HEADS, head, 0)


def peer_topk(s1t, s2t, *, tt):
    nh, nk, t = s1t.shape
    spec = pl.BlockSpec((nh, nk, tt), lambda i: (0, 0, i))
    shp = jax.ShapeDtypeStruct((nh, nk, t), F32)
    tspec = pl.BlockSpec((nh, tt // LANES, nk // 2, LANES), lambda i: (0, i, 0, 0))
    tshp = jax.ShapeDtypeStruct((nh, t // LANES, nk // 2, LANES), jnp.uint32)
    return pl.pallas_call(
        _peer_topk_kernel,
        grid=(t // tt,),
        in_specs=[spec, spec],
        out_specs=[spec, spec, tspec, tspec],
        out_shape=[shp, shp, tshp, tshp],
        compiler_params=_cparams(("parallel",)),
        name="peer_topk",
    )(s1t, s2t)


GELU_C0 = math.sqrt(2.0 / math.pi)
GELU_C1 = GELU_C0 * 0.044715


def _gelu(z):
    return 0.5 * z * (1.0 + jnp.tanh(z * (GELU_C0 + GELU_C1 * (z * z))))


def _peer_main_kernel(h_ref, xn_ref, u_ref, vt_ref, cnt_ref, ar_ref, rk_ref, bt_ref, o_ref,
                      acc_ref, zt_ref, a_ref, *, n_blocks, blocks_per_tile, lane_blocks_per_iter):
    g_step = pl.program_id(0)
    et, tt = zt_ref.shape
    n_e1 = et // N_KEYS
    jb = jnp.maximum(g_step - 1, 0) % blocks_per_tile

    @pl.when(g_step == 0)
    def _():
        a_ref[...] = jnp.zeros_like(a_ref)

    @pl.when(jb == 0)
    def _():
        acc_ref[...] = jnp.zeros_like(acc_ref)

    acc_ref[...] += jnp.dot(vt_ref[...], a_ref[...], preferred_element_type=F32)
    zt_ref[...] = lax.dot_general(u_ref[...], xn_ref[...], NT_DIMS, preferred_element_type=F32)

    @pl.when((g_step >= 1) & (jb == blocks_per_tile - 1))
    def _():
        o_ref[...] = h_ref[...] + acc_ref[...].T

    def lane_group(li, carry):
        for sub in range(lane_blocks_per_iter):
            lb = li * lane_blocks_per_iter + sub
            ln = pl.ds(pl.multiple_of(lb * LANES, LANES), LANES)
            for c in range(n_e1):
                rows = slice(c * N_KEYS, (c + 1) * N_KEYS)
                g = jnp.zeros((N_KEYS, LANES), BF16)
                for hh in range(PEER_HEADS):
                    cnt = cnt_ref[hh, c:c + 1, ln].astype(BF16)
                    ar = ar_ref[hh, c:c + 1, ln].astype(BF16)
                    rk = pltpu.bitcast(rk_ref[hh, lb], BF16)
                    bt = pltpu.bitcast(bt_ref[hh, lb], BF16)
                    g = g + jnp.where(rk < cnt, bt * ar, 0)
                a_ref[rows, ln] = _gelu(zt_ref[rows, ln].astype(BF16)) * g
        return carry

    lax.fori_loop(0, tt // (LANES * lane_blocks_per_iter), lane_group, 0)


def peer_main(h, xn, u, vt, cnt, ar, rank_tiles, bt_tiles, *, tt, et, lane_blocks_per_iter):
    t, d = h.shape
    ne = u.shape[0]
    n_e1 = et // N_KEYS
    bpt = ne // et
    n_blocks = (t // tt) * bpt
    kern = functools.partial(_peer_main_kernel, n_blocks=n_blocks, blocks_per_tile=bpt,
                             lane_blocks_per_iter=lane_blocks_per_iter)

    def cur(g):
        return jnp.minimum(g, n_blocks - 1)

    def prev(g):
        return jnp.maximum(g - 1, 0)

    row_spec = pl.BlockSpec((PEER_HEADS, n_e1, tt), lambda g: (0, cur(g) % bpt, cur(g) // bpt))
    tok_spec = pl.BlockSpec((PEER_HEADS, tt // LANES, N_KEYS // 2, LANES),
                            lambda g: (0, cur(g) // bpt, 0, 0))
    return pl.pallas_call(
        kern,
        grid=(n_blocks + 1,),
        in_specs=[pl.BlockSpec((tt, d), lambda g: (prev(g) // bpt, 0)),
                  pl.BlockSpec((tt, d), lambda g: (cur(g) // bpt, 0)),
                  pl.BlockSpec((et, d), lambda g: (cur(g) % bpt, 0)),
                  pl.BlockSpec((d, et), lambda g: (0, prev(g) % bpt)),
                  row_spec, row_spec, tok_spec, tok_spec],
        out_specs=pl.BlockSpec((tt, d), lambda g: (prev(g) // bpt, 0)),
        out_shape=jax.ShapeDtypeStruct((t, d), F32),
        scratch_shapes=[pltpu.VMEM((d, tt), F32), pltpu.VMEM((et, tt), F32),
                        pltpu.VMEM((et, tt), BF16)],
        compiler_params=_cparams(("arbitrary",)),
        name="peer_main",
    )(h, xn, u, vt, cnt, ar, rank_tiles, bt_tiles)


def peer_layer(h, gain, wq, keys, u, v):
    t = h.shape[0]
    xn, s1t, s2t = peer_prep(h, gain, wq.astype(BF16), keys, tm=min(512, t))
    cnt, ar, rank_tiles, bt_tiles = peer_topk(s1t, s2t, tt=min(256, t))
    return peer_main(h, xn, u.astype(BF16), v.astype(BF16).T, cnt, ar, rank_tiles, bt_tiles,
                     tt=min(512, t), et=2048, lane_blocks_per_iter=1)


def _rope_tables(s):
    half = ROT_DIM // 2
    inv = ROPE_THETA ** (-jnp.arange(half, dtype=F32) * 2.0 / ROT_DIM)
    ang = jnp.arange(s).astype(F32)[:, None] * inv[None, :]
    cos, sin = jnp.cos(ang), jnp.sin(ang)
    pad = HEAD - ROT_DIM
    ones = jnp.ones((s, pad), F32)
    zeros = jnp.zeros((s, pad), F32)
    zh = jnp.zeros((s, half), F32)
    ct = jnp.concatenate([cos, cos, ones], axis=1)
    at = jnp.concatenate([-sin, zh, zeros], axis=1)
    bt = jnp.concatenate([zh, sin, zeros], axis=1)
    return ct, at, bt


ATTN_QG = KV_PER_GROUP * Q_PER_KV * HEAD
ATTN_KG = KV_PER_GROUP * HEAD


def _attn_proj_kernel(h_ref, kvg_ref, ag_ref, wkv_ref, wq_ref, kn_ref, qn_ref,
                      ct_ref, at_ref, bt_ref, *refs):
    out_refs, (qs_ref, kvs_ref) = refs[:-2], refs[-2:]
    y = _rms(h_ref[...])
    kv = jnp.dot((y * kvg_ref[...]).astype(BF16), wkv_ref[...], preferred_element_type=F32)
    qq = jnp.dot((y * ag_ref[...]).astype(BF16), wq_ref[...], preferred_element_type=F32)
    ct, at, bt = ct_ref[...], at_ref[...], bt_ref[...]
    half = ROT_DIM // 2

    def head_norm_rope(x, gain):
        n = _rms(x) * gain
        return n * ct + pltpu.roll(n, HEAD - half, 1) * at + pltpu.roll(n, half, 1) * bt

    n_kh = kv.shape[1] // (2 * HEAD)
    for hd in range(n_kh):
        cs = slice(hd * HEAD, (hd + 1) * HEAD)
        kvs_ref[hd] = head_norm_rope(kv[:, cs], kn_ref[...])
        kvs_ref[n_kh + hd] = kv[:, n_kh * HEAD + hd * HEAD:n_kh * HEAD + (hd + 1) * HEAD]
    for hd in range(qq.shape[1] // HEAD):
        qs_ref[hd] = head_norm_rope(qq[:, hd * HEAD:(hd + 1) * HEAD], qn_ref[...])

    tm = qs_ref.shape[1]
    q_heads, k_heads = ATTN_QG // HEAD, ATTN_KG // HEAD
    for g, (_, dil) in enumerate(DIL_GROUPS):
        q_ref, k_ref, v_ref = out_refs[3 * g:3 * g + 3]
        for r in range(dil):
            rows = pl.ds(r, tm // dil, stride=dil) if dil > 1 else slice(None)
            for hd in range(q_heads):
                q_ref[:, (r * q_heads + hd) * HEAD:(r * q_heads + hd + 1) * HEAD] = (
                    qs_ref[g * q_heads + hd, rows, :].astype(q_ref.dtype))
            for hd in range(k_heads):
                cols = slice((r * k_heads + hd) * HEAD, (r * k_heads + hd + 1) * HEAD)
                k_ref[:, cols] = kvs_ref[g * k_heads + hd, rows, :].astype(k_ref.dtype)
                v_ref[:, cols] = kvs_ref[n_kh + g * k_heads + hd, rows, :].astype(v_ref.dtype)


def attn_proj(h, kv_gain, a_gain, wkv, wq, k_norm, q_norm, seq, *, tm):
    t, d = h.shape
    ct, at, bt = _rope_tables(seq)
    ns = seq // tm
    tab = pl.BlockSpec((tm, HEAD), lambda i: (i % ns, 0))
    vec = lambda n: pl.BlockSpec((1, n), lambda i: (0, 0))
    out_specs, out_shape = [], []
    for _, dil in DIL_GROUPS:
        for width in (ATTN_QG, ATTN_KG, ATTN_KG):
            out_specs.append(pl.BlockSpec((tm // dil, dil * width), lambda i: (i, 0)))
            out_shape.append(jax.ShapeDtypeStruct((t // dil, dil * width), BF16))
    outs = pl.pallas_call(
        _attn_proj_kernel,
        grid=(t // tm,),
        in_specs=[pl.BlockSpec((tm, d), lambda i: (i, 0)), vec(d), vec(d),
                  pl.BlockSpec(wkv.shape, lambda i: (0, 0)),
                  pl.BlockSpec(wq.shape, lambda i: (0, 0)),
                  vec(HEAD), vec(HEAD), tab, tab, tab],
        out_specs=out_specs,
        out_shape=out_shape,
        scratch_shapes=[pltpu.VMEM((wq.shape[1] // HEAD, tm, HEAD), F32),
                        pltpu.VMEM((wkv.shape[1] // HEAD, tm, HEAD), F32)],
        compiler_params=_cparams(("parallel",)),
        name="attn_proj",
    )(h, kv_gain.reshape(1, d), a_gain.reshape(1, d), wkv, wq,
      k_norm.reshape(1, HEAD), q_norm.reshape(1, HEAD), ct, at, bt)
    return [tuple(outs[3 * g:3 * g + 3]) for g in range(len(DIL_GROUPS))]


ATTN_EXP2_SCALE = HEAD ** -0.5 * math.log2(math.e)
MAX_WINDOW = max(w for w, _ in DIL_GROUPS)


def _attn_window_bias(w, dil):
    rows = Q_PER_KV * Q_BLOCK
    tq = w + (np.arange(rows)[:, None] % Q_BLOCK)
    diff = tq - np.arange(w + Q_BLOCK)[None, :]
    valid = (diff >= 0) & (diff <= w) & (diff % dil == 0)
    return np.where(valid, 0.0, -np.inf).astype(np.float32)


def _attn_kernel(q0_ref, q1_ref, q2_ref, k0_ref, k1_ref, k2_ref, v0_ref, v1_ref, v2_ref,
                 b0_ref, b1_ref, b2_ref, o_ref):
    qb = pl.program_id(2)
    t0 = qb * Q_BLOCK
    rows = Q_PER_KV * Q_BLOCK
    groups = list(zip(DIL_GROUPS, (q0_ref, q1_ref, q2_ref), (k0_ref, k1_ref, k2_ref),
                      (v0_ref, v1_ref, v2_ref), (b0_ref, b1_ref, b2_ref)))

    def raw_scores(q_ref, k_ref, ks, span):
        q2 = jnp.concatenate([q_ref[0, :, r * HEAD:(r + 1) * HEAD] for r in range(Q_PER_KV)],
                             axis=0)
        return lax.dot_general(q2, k_ref[0, pl.ds(ks, span), :], NT_DIMS,
                               preferred_element_type=F32)

    def finish(scores, starts):
        m = functools.reduce(jnp.maximum, [jnp.max(s, axis=-1, keepdims=True) for s in scores])
        den = jnp.zeros((rows, 1), F32)
        out = jnp.zeros((rows, HEAD), F32)
        for s, ks, ((w, _), _, _, v_ref, _) in zip(scores, starts, groups):
            p = jnp.exp2((s - m) * ATTN_EXP2_SCALE)
            den = den + jnp.sum(p, axis=-1, keepdims=True)
            out = out + jnp.dot(p.astype(BF16), v_ref[0, pl.ds(ks, w + Q_BLOCK), :],
                                preferred_element_type=F32)
        out = out / den
        for r in range(Q_PER_KV):
            o_ref[0, :, r * HEAD:(r + 1) * HEAD] = (
                out[r * Q_BLOCK:(r + 1) * Q_BLOCK].astype(o_ref.dtype))

    @pl.when(t0 >= MAX_WINDOW)
    def _():
        scores, starts = [], []
        for (w, dil), q_ref, k_ref, _, b_ref in groups:
            ks = pl.multiple_of(t0 - w, Q_BLOCK)
            scores.append(raw_scores(q_ref, k_ref, ks, w + Q_BLOCK) + b_ref[...])
            starts.append(ks)
        finish(scores, starts)

    @pl.when(t0 < MAX_WINDOW)
    def _():
        scores, starts = [], []
        for (w, dil), q_ref, k_ref, _, _ in groups:
            span = w + Q_BLOCK
            ks = pl.multiple_of(jnp.maximum(t0 - w, 0), Q_BLOCK)
            tq = t0 + (lax.broadcasted_iota(jnp.int32, (rows, span), 0) & (Q_BLOCK - 1))
            diff = tq - (ks + lax.broadcasted_iota(jnp.int32, (rows, span), 1))
            valid = (diff >= 0) & (diff <= w) & ((diff & (dil - 1)) == 0)
            scores.append(jnp.where(valid, raw_scores(q_ref, k_ref, ks, span), -jnp.inf))
            starts.append(ks)
        finish(scores, starts)


def dilated_attention(q, k, v):
    b, s, _ = q.shape
    qcols = Q_PER_KV * HEAD

    def qspec(g):
        return pl.BlockSpec((1, Q_BLOCK, qcols), lambda bi, kv, qb, g=g: (bi, qb, g * KV_PER_GROUP + kv))

    def kvspec(g):
        return pl.BlockSpec((1, s, HEAD), lambda bi, kv, qb, g=g: (bi, 0, g * KV_PER_GROUP + kv))

    ng = len(DIL_GROUPS)
    biases = [jnp.asarray(_attn_window_bias(w, dil)) for w, dil in DIL_GROUPS]
    return pl.pallas_call(
        _attn_kernel,
        grid=(b, KV_PER_GROUP, s // Q_BLOCK),
        in_specs=([qspec(g) for g in range(ng)] + [kvspec(g) for g in range(ng)] * 2
                  + [pl.BlockSpec(bias.shape, lambda bi, kv, qb: (0, 0)) for bias in biases]),
        out_specs=pl.BlockSpec((1, Q_BLOCK, qcols), lambda bi, kv, qb: (bi, qb, kv)),
        out_shape=jax.ShapeDtypeStruct((b, s, KV_PER_GROUP * qcols), BF16),
        compiler_params=_cparams(("parallel", "parallel", "arbitrary")),
        name="dilated_attention",
    )(q, q, q, k, k, k, v, v, v, *biases)


ATTN_SUB = 128


def _band_bias(first):
    rq = np.arange(Q_PER_KV * ATTN_SUB)[:, None] % ATTN_SUB
    ck = np.arange(2 * ATTN_SUB)[None, :]
    diff = rq - ck if first else rq + ATTN_SUB - ck
    return np.where((diff >= 0) & (diff <= ATTN_SUB), 0.0, -np.inf).astype(np.float32)


def _attn_group_kernel(q_ref, k_ref, v_ref, bfirst_ref, bband_ref, o_ref, lse_ref, *, qb):
    nb = pl.program_id(3)
    scale = HEAD ** -0.5
    for sb in range(qb // ATTN_SUB):
        rows = slice(sb * ATTN_SUB, (sb + 1) * ATTN_SUB)
        n0 = nb * qb + sb * ATTN_SUB
        ks = pl.multiple_of(jnp.maximum(n0 - ATTN_SUB, 0), ATTN_SUB)
        q2 = jnp.concatenate([q_ref[0, rows, r * HEAD:(r + 1) * HEAD] for r in range(Q_PER_KV)],
                             axis=0)
        s = lax.dot_general(q2, k_ref[0, pl.ds(ks, 2 * ATTN_SUB), :], NT_DIMS,
                            preferred_element_type=F32)
        s = s + jnp.where(n0 == 0, bfirst_ref[...], bband_ref[...])
        m = jnp.max(s, axis=-1, keepdims=True)
        p = jnp.exp2((s - m) * ATTN_EXP2_SCALE)
        den = jnp.sum(p, axis=-1, keepdims=True)
        out = jnp.dot(p.astype(BF16), v_ref[0, pl.ds(ks, 2 * ATTN_SUB), :],
                      preferred_element_type=F32) / den
        lse = m * scale + jnp.log(den)
        for r in range(Q_PER_KV):
            part = slice(r * ATTN_SUB, (r + 1) * ATTN_SUB)
            o_ref[0, rows, r * HEAD:(r + 1) * HEAD] = out[part].astype(o_ref.dtype)
            lse_ref[0, rows, r * HEAD:(r + 1) * HEAD] = jnp.broadcast_to(lse[part],
                                                                          (ATTN_SUB, HEAD))


def attn_group(q, k, v, g, batch, *, qb):
    w, dil = DIL_GROUPS[g]
    assert w == ATTN_SUB * dil
    sub = q.shape[0] // batch
    qb = min(qb, sub)
    qcols = Q_PER_KV * HEAD
    kern = functools.partial(_attn_group_kernel, qb=qb)
    const = lambda bi, r, kv, nb: (0, 0)
    q_map = lambda bi, r, kv, nb: (bi, nb, r * KV_PER_GROUP + kv)
    kv_map = lambda bi, r, kv, nb: (bi, 0, r * KV_PER_GROUP + kv)
    o, lse = pl.pallas_call(
        kern,
        grid=(batch, dil, KV_PER_GROUP, sub // qb),
        in_specs=[pl.BlockSpec((1, qb, qcols), q_map),
                  pl.BlockSpec((1, sub, HEAD), kv_map),
                  pl.BlockSpec((1, sub, HEAD), kv_map),
                  pl.BlockSpec((Q_PER_KV * ATTN_SUB, 2 * ATTN_SUB), const),
                  pl.BlockSpec((Q_PER_KV * ATTN_SUB, 2 * ATTN_SUB), const)],
        out_specs=[pl.BlockSpec((1, qb, qcols), q_map), pl.BlockSpec((1, qb, qcols), q_map)],
        out_shape=[jax.ShapeDtypeStruct((batch, sub, dil * ATTN_QG), BF16),
                   jax.ShapeDtypeStruct((batch, sub, dil * ATTN_QG), F32)],
        compiler_params=_cparams(("parallel", "parallel", "parallel", "arbitrary")),
        name="attn_group",
    )(q.reshape(batch, sub, -1), k.reshape(batch, sub, -1), v.reshape(batch, sub, -1),
      jnp.asarray(_band_bias(True)), jnp.asarray(_band_bias(False)))
    return o.reshape(batch * sub, -1), lse.reshape(batch * sub, -1)


def _combine_proj_kernel(o0_ref, o1_ref, o2_ref, l0_ref, l1_ref, l2_ref, w_ref, r_ref, out_ref,
                         osc_ref, lsc_ref):
    tm = out_ref.shape[0]
    ng = len(DIL_GROUPS)
    n_heads = ATTN_QG // HEAD
    for g, (o_ref, l_ref) in enumerate(zip((o0_ref, o1_ref, o2_ref), (l0_ref, l1_ref, l2_ref))):
        dil = DIL_GROUPS[g][1]
        for r in range(dil):
            rows = pl.ds(r, tm // dil, stride=dil) if dil > 1 else slice(None)
            for hd in range(n_heads):
                cols = slice((r * n_heads + hd) * HEAD, (r * n_heads + hd + 1) * HEAD)
                osc_ref[g * n_heads + hd, rows, :] = o_ref[:, cols].astype(F32)
                lsc_ref[g * n_heads + hd, rows, :] = l_ref[:, cols]
    heads = []
    for hd in range(n_heads):
        lses = [lsc_ref[g * n_heads + hd] for g in range(ng)]
        m = functools.reduce(jnp.maximum, lses)
        ws = [jnp.exp(x - m) for x in lses]
        num = functools.reduce(lambda a, b: a + b,
                               [ws[g] * osc_ref[g * n_heads + hd] for g in range(ng)])
        heads.append((num / functools.reduce(lambda a, b: a + b, ws)).astype(BF16))
    comb = jnp.concatenate(heads, axis=1)
    out_ref[...] = r_ref[...] + jnp.dot(comb, w_ref[...], preferred_element_type=F32)


def combine_proj(outs, lses, w, res, *, tm):
    t, n = res.shape
    k = w.shape[0]
    ng = len(DIL_GROUPS)
    packed = [pl.BlockSpec((tm // dil, dil * k), lambda i: (i, 0)) for _, dil in DIL_GROUPS]
    return pl.pallas_call(
        _combine_proj_kernel,
        grid=(t // tm,),
        in_specs=packed + packed + [pl.BlockSpec((k, n), lambda i: (0, 0)),
                                    pl.BlockSpec((tm, n), lambda i: (i, 0))],
        out_specs=pl.BlockSpec((tm, n), lambda i: (i, 0)),
        out_shape=jax.ShapeDtypeStruct((t, n), F32),
        scratch_shapes=[pltpu.VMEM((ng * k // HEAD, tm, HEAD), F32),
                        pltpu.VMEM((ng * k // HEAD, tm, HEAD), F32)],
        compiler_params=_cparams(("parallel",)),
        name="combine_proj",
    )(*outs, *lses, w, res)


def kernel(x, hgrn_norm, hgrn_w_in, hgrn_lb_logits, hgrn_out_norm, hgrn_w_out, kv_norm, w_kv,
           k_norm, attn_norm, w_q, q_norm, w_o, ffn_norm, peer_w_q, peer_sub_keys, peer_u, peer_v):
    b, s, d = x.shape
    t = b * s
    depth = ffn_norm.shape[0]
    n_a = hgrn_norm.shape[0]
    h = x.reshape(t, d)
    tm = min(1024, t)
    for layer in range(depth):
        if layer < n_a:
            y = norm_matmul(h, hgrn_norm[layer], hgrn_w_in[layer].astype(BF16), tm=tm, tn=512)
            o = hgrn_scan(y.reshape(b, s, 4 * d), hgrn_lb_logits, hgrn_out_norm[layer],
                          layer=layer, seq_block=min(1024, s))
            h = matmul_residual(o.reshape(t, d), hgrn_w_out[layer].astype(BF16), h, tm=tm)
        else:
            j = layer - n_a
            qkv = attn_proj(h, kv_norm, attn_norm[j], w_kv.astype(BF16), w_q[j].astype(BF16),
                            k_norm, q_norm[j], s, tm=min(512, s))
            if j == 0:
                kv_sh = [(kg, vg) for _, kg, vg in qkv]
            parts = [attn_group(qkv[g][0], kv_sh[g][0], kv_sh[g][1], g, b, qb=512)
                     for g in range(len(DIL_GROUPS))]
            h = combine_proj([o for o, _ in parts], [l for _, l in parts],
                             w_o[j].astype(BF16), h, tm=min(512, t))
        h = peer_layer(h, ffn_norm[layer], peer_w_q[layer], peer_sub_keys[layer],
                       peer_u[layer], peer_v[layer])
    return h.reshape(b, s, d)
```

```python
import functools
import math

import numpy as np
import jax
import jax.numpy as jnp
from jax import lax
from jax.experimental import pallas as pl
from jax.experimental.pallas import tpu as pltpu

F32 = jnp.float32
BF16 = jnp.bfloat16
EPS = 1e-6

LANES = 128
HEAD = 128
HG_CHUNK = 128
HG_SAFE_EXPONENT = 80.0
DIL_GROUPS = ((128, 1), (512, 4), (2048, 16))
KV_PER_GROUP = 2
Q_PER_KV = 2
ROT_DIM = HEAD // 4
ROPE_THETA = 500000.0
Q_BLOCK = 128
PEER_HEADS = 8
PEER_TOPK = 16
N_KEYS = 128
VMEM_LIMIT = 56 * 1024 * 1024

NT_DIMS = (((1,), (1,)), ((), ()))


def _cparams(sem):
    return pltpu.CompilerParams(dimension_semantics=sem, vmem_limit_bytes=VMEM_LIMIT)


def _rms(x):
    return x * lax.rsqrt(jnp.mean(x * x, axis=-1, keepdims=True) + EPS)


def _sigmoid(x):
    return 1.0 / (1.0 + jnp.exp(-x))


def _norm_matmul_kernel(x_ref, g_ref, w_ref, o_ref, xn_ref):
    @pl.when(pl.program_id(1) == 0)
    def _():
        xn_ref[...] = (_rms(x_ref[...]) * g_ref[...]).astype(xn_ref.dtype)

    o_ref[...] = jnp.dot(xn_ref[...], w_ref[...],
                         preferred_element_type=F32).astype(o_ref.dtype)


def norm_matmul(x, gain, w, *, tm, tn, out_dtype=F32):
    t, d = x.shape
    n = w.shape[1]
    return pl.pallas_call(
        _norm_matmul_kernel,
        grid=(t // tm, n // tn),
        in_specs=[pl.BlockSpec((tm, d), lambda i, j: (i, 0)),
                  pl.BlockSpec((1, d), lambda i, j: (0, 0)),
                  pl.BlockSpec((d, tn), lambda i, j: (0, j))],
        out_specs=pl.BlockSpec((tm, tn), lambda i, j: (i, j)),
        out_shape=jax.ShapeDtypeStruct((t, n), out_dtype),
        scratch_shapes=[pltpu.VMEM((tm, d), BF16)],
        compiler_params=_cparams(("parallel", "arbitrary")),
        name="norm_matmul",
    )(x, gain.reshape(1, d), w)


def _matmul_res_kernel(a_ref, w_ref, r_ref, o_ref):
    o_ref[...] = r_ref[...] + jnp.dot(a_ref[...], w_ref[...], preferred_element_type=F32)


def matmul_residual(a, w, res, *, tm):
    t, k = a.shape
    n = w.shape[1]
    return pl.pallas_call(
        _matmul_res_kernel,
        grid=(t // tm,),
        in_specs=[pl.BlockSpec((tm, k), lambda i: (i, 0)),
                  pl.BlockSpec((k, n), lambda i: (0, 0)),
                  pl.BlockSpec((tm, n), lambda i: (i, 0))],
        out_specs=pl.BlockSpec((tm, n), lambda i: (i, 0)),
        out_shape=jax.ShapeDtypeStruct((t, n), F32),
        compiler_params=_cparams(("parallel",)),
        name="matmul_residual",
    )(a, w, res)


def _hgrn_levels(c):
    return int(math.log2(c))


def _hgrn_sum_matrix(c):
    nlev = _hgrn_levels(c)
    m = np.zeros(((nlev + 2) * c, c), np.float32)
    for l in range(nlev):
        h = c >> (l + 1)
        for r in range(c):
            mid = (r // (2 * h)) * 2 * h + h
            if r >= mid:
                m[l * c + r, mid:r + 1] = 1.0
            else:
                m[l * c + r, r + 1:mid] = 1.0
    for r in range(c):
        m[nlev * c + r, :r + 1] = 1.0
        m[(nlev + 1) * c + r, r + 1:] = 1.0
    return m


def _hgrn_kernel(x_ref, ng_ref, w_ref, lbl_ref, og_ref, m_ref, o_ref, xn_ref, y_ref, st_all_ref, *,
                 layer, chunk, n_chunks):
    c = chunk
    nlev = _hgrn_levels(c)
    head = pl.program_id(2)
    st_ref = st_all_ref.at[head]

    @pl.when(head == 0)
    def _():
        xn_ref[...] = (_rms(x_ref[0]) * ng_ref[...]).astype(xn_ref.dtype)

    @pl.when(pl.program_id(1) == 0)
    def _():
        st_ref[...] = jnp.zeros_like(st_ref)

    y_ref[...] = jnp.dot(xn_ref[...], w_ref[...], preferred_element_type=F32)

    lg = lbl_ref[...]
    e = jnp.exp(lg - jnp.max(lg, axis=0, keepdims=True))
    lb = jnp.sum(e[:layer + 1], axis=0, keepdims=True) / jnp.sum(e, axis=0, keepdims=True)
    og = og_ref[...]
    row = lax.broadcasted_iota(jnp.int32, (c, c), 0)
    col = lax.broadcasted_iota(jnp.int32, (c, c), 1)

    def body(ci, carry, *, mild):
        sl = pl.ds(pl.multiple_of(ci * c, c), c)
        q = y_ref[sl, 0:HEAD]
        fr = y_ref[sl, HEAD:2 * HEAD]
        v = y_ref[sl, 2 * HEAD:3 * HEAD]
        gt = y_ref[sl, 3 * HEAD:4 * HEAD]
        qa = q * _sigmoid(q)
        f = lb + (1.0 - lb) * _sigmoid(fr)
        k = 1.0 - f
        logf = jnp.log(f)
        hi = logf.astype(BF16)
        lo = (logf - hi.astype(F32)).astype(BF16)
        g2 = jnp.concatenate([hi, lo], axis=1)
        if mild:
            e2 = jnp.dot(m_ref[nlev * c:(nlev + 1) * c, :], g2, preferred_element_type=F32)
            b_incl = e2[:, :HEAD] + e2[:, HEAD:]
            dmid = b_incl - b_incl[c // 2 - 1:c // 2, :]
            p = lax.dot_general((qa * jnp.exp(dmid)).astype(BF16),
                                (k * jnp.exp(-dmid)).astype(BF16), NT_DIMS,
                                preferred_element_type=F32)
            s = jnp.where(row >= col, p, 0.0)
            rev = b_incl[c - 1:c, :] - b_incl
        else:
            e2 = jnp.dot(m_ref[...], g2, preferred_element_type=F32)
            ex = e2[:, :HEAD] + e2[:, HEAD:]
            s = jnp.where(row == col,
                          lax.dot_general(qa.astype(BF16), k.astype(BF16), NT_DIMS,
                                          preferred_element_type=F32),
                          0.0)
            for l in range(nlev):
                sh = nlev - 1 - l
                x = jnp.exp(ex[l * c:(l + 1) * c])
                p = lax.dot_general((qa * x).astype(BF16), (k * x).astype(BF16), NT_DIMS,
                                    preferred_element_type=F32)
                mask = (((row >> (sh + 1)) == (col >> (sh + 1)))
                        & (((row >> sh) & 1) == 1) & (((col >> sh) & 1) == 0))
                s = jnp.where(mask, p, s)
            b_incl = ex[nlev * c:(nlev + 1) * c]
            rev = ex[(nlev + 1) * c:(nlev + 2) * c]
        vb = v.astype(BF16)
        intra = jnp.dot(s.astype(BF16), vb, preferred_element_type=F32)
        st = st_ref[...]
        inter = lax.dot_general((qa * jnp.exp(b_incl)).astype(BF16), st.astype(BF16), NT_DIMS,
                                preferred_element_type=F32)
        o = inter + intra
        y = _rms(o) * og * (gt * _sigmoid(gt))
        o_ref[0, sl, :] = y.astype(o_ref.dtype)

        k2 = (k * jnp.exp(rev)).astype(BF16)
        upd = jnp.dot(v.T.astype(BF16), k2, preferred_element_type=F32)
        st_ref[...] = st * jnp.exp(b_incl[c - 1:c, :]) + upd
        return carry

    mild = jnp.min(lb) >= math.exp(-HG_SAFE_EXPONENT / (c // 2))

    @pl.when(mild)
    def _():
        lax.fori_loop(0, n_chunks, functools.partial(body, mild=True), 0, unroll=4)

    @pl.when(jnp.logical_not(mild))
    def _():
        lax.fori_loop(0, n_chunks, functools.partial(body, mild=False), 0, unroll=4)


def hgrn_mixer(x, norm_gain, w_in, lb_logits, out_gain, *, layer, seq_block):
    b, s, d = x.shape
    nh = d // HEAD
    c = min(HG_CHUNK, seq_block)
    msel = jnp.asarray(_hgrn_sum_matrix(c), BF16)
    nl = lb_logits.shape[0]
    w_heads = (w_in.astype(BF16).reshape(d, 4, nh, HEAD).transpose(0, 2, 1, 3)
               .reshape(d, nh * 4 * HEAD))
    kern = functools.partial(_hgrn_kernel, layer=layer, chunk=c, n_chunks=seq_block // c)
    return pl.pallas_call(
        kern,
        grid=(b, s // seq_block, nh),
        in_specs=[pl.BlockSpec((1, seq_block, d), lambda bi, si, hi: (bi, si, 0)),
                  pl.BlockSpec((1, d), lambda bi, si, hi: (0, 0)),
                  pl.BlockSpec((d, 4 * HEAD), lambda bi, si, hi: (0, hi)),
                  pl.BlockSpec((nl, HEAD), lambda bi, si, hi: (0, hi)),
                  pl.BlockSpec((1, HEAD), lambda bi, si, hi: (0, 0)),
                  pl.BlockSpec(msel.shape, lambda bi, si, hi: (0, 0))],
        out_specs=pl.BlockSpec((1, seq_block, HEAD), lambda bi, si, hi: (bi, si, hi)),
        out_shape=jax.ShapeDtypeStruct((b, s, d), BF16),
        scratch_shapes=[pltpu.VMEM((seq_block, d), BF16), pltpu.VMEM((seq_block, 4 * HEAD), F32),
                        pltpu.VMEM((nh, HEAD, HEAD), F32)],
        compiler_params=_cparams(("parallel", "arbitrary", "arbitrary")),
        name="hgrn_mixer",
    )(x, norm_gain.reshape(1, d), w_heads, lb_logits, out_gain.reshape(1, HEAD), msel)


def _peer_prep_kernel(h_ref, g_ref, wq_ref, keys_ref, xn_ref, s1_ref, s2_ref):
    xn = (_rms(h_ref[...]) * g_ref[...]).astype(BF16)
    xn_ref[...] = xn
    q = jnp.dot(xn, wq_ref[...], preferred_element_type=F32)
    k0 = keys_ref[0].astype(BF16)
    k1 = keys_ref[1].astype(BF16)
    for hh in range(PEER_HEADS):
        base = hh * 2 * HEAD
        q1 = q[:, base:base + HEAD].astype(BF16)
        q2 = q[:, base + HEAD:base + 2 * HEAD].astype(BF16)
        s1_ref[hh] = lax.dot_general(k0, q1, NT_DIMS, preferred_element_type=F32)
        s2_ref[hh] = lax.dot_general(k1, q2, NT_DIMS, preferred_element_type=F32)


def peer_prep(h, gain, wq, keys, *, tm):
    t, d = h.shape
    nq = wq.shape[1]
    return pl.pallas_call(
        _peer_prep_kernel,
        grid=(t // tm,),
        in_specs=[pl.BlockSpec((tm, d), lambda i: (i, 0)),
                  pl.BlockSpec((1, d), lambda i: (0, 0)),
                  pl.BlockSpec((d, nq), lambda i: (0, 0)),
                  pl.BlockSpec(keys.shape, lambda i: (0, 0, 0))],
        out_specs=[pl.BlockSpec((tm, d), lambda i: (i, 0)),
                   pl.BlockSpec((PEER_HEADS, N_KEYS, tm), lambda i: (0, 0, i)),
                   pl.BlockSpec((PEER_HEADS, N_KEYS, tm), lambda i: (0, 0, i))],
        out_shape=[jax.ShapeDtypeStruct((t, d), BF16),
                   jax.ShapeDtypeStruct((PEER_HEADS, N_KEYS, t), F32),
                   jax.ShapeDtypeStruct((PEER_HEADS, N_KEYS, t), F32)],
        compiler_params=_cparams(("parallel",)),
        name="peer_prep",
    )(h, gain.reshape(1, d), wq, keys)


def _oddeven_merge(lo, hi, r):
    step = r * 2
    if step < hi - lo:
        yield from _oddeven_merge(lo, hi, step)
        yield from _oddeven_merge(lo + r, hi, step)
        yield from [(i, i + r) for i in range(lo + r, hi - r, step)]
    else:
        yield (lo, lo + r)


def _oddeven_sort(lo, hi):
    if hi - lo >= 1:
        mid = lo + (hi - lo) // 2
        yield from _oddeven_sort(lo, mid)
        yield from _oddeven_sort(mid + 1, hi)
        yield from _oddeven_merge(lo, hi, 1)


SORT16 = tuple(_oddeven_sort(0, PEER_TOPK - 1))
BITONIC16 = tuple((i, i + h) for h in (8, 4, 2, 1) for i in range(PEER_TOPK) if i % (2 * h) < h)
SUBLANES = 8


def _compare_exchange(v, pairs):
    v = list(v)
    for i, j in pairs:
        v[i], v[j] = jnp.maximum(v[i], v[j]), jnp.minimum(v[i], v[j])
    return v


def _merge_sublane_lists(v):
    n = len(v)
    for shift in (4, 2, 1):
        w = [pltpu.roll(x, shift, 0) for x in v]
        v = _compare_exchange([jnp.maximum(v[k], w[n - 1 - k]) for k in range(n)], BITONIC16)
    return v


def _top16(x):
    slabs = [x[SUBLANES * k:SUBLANES * (k + 1), :] for k in range(PEER_TOPK)]
    return _merge_sublane_lists(_compare_exchange(slabs, SORT16))


def _peer_topk_kernel(s1_ref, s2_ref, cnt_ref, ar_ref, rk_ref, bt_ref):
    k = PEER_TOPK
    lanes = s1_ref.shape[2]
    sub = lax.broadcasted_iota(jnp.int32, (SUBLANES, lanes), 0)

    def head(hh, carry):
        s1 = s1_ref[hh]
        s2 = s2_ref[hh]
        va = _top16(s1)
        vb = _top16(s2)
        bd = vb[0]
        for p in range(1, SUBLANES):
            bd = jnp.where(sub == p, vb[p], bd)
        cand = []
        for i in range(k):
            c = va[i] + bd
            lim = k // (i + 1)
            cand.append(c if lim >= SUBLANES else jnp.where(sub < lim, c, -jnp.inf))
        tail = [va[0] + vb[SUBLANES + q] for q in range(k - SUBLANES)]
        top16 = _merge_sublane_lists(cand)
        tau = top16[k - SUBLANES - 1]
        for q in range(k - SUBLANES):
            tau = jnp.minimum(tau, jnp.maximum(top16[k - 1 - q], tail[q]))
        top = va[0] + vb[0]
        zc = jnp.zeros((SUBLANES, lanes), F32)
        for c in cand:
            zc = zc + jnp.where(c >= tau, jnp.exp(c - top), 0.0)
        zt = jnp.zeros((SUBLANES, lanes), F32)
        for c in tail:
            zt = zt + jnp.where(c >= tau, jnp.exp(c - top), 0.0)
        z = jnp.sum(zc, axis=0, keepdims=True) + zt[0:1, :]
        tau_row = tau[0:1, :]
        b_rows = [x[0:1, :] for x in vb]
        cnt = jnp.zeros(s1.shape, F32)
        rank2 = jnp.zeros(s2.shape, F32)
        for j in range(k):
            cnt = cnt + jnp.where(s1 + b_rows[j] >= tau_row, 1.0, 0.0)
            rank2 = rank2 + jnp.where(b_rows[j] > s2, 1.0, 0.0)
        cnt_ref[hh] = cnt
        ar_ref[hh] = jnp.exp(s1 - va[0][0:1, :]) / z
        bt = jnp.exp(s2 - b_rows[0])
        for lb in range(lanes // LANES):
            ln = slice(lb * LANES, (lb + 1) * LANES)
            rk_ref[hh, lb] = pltpu.bitcast(rank2[:, ln].astype(BF16), jnp.uint32)
            bt_ref[hh, lb] = pltpu.bitcast(bt[:, ln].astype(BF16), jnp.uint32)
        return carry

    lax.fori_loop(0, PEER_HEADS, head, 0)


def peer_topk(s1t, s2t, *, tt):
    nh, nk, t = s1t.shape
    spec = pl.BlockSpec((nh, nk, tt), lambda i: (0, 0, i))
    shp = jax.ShapeDtypeStruct((nh, nk, t), F32)
    tspec = pl.BlockSpec((nh, tt // LANES, nk // 2, LANES), lambda i: (0, i, 0, 0))
    tshp = jax.ShapeDtypeStruct((nh, t // LANES, nk // 2, LANES), jnp.uint32)
    return pl.pallas_call(
        _peer_topk_kernel,
        grid=(t // tt,),
        in_specs=[spec, spec],
        out_specs=[spec, spec, tspec, tspec],
        out_shape=[shp, shp, tshp, tshp],
        compiler_params=_cparams(("parallel",)),
        name="peer_topk",
    )(s1t, s2t)


GELU_C0 = math.sqrt(2.0 / math.pi)
GELU_C1 = GELU_C0 * 0.044715


def _gelu(z):
    return 0.5 * z * (1.0 + jnp.tanh(z * (GELU_C0 + GELU_C1 * (z * z))))


def _peer_main_kernel(h_ref, xn_ref, u_ref, vt_ref, cnt_ref, ar_ref, rk_ref, bt_ref, o_ref,
                      acc_ref, zt_ref, a_ref, *, n_blocks, blocks_per_tile, lane_blocks_per_iter):
    g_step = pl.program_id(0)
    et, tt = zt_ref.shape
    n_e1 = et // N_KEYS
    jb = jnp.maximum(g_step - 1, 0) % blocks_per_tile

    @pl.when(g_step == 0)
    def _():
        a_ref[...] = jnp.zeros_like(a_ref)

    @pl.when(jb == 0)
    def _():
        acc_ref[...] = jnp.zeros_like(acc_ref)

    acc_ref[...] += jnp.dot(vt_ref[...], a_ref[...], preferred_element_type=F32)
    zt_ref[...] = lax.dot_general(u_ref[...], xn_ref[...], NT_DIMS, preferred_element_type=F32)

    @pl.when((g_step >= 1) & (jb == blocks_per_tile - 1))
    def _():
        o_ref[...] = h_ref[...] + acc_ref[...].T

    def lane_group(li, carry):
        for sub in range(lane_blocks_per_iter):
            lb = li * lane_blocks_per_iter + sub
            ln = pl.ds(pl.multiple_of(lb * LANES, LANES), LANES)
            for c in range(n_e1):
                rows = slice(c * N_KEYS, (c + 1) * N_KEYS)
                g = jnp.zeros((N_KEYS, LANES), BF16)
                for hh in range(PEER_HEADS):
                    cnt = cnt_ref[hh, c:c + 1, ln].astype(BF16)
                    ar = ar_ref[hh, c:c + 1, ln].astype(BF16)
                    rk = pltpu.bitcast(rk_ref[hh, lb], BF16)
                    bt = pltpu.bitcast(bt_ref[hh, lb], BF16)
                    g = g + jnp.where(rk < cnt, bt * ar, 0)
                a_ref[rows, ln] = _gelu(zt_ref[rows, ln].astype(BF16)) * g
        return carry

    lax.fori_loop(0, tt // (LANES * lane_blocks_per_iter), lane_group, 0)


def peer_main(h, xn, u, vt, cnt, ar, rank_tiles, bt_tiles, *, tt, et, lane_blocks_per_iter):
    t, d = h.shape
    ne = u.shape[0]
    n_e1 = et // N_KEYS
    bpt = ne // et
    n_blocks = (t // tt) * bpt
    kern = functools.partial(_peer_main_kernel, n_blocks=n_blocks, blocks_per_tile=bpt,
                             lane_blocks_per_iter=lane_blocks_per_iter)

    def cur(g):
        return jnp.minimum(g, n_blocks - 1)

    def prev(g):
        return jnp.maximum(g - 1, 0)

    row_spec = pl.BlockSpec((PEER_HEADS, n_e1, tt), lambda g: (0, cur(g) % bpt, cur(g) // bpt))
    tok_spec = pl.BlockSpec((PEER_HEADS, tt // LANES, N_KEYS // 2, LANES),
                            lambda g: (0, cur(g) // bpt, 0, 0))
    return pl.pallas_call(
        kern,
        grid=(n_blocks + 1,),
        in_specs=[pl.BlockSpec((tt, d), lambda g: (prev(g) // bpt, 0)),
                  pl.BlockSpec((tt, d), lambda g: (cur(g) // bpt, 0)),
                  pl.BlockSpec((et, d), lambda g: (cur(g) % bpt, 0)),
                  pl.BlockSpec((d, et), lambda g: (0, prev(g) % bpt)),
                  row_spec, row_spec, tok_spec, tok_spec],
        out_specs=pl.BlockSpec((tt, d), lambda g: (prev(g) // bpt, 0)),
        out_shape=jax.ShapeDtypeStruct((t, d), F32),
        scratch_shapes=[pltpu.VMEM((d, tt), F32), pltpu.VMEM((et, tt), F32),
                        pltpu.VMEM((et, tt), BF16)],
        compiler_params=_cparams(("arbitrary",)),
        name="peer_main",
    )(h, xn, u, vt, cnt, ar, rank_tiles, bt_tiles)


def peer_layer(h, gain, wq, keys, u, v):
    t = h.shape[0]
    xn, s1t, s2t = peer_prep(h, gain, wq.astype(BF16), keys, tm=min(512, t))
    cnt, ar, rank_tiles, bt_tiles = peer_topk(s1t, s2t, tt=min(256, t))
    return peer_main(h, xn, u.astype(BF16), v.astype(BF16).T, cnt, ar, rank_tiles, bt_tiles,
                     tt=min(512, t), et=2048, lane_blocks_per_iter=1)


def _rope_tables(s):
    half = ROT_DIM // 2
    inv = ROPE_THETA ** (-jnp.arange(half, dtype=F32) * 2.0 / ROT_DIM)
    ang = jnp.arange(s).astype(F32)[:, None] * inv[None, :]
    cos, sin = jnp.cos(ang), jnp.sin(ang)
    pad = HEAD - ROT_DIM
    ones = jnp.ones((s, pad), F32)
    zeros = jnp.zeros((s, pad), F32)
    zh = jnp.zeros((s, half), F32)
    ct = jnp.concatenate([cos, cos, ones], axis=1)
    at = jnp.concatenate([-sin, zh, zeros], axis=1)
    bt = jnp.concatenate([zh, sin, zeros], axis=1)
    return ct, at, bt


ATTN_QG = KV_PER_GROUP * Q_PER_KV * HEAD
ATTN_KG = KV_PER_GROUP * HEAD


def _attn_proj_kernel(h_ref, kvg_ref, ag_ref, wkv_ref, wq_ref, kn_ref, qn_ref,
                      ct_ref, at_ref, bt_ref, *refs):
    out_refs, (qs_ref, kvs_ref) = refs[:-2], refs[-2:]
    y = _rms(h_ref[...])
    kv = jnp.dot((y * kvg_ref[...]).astype(BF16), wkv_ref[...], preferred_element_type=F32)
    qq = jnp.dot((y * ag_ref[...]).astype(BF16), wq_ref[...], preferred_element_type=F32)
    ct, at, bt = ct_ref[...], at_ref[...], bt_ref[...]
    half = ROT_DIM // 2

    def head_norm_rope(x, gain):
        n = _rms(x) * gain
        return n * ct + pltpu.roll(n, HEAD - half, 1) * at + pltpu.roll(n, half, 1) * bt

    n_kh = kv.shape[1] // (2 * HEAD)
    for hd in range(n_kh):
        cs = slice(hd * HEAD, (hd + 1) * HEAD)
        kvs_ref[hd] = head_norm_rope(kv[:, cs], kn_ref[...])
        kvs_ref[n_kh + hd] = kv[:, n_kh * HEAD + hd * HEAD:n_kh * HEAD + (hd + 1) * HEAD]
    for hd in range(qq.shape[1] // HEAD):
        qs_ref[hd] = head_norm_rope(qq[:, hd * HEAD:(hd + 1) * HEAD], qn_ref[...])

    tm = qs_ref.shape[1]
    q_heads, k_heads = ATTN_QG // HEAD, ATTN_KG // HEAD
    for g, (_, dil) in enumerate(DIL_GROUPS):
        q_ref, k_ref, v_ref = out_refs[3 * g:3 * g + 3]
        for r in range(dil):
            rows = pl.ds(r, tm // dil, stride=dil) if dil > 1 else slice(None)
            for hd in range(q_heads):
                q_ref[:, (r * q_heads + hd) * HEAD:(r * q_heads + hd + 1) * HEAD] = (
                    qs_ref[g * q_heads + hd, rows, :].astype(q_ref.dtype))
            for hd in range(k_heads):
                cols = slice((r * k_heads + hd) * HEAD, (r * k_heads + hd + 1) * HEAD)
                k_ref[:, cols] = kvs_ref[g * k_heads + hd, rows, :].astype(k_ref.dtype)
                v_ref[:, cols] = kvs_ref[n_kh + g * k_heads + hd, rows, :].astype(v_ref.dtype)


def attn_proj(h, kv_gain, a_gain, wkv, wq, k_norm, q_norm, seq, *, tm):
    t, d = h.shape
    ct, at, bt = _rope_tables(seq)
    ns = seq // tm
    tab = pl.BlockSpec((tm, HEAD), lambda i: (i % ns, 0))
    vec = lambda n: pl.BlockSpec((1, n), lambda i: (0, 0))
    out_specs, out_shape = [], []
    for _, dil in DIL_GROUPS:
        for width in (ATTN_QG, ATTN_KG, ATTN_KG):
            out_specs.append(pl.BlockSpec((tm // dil, dil * width), lambda i: (i, 0)))
            out_shape.append(jax.ShapeDtypeStruct((t // dil, dil * width), BF16))
    outs = pl.pallas_call(
        _attn_proj_kernel,
        grid=(t // tm,),
        in_specs=[pl.BlockSpec((tm, d), lambda i: (i, 0)), vec(d), vec(d),
                  pl.BlockSpec(wkv.shape, lambda i: (0, 0)),
                  pl.BlockSpec(wq.shape, lambda i: (0, 0)),
                  vec(HEAD), vec(HEAD), tab, tab, tab],
        out_specs=out_specs,
        out_shape=out_shape,
        scratch_shapes=[pltpu.VMEM((wq.shape[1] // HEAD, tm, HEAD), F32),
                        pltpu.VMEM((wkv.shape[1] // HEAD, tm, HEAD), F32)],
        compiler_params=_cparams(("parallel",)),
        name="attn_proj",
    )(h, kv_gain.reshape(1, d), a_gain.reshape(1, d), wkv, wq,
      k_norm.reshape(1, HEAD), q_norm.reshape(1, HEAD), ct, at, bt)
    return [tuple(outs[3 * g:3 * g + 3]) for g in range(len(DIL_GROUPS))]


ATTN_EXP2_SCALE = HEAD ** -0.5 * math.log2(math.e)
MAX_WINDOW = max(w for w, _ in DIL_GROUPS)


def _attn_window_bias(w, dil):
    rows = Q_PER_KV * Q_BLOCK
    tq = w + (np.arange(rows)[:, None] % Q_BLOCK)
    diff = tq - np.arange(w + Q_BLOCK)[None, :]
    valid = (diff >= 0) & (diff <= w) & (diff % dil == 0)
    return np.where(valid, 0.0, -np.inf).astype(np.float32)


def _attn_kernel(q0_ref, q1_ref, q2_ref, k0_ref, k1_ref, k2_ref, v0_ref, v1_ref, v2_ref,
                 b0_ref, b1_ref, b2_ref, o_ref):
    qb = pl.program_id(2)
    t0 = qb * Q_BLOCK
    rows = Q_PER_KV * Q_BLOCK
    groups = list(zip(DIL_GROUPS, (q0_ref, q1_ref, q2_ref), (k0_ref, k1_ref, k2_ref),
                      (v0_ref, v1_ref, v2_ref), (b0_ref, b1_ref, b2_ref)))

    def raw_scores(q_ref, k_ref, ks, span):
        q2 = jnp.concatenate([q_ref[0, :, r * HEAD:(r + 1) * HEAD] for r in range(Q_PER_KV)],
                             axis=0)
        return lax.dot_general(q2, k_ref[0, pl.ds(ks, span), :], NT_DIMS,
                               preferred_element_type=F32)

    def finish(scores, starts):
        m = functools.reduce(jnp.maximum, [jnp.max(s, axis=-1, keepdims=True) for s in scores])
        den = jnp.zeros((rows, 1), F32)
        out = jnp.zeros((rows, HEAD), F32)
        for s, ks, ((w, _), _, _, v_ref, _) in zip(scores, starts, groups):
            p = jnp.exp2((s - m) * ATTN_EXP2_SCALE)
            den = den + jnp.sum(p, axis=-1, keepdims=True)
            out = out + jnp.dot(p.astype(BF16), v_ref[0, pl.ds(ks, w + Q_BLOCK), :],
                                preferred_element_type=F32)
        out = out / den
        for r in range(Q_PER_KV):
            o_ref[0, :, r * HEAD:(r + 1) * HEAD] = (
                out[r * Q_BLOCK:(r + 1) * Q_BLOCK].astype(o_ref.dtype))

    @pl.when(t0 >= MAX_WINDOW)
    def _():
        scores, starts = [], []
        for (w, dil), q_ref, k_ref, _, b_ref in groups:
            ks = pl.multiple_of(t0 - w, Q_BLOCK)
            scores.append(raw_scores(q_ref, k_ref, ks, w + Q_BLOCK) + b_ref[...])
            starts.append(ks)
        finish(scores, starts)

    @pl.when(t0 < MAX_WINDOW)
    def _():
        scores, starts = [], []
        for (w, dil), q_ref, k_ref, _, _ in groups:
            span = w + Q_BLOCK
            ks = pl.multiple_of(jnp.maximum(t0 - w, 0), Q_BLOCK)
            tq = t0 + (lax.broadcasted_iota(jnp.int32, (rows, span), 0) & (Q_BLOCK - 1))
            diff = tq - (ks + lax.broadcasted_iota(jnp.int32, (rows, span), 1))
            valid = (diff >= 0) & (diff <= w) & ((diff & (dil - 1)) == 0)
            scores.append(jnp.where(valid, raw_scores(q_ref, k_ref, ks, span), -jnp.inf))
            starts.append(ks)
        finish(scores, starts)


def dilated_attention(q, k, v):
    b, s, _ = q.shape
    qcols = Q_PER_KV * HEAD

    def qspec(g):
        return pl.BlockSpec((1, Q_BLOCK, qcols), lambda bi, kv, qb, g=g: (bi, qb, g * KV_PER_GROUP + kv))

    def kvspec(g):
        return pl.BlockSpec((1, s, HEAD), lambda bi, kv, qb, g=g: (bi, 0, g * KV_PER_GROUP + kv))

    ng = len(DIL_GROUPS)
    biases = [jnp.asarray(_attn_window_bias(w, dil)) for w, dil in DIL_GROUPS]
    return pl.pallas_call(
        _attn_kernel,
        grid=(b, KV_PER_GROUP, s // Q_BLOCK),
        in_specs=([qspec(g) for g in range(ng)] + [kvspec(g) for g in range(ng)] * 2
                  + [pl.BlockSpec(bias.shape, lambda bi, kv, qb: (0, 0)) for bias in biases]),
        out_specs=pl.BlockSpec((1, Q_BLOCK, qcols), lambda bi, kv, qb: (bi, qb, kv)),
        out_shape=jax.ShapeDtypeStruct((b, s, KV_PER_GROUP * qcols), BF16),
        compiler_params=_cparams(("parallel", "parallel", "arbitrary")),
        name="dilated_attention",
    )(q, q, q, k, k, k, v, v, v, *biases)


ATTN_SUB = 128


def _band_bias(first):
    rq = np.arange(Q_PER_KV * ATTN_SUB)[:, None] % ATTN_SUB
    ck = np.arange(2 * ATTN_SUB)[None, :]
    diff = rq - ck if first else rq + ATTN_SUB - ck
    return np.where((diff >= 0) & (diff <= ATTN_SUB), 0.0, -np.inf).astype(np.float32)


def _attn_group_kernel(q_ref, k_ref, v_ref, bfirst_ref, bband_ref, o_ref, lse_ref, *, qb):
    nb = pl.program_id(3)
    scale = HEAD ** -0.5
    for sb in range(qb // ATTN_SUB):
        rows = slice(sb * ATTN_SUB, (sb + 1) * ATTN_SUB)
        n0 = nb * qb + sb * ATTN_SUB
        ks = pl.multiple_of(jnp.maximum(n0 - ATTN_SUB, 0), ATTN_SUB)
        q2 = jnp.concatenate([q_ref[0, rows, r * HEAD:(r + 1) * HEAD] for r in range(Q_PER_KV)],
                             axis=0)
        s = lax.dot_general(q2, k_ref[0, pl.ds(ks, 2 * ATTN_SUB), :], NT_DIMS,
                            preferred_element_type=F32)
        s = s + jnp.where(n0 == 0, bfirst_ref[...], bband_ref[...])
        m = jnp.max(s, axis=-1, keepdims=True)
        p = jnp.exp2((s - m) * ATTN_EXP2_SCALE)
        den = jnp.sum(p, axis=-1, keepdims=True)
        out = jnp.dot(p.astype(BF16), v_ref[0, pl.ds(ks, 2 * ATTN_SUB), :],
                      preferred_element_type=F32) / den
        lse = m * scale + jnp.log(den)
        for r in range(Q_PER_KV):
            part = slice(r * ATTN_SUB, (r + 1) * ATTN_SUB)
            o_ref[0, rows, r * HEAD:(r + 1) * HEAD] = out[part].astype(o_ref.dtype)
            lse_ref[0, rows, r * HEAD:(r + 1) * HEAD] = jnp.broadcast_to(lse[part],
                                                                          (ATTN_SUB, HEAD))


def attn_group(q, k, v, g, batch, *, qb):
    w, dil = DIL_GROUPS[g]
    assert w == ATTN_SUB * dil
    sub = q.shape[0] // batch
    qb = min(qb, sub)
    qcols = Q_PER_KV * HEAD
    kern = functools.partial(_attn_group_kernel, qb=qb)
    const = lambda bi, r, kv, nb: (0, 0)
    q_map = lambda bi, r, kv, nb: (bi, nb, r * KV_PER_GROUP + kv)
    kv_map = lambda bi, r, kv, nb: (bi, 0, r * KV_PER_GROUP + kv)
    o, lse = pl.pallas_call(
        kern,
        grid=(batch, dil, KV_PER_GROUP, sub // qb),
        in_specs=[pl.BlockSpec((1, qb, qcols), q_map),
                  pl.BlockSpec((1, sub, HEAD), kv_map),
                  pl.BlockSpec((1, sub, HEAD), kv_map),
                  pl.BlockSpec((Q_PER_KV * ATTN_SUB, 2 * ATTN_SUB), const),
                  pl.BlockSpec((Q_PER_KV * ATTN_SUB, 2 * ATTN_SUB), const)],
        out_specs=[pl.BlockSpec((1, qb, qcols), q_map), pl.BlockSpec((1, qb, qcols), q_map)],
        out_shape=[jax.ShapeDtypeStruct((batch, sub, dil * ATTN_QG), BF16),
                   jax.ShapeDtypeStruct((batch, sub, dil * ATTN_QG), F32)],
        compiler_params=_cparams(("parallel", "parallel", "parallel", "arbitrary")),
        name="attn_group",
    )(q.reshape(batch, sub, -1), k.reshape(batch, sub, -1), v.reshape(batch, sub, -1),
      jnp.asarray(_band_bias(True)), jnp.asarray(_band_bias(False)))
    return o.reshape(batch * sub, -1), lse.reshape(batch * sub, -1)


def _combine_proj_kernel(o0_ref, o1_ref, o2_ref, l0_ref, l1_ref, l2_ref, w_ref, r_ref, out_ref,
                         osc_ref, lsc_ref):
    tm = out_ref.shape[0]
    ng = len(DIL_GROUPS)
    n_heads = ATTN_QG // HEAD
    for g, (o_ref, l_ref) in enumerate(zip((o0_ref, o1_ref, o2_ref), (l0_ref, l1_ref, l2_ref))):
        dil = DIL_GROUPS[g][1]
        for r in range(dil):
            rows = pl.ds(r, tm // dil, stride=dil) if dil > 1 else slice(None)
            for hd in range(n_heads):
                cols = slice((r * n_heads + hd) * HEAD, (r * n_heads + hd + 1) * HEAD)
                osc_ref[g * n_heads + hd, rows, :] = o_ref[:, cols].astype(F32)
                lsc_ref[g * n_heads + hd, rows, :] = l_ref[:, cols]
    heads = []
    for hd in range(n_heads):
        lses = [lsc_ref[g * n_heads + hd] for g in range(ng)]
        m = functools.reduce(jnp.maximum, lses)
        ws = [jnp.exp(x - m) for x in lses]
        num = functools.reduce(lambda a, b: a + b,
                               [ws[g] * osc_ref[g * n_heads + hd] for g in range(ng)])
        heads.append((num / functools.reduce(lambda a, b: a + b, ws)).astype(BF16))
    comb = jnp.concatenate(heads, axis=1)
    out_ref[...] = r_ref[...] + jnp.dot(comb, w_ref[...], preferred_element_type=F32)


def combine_proj(outs, lses, w, res, *, tm):
    t, n = res.shape
    k = w.shape[0]
    ng = len(DIL_GROUPS)
    packed = [pl.BlockSpec((tm // dil, dil * k), lambda i: (i, 0)) for _, dil in DIL_GROUPS]
    return pl.pallas_call(
        _combine_proj_kernel,
        grid=(t // tm,),
        in_specs=packed + packed + [pl.BlockSpec((k, n), lambda i: (0, 0)),
                                    pl.BlockSpec((tm, n), lambda i: (i, 0))],
        out_specs=pl.BlockSpec((tm, n), lambda i: (i, 0)),
        out_shape=jax.ShapeDtypeStruct((t, n), F32),
        scratch_shapes=[pltpu.VMEM((ng * k // HEAD, tm, HEAD), F32),
                        pltpu.VMEM((ng * k // HEAD, tm, HEAD), F32)],
        compiler_params=_cparams(("parallel",)),
        name="combine_proj",
    )(*outs, *lses, w, res)


def kernel(x, hgrn_norm, hgrn_w_in, hgrn_lb_logits, hgrn_out_norm, hgrn_w_out, kv_norm, w_kv,
           k_norm, attn_norm, w_q, q_norm, w_o, ffn_norm, peer_w_q, peer_sub_keys, peer_u, peer_v):
    b, s, d = x.shape
    t = b * s
    depth = ffn_norm.shape[0]
    n_a = hgrn_norm.shape[0]
    h = x.reshape(t, d)
    tm = min(1024, t)
    for layer in range(depth):
        if layer < n_a:
            o = hgrn_mixer(h.reshape(b, s, d), hgrn_norm[layer], hgrn_w_in[layer], hgrn_lb_logits,
                           hgrn_out_norm[layer], layer=layer, seq_block=min(1024, s))
            h = matmul_residual(o.reshape(t, d), hgrn_w_out[layer].astype(BF16), h, tm=tm)
        else:
            j = layer - n_a
            qkv = attn_proj(h, kv_norm, attn_norm[j], w_kv.astype(BF16), w_q[j].astype(BF16),
                            k_norm, q_norm[j], s, tm=min(512, s))
            if j == 0:
                kv_sh = [(kg, vg) for _, kg, vg in qkv]
            parts = [attn_group(qkv[g][0], kv_sh[g][0], kv_sh[g][1], g, b, qb=512)
                     for g in range(len(DIL_GROUPS))]
            h = combine_proj([o for o, _ in parts], [l for _, l in parts],
                             w_o[j].astype(BF16), h, tm=min(512, t))
        h = peer_layer(h, ffn_norm[layer], peer_w_q[layer], peer_sub_keys[layer],
                       peer_u[layer], peer_v[layer])
    return h.reshape(b, s, d)
```

```python
import functools
import math

import numpy as np
import jax
import jax.numpy as jnp
from jax import lax
from jax.experimental import pallas as pl
from jax.experimental.pallas import tpu as pltpu

F32 = jnp.float32
BF16 = jnp.bfloat16
EPS = 1e-6

LANES = 128
HEAD = 128
HG_CHUNK = 128
HG_SAFE_EXPONENT = 80.0
DIL_GROUPS = ((128, 1), (512, 4), (2048, 16))
KV_PER_GROUP = 2
Q_PER_KV = 2
ROT_DIM = HEAD // 4
ROPE_THETA = 500000.0
Q_BLOCK = 128
PEER_HEADS = 8
PEER_TOPK = 16
N_KEYS = 128
VMEM_LIMIT = 56 * 1024 * 1024

NT_DIMS = (((1,), (1,)), ((), ()))


def _cparams(sem):
    return pltpu.CompilerParams(dimension_semantics=sem, vmem_limit_bytes=VMEM_LIMIT)


def _rms(x):
    return x * lax.rsqrt(jnp.mean(x * x, axis=-1, keepdims=True) + EPS)


def _sigmoid(x):
    return 1.0 / (1.0 + jnp.exp(-x))


def _norm_matmul_kernel(x_ref, g_ref, w_ref, o_ref, xn_ref):
    @pl.when(pl.program_id(1) == 0)
    def _():
        xn_ref[...] = (_rms(x_ref[...]) * g_ref[...]).astype(xn_ref.dtype)

    o_ref[...] = jnp.dot(xn_ref[...], w_ref[...],
                         preferred_element_type=F32).astype(o_ref.dtype)


def norm_matmul(x, gain, w, *, tm, tn, out_dtype=F32):
    t, d = x.shape
    n = w.shape[1]
    return pl.pallas_call(
        _norm_matmul_kernel,
        grid=(t // tm, n // tn),
        in_specs=[pl.BlockSpec((tm, d), lambda i, j: (i, 0)),
                  pl.BlockSpec((1, d), lambda i, j: (0, 0)),
                  pl.BlockSpec((d, tn), lambda i, j: (0, j))],
        out_specs=pl.BlockSpec((tm, tn), lambda i, j: (i, j)),
        out_shape=jax.ShapeDtypeStruct((t, n), out_dtype),
        scratch_shapes=[pltpu.VMEM((tm, d), BF16)],
        compiler_params=_cparams(("parallel", "arbitrary")),
        name="norm_matmul",
    )(x, gain.reshape(1, d), w)


def _matmul_res_kernel(a_ref, w_ref, r_ref, o_ref):
    o_ref[...] = r_ref[...] + jnp.dot(a_ref[...], w_ref[...], preferred_element_type=F32)


def matmul_residual(a, w, res, *, tm):
    t, k = a.shape
    n = w.shape[1]
    return pl.pallas_call(
        _matmul_res_kernel,
        grid=(t // tm,),
        in_specs=[pl.BlockSpec((tm, k), lambda i: (i, 0)),
                  pl.BlockSpec((k, n), lambda i: (0, 0)),
                  pl.BlockSpec((tm, n), lambda i: (i, 0))],
        out_specs=pl.BlockSpec((tm, n), lambda i: (i, 0)),
        out_shape=jax.ShapeDtypeStruct((t, n), F32),
        compiler_params=_cparams(("parallel",)),
        name="matmul_residual",
    )(a, w, res)


def _hgrn_levels(c):
    return int(math.log2(c))


def _hgrn_sum_matrix(c):
    nlev = _hgrn_levels(c)
    m = np.zeros(((nlev + 2) * c, c), np.float32)
    for l in range(nlev):
        h = c >> (l + 1)
        for r in range(c):
            mid = (r // (2 * h)) * 2 * h + h
            if r >= mid:
                m[l * c + r, mid:r + 1] = 1.0
            else:
                m[l * c + r, r + 1:mid] = 1.0
    for r in range(c):
        m[nlev * c + r, :r + 1] = 1.0
        m[(nlev + 1) * c + r, r + 1:] = 1.0
    return m


def _hgrn_kernel(x_ref, ng_ref, w_ref, lbl_ref, og_ref, m_ref, o_ref, xn_ref, y_ref, st_all_ref, *,
                 layer, chunk, n_chunks):
    c = chunk
    nlev = _hgrn_levels(c)
    head = pl.program_id(2)
    st_ref = st_all_ref.at[head]

    @pl.when(head == 0)
    def _():
        xn_ref[...] = (_rms(x_ref[0]) * ng_ref[...]).astype(xn_ref.dtype)

    @pl.when(pl.program_id(1) == 0)
    def _():
        st_ref[...] = jnp.zeros_like(st_ref)

    y_ref[...] = jnp.dot(xn_ref[...], w_ref[...], preferred_element_type=F32)

    lg = lbl_ref[...]
    e = jnp.exp(lg - jnp.max(lg, axis=0, keepdims=True))
    lb = jnp.sum(e[:layer + 1], axis=0, keepdims=True) / jnp.sum(e, axis=0, keepdims=True)
    og = og_ref[...]
    row = lax.broadcasted_iota(jnp.int32, (c, c), 0)
    col = lax.broadcasted_iota(jnp.int32, (c, c), 1)

    def body(ci, carry, *, mild):
        sl = pl.ds(pl.multiple_of(ci * c, c), c)
        q = y_ref[sl, 0:HEAD]
        fr = y_ref[sl, HEAD:2 * HEAD]
        v = y_ref[sl, 2 * HEAD:3 * HEAD]
        gt = y_ref[sl, 3 * HEAD:4 * HEAD]
        qa = q * _sigmoid(q)
        f = lb + (1.0 - lb) * _sigmoid(fr)
        k = 1.0 - f
        logf = jnp.log(f)
        hi = logf.astype(BF16)
        lo = (logf - hi.astype(F32)).astype(BF16)
        g2 = jnp.concatenate([hi, lo], axis=1)
        if mild:
            e2 = jnp.dot(m_ref[nlev * c:(nlev + 1) * c, :], g2, preferred_element_type=F32)
            b_incl = e2[:, :HEAD] + e2[:, HEAD:]
            dmid = b_incl - b_incl[c // 2 - 1:c // 2, :]
            p = lax.dot_general((qa * jnp.exp(dmid)).astype(BF16),
                                (k * jnp.exp(-dmid)).astype(BF16), NT_DIMS,
                                preferred_element_type=F32)
            s = jnp.where(row >= col, p, 0.0)
            rev = b_incl[c - 1:c, :] - b_incl
        else:
            e2 = jnp.dot(m_ref[...], g2, preferred_element_type=F32)
            ex = e2[:, :HEAD] + e2[:, HEAD:]
            s = jnp.where(row == col,
                          lax.dot_general(qa.astype(BF16), k.astype(BF16), NT_DIMS,
                                          preferred_element_type=F32),
                          0.0)
            for l in range(nlev):
                sh = nlev - 1 - l
                x = jnp.exp(ex[l * c:(l + 1) * c])
                p = lax.dot_general((qa * x).astype(BF16), (k * x).astype(BF16), NT_DIMS,
                                    preferred_element_type=F32)
                mask = (((row >> (sh + 1)) == (col >> (sh + 1)))
                        & (((row >> sh) & 1) == 1) & (((col >> sh) & 1) == 0))
                s = jnp.where(mask, p, s)
            b_incl = ex[nlev * c:(nlev + 1) * c]
            rev = ex[(nlev + 1) * c:(nlev + 2) * c]
        vb = v.astype(BF16)
        intra = jnp.dot(s.astype(BF16), vb, preferred_element_type=F32)
        st = st_ref[...]
        inter = lax.dot_general((qa * jnp.exp(b_incl)).astype(BF16), st.astype(BF16), NT_DIMS,
                                preferred_element_type=F32)
        o = inter + intra
        y = _rms(o) * og * (gt * _sigmoid(gt))
        o_ref[0, sl, :] = y.astype(o_ref.dtype)

        k2 = (k * jnp.exp(rev)).astype(BF16)
        upd = jnp.dot(v.T.astype(BF16), k2, preferred_element_type=F32)
        st_ref[...] = st * jnp.exp(b_incl[c - 1:c, :]) + upd
        return carry

    mild = jnp.min(lb) >= math.exp(-HG_SAFE_EXPONENT / (c // 2))

    @pl.when(mild)
    def _():
        lax.fori_loop(0, n_chunks, functools.partial(body, mild=True), 0, unroll=4)

    @pl.when(jnp.logical_not(mild))
    def _():
        lax.fori_loop(0, n_chunks, functools.partial(body, mild=False), 0, unroll=4)


def hgrn_mixer(x, norm_gain, w_in, lb_logits, out_gain, *, layer, seq_block):
    b, s, d = x.shape
    nh = d // HEAD
    c = min(HG_CHUNK, seq_block)
    msel = jnp.asarray(_hgrn_sum_matrix(c), BF16)
    nl = lb_logits.shape[0]
    w_heads = (w_in.astype(BF16).reshape(d, 4, nh, HEAD).transpose(0, 2, 1, 3)
               .reshape(d, nh * 4 * HEAD))
    kern = functools.partial(_hgrn_kernel, layer=layer, chunk=c, n_chunks=seq_block // c)
    return pl.pallas_call(
        kern,
        grid=(b, s // seq_block, nh),
        in_specs=[pl.BlockSpec((1, seq_block, d), lambda bi, si, hi: (bi, si, 0)),
                  pl.BlockSpec((1, d), lambda bi, si, hi: (0, 0)),
                  pl.BlockSpec((d, 4 * HEAD), lambda bi, si, hi: (0, hi)),
                  pl.BlockSpec((nl, HEAD), lambda bi, si, hi: (0, hi)),
                  pl.BlockSpec((1, HEAD), lambda bi, si, hi: (0, 0)),
                  pl.BlockSpec(msel.shape, lambda bi, si, hi: (0, 0))],
        out_specs=pl.BlockSpec((1, seq_block, HEAD), lambda bi, si, hi: (bi, si, hi)),
        out_shape=jax.ShapeDtypeStruct((b, s, d), BF16),
        scratch_shapes=[pltpu.VMEM((seq_block, d), BF16), pltpu.VMEM((seq_block, 4 * HEAD), F32),
                        pltpu.VMEM((nh, HEAD, HEAD), F32)],
        compiler_params=_cparams(("parallel", "arbitrary", "arbitrary")),
        name="hgrn_mixer",
    )(x, norm_gain.reshape(1, d), w_heads, lb_logits, out_gain.reshape(1, HEAD), msel)


def _peer_prep_kernel(h_ref, g_ref, wq_ref, keys_ref, xn_ref, s1_ref, s2_ref):
    xn = (_rms(h_ref[...]) * g_ref[...]).astype(BF16)
    xn_ref[...] = xn
    q = jnp.dot(xn, wq_ref[...], preferred_element_type=F32)
    k0 = keys_ref[0].astype(BF16)
    k1 = keys_ref[1].astype(BF16)
    for hh in range(PEER_HEADS):
        base = hh * 2 * HEAD
        q1 = q[:, base:base + HEAD].astype(BF16)
        q2 = q[:, base + HEAD:base + 2 * HEAD].astype(BF16)
        s1_ref[hh] = lax.dot_general(k0, q1, NT_DIMS, preferred_element_type=F32)
        s2_ref[hh] = lax.dot_general(k1, q2, NT_DIMS, preferred_element_type=F32)


def peer_prep(h, gain, wq, keys, *, tm):
    t, d = h.shape
    nq = wq.shape[1]
    return pl.pallas_call(
        _peer_prep_kernel,
        grid=(t // tm,),
        in_specs=[pl.BlockSpec((tm, d), lambda i: (i, 0)),
                  pl.BlockSpec((1, d), lambda i: (0, 0)),
                  pl.BlockSpec((d, nq), lambda i: (0, 0)),
                  pl.BlockSpec(keys.shape, lambda i: (0, 0, 0))],
        out_specs=[pl.BlockSpec((tm, d), lambda i: (i, 0)),
                   pl.BlockSpec((PEER_HEADS, N_KEYS, tm), lambda i: (0, 0, i)),
                   pl.BlockSpec((PEER_HEADS, N_KEYS, tm), lambda i: (0, 0, i))],
        out_shape=[jax.ShapeDtypeStruct((t, d), BF16),
                   jax.ShapeDtypeStruct((PEER_HEADS, N_KEYS, t), F32),
                   jax.ShapeDtypeStruct((PEER_HEADS, N_KEYS, t), F32)],
        compiler_params=_cparams(("parallel",)),
        name="peer_prep",
    )(h, gain.reshape(1, d), wq, keys)


def _oddeven_merge(lo, hi, r):
    step = r * 2
    if step < hi - lo:
        yield from _oddeven_merge(lo, hi, step)
        yield from _oddeven_merge(lo + r, hi, step)
        yield from [(i, i + r) for i in range(lo + r, hi - r, step)]
    else:
        yield (lo, lo + r)


def _oddeven_sort(lo, hi):
    if hi - lo >= 1:
        mid = lo + (hi - lo) // 2
        yield from _oddeven_sort(lo, mid)
        yield from _oddeven_sort(mid + 1, hi)
        yield from _oddeven_merge(lo, hi, 1)


SORT16 = tuple(_oddeven_sort(0, PEER_TOPK - 1))
BITONIC16 = tuple((i, i + h) for h in (8, 4, 2, 1) for i in range(PEER_TOPK) if i % (2 * h) < h)
SUBLANES = 8


def _compare_exchange(v, pairs):
    v = list(v)
    for i, j in pairs:
        v[i], v[j] = jnp.maximum(v[i], v[j]), jnp.minimum(v[i], v[j])
    return v


def _merge_sublane_lists(v):
    n = len(v)
    for shift in (4, 2, 1):
        w = [pltpu.roll(x, shift, 0) for x in v]
        v = _compare_exchange([jnp.maximum(v[k], w[n - 1 - k]) for k in range(n)], BITONIC16)
    return v


def _top16(x):
    slabs = [x[SUBLANES * k:SUBLANES * (k + 1), :] for k in range(PEER_TOPK)]
    return _merge_sublane_lists(_compare_exchange(slabs, SORT16))


def _peer_topk_kernel(s1_ref, s2_ref, cnt_ref, ar_ref, rk_ref, bt_ref):
    k = PEER_TOPK
    lanes = s1_ref.shape[2]
    sub = lax.broadcasted_iota(jnp.int32, (SUBLANES, lanes), 0)

    def head(hh, carry):
        s1 = s1_ref[hh]
        s2 = s2_ref[hh]
        va = _top16(s1)
        vb = _top16(s2)
        bd = vb[0]
        for p in range(1, SUBLANES):
            bd = jnp.where(sub == p, vb[p], bd)
        cand = []
        for i in range(k):
            c = va[i] + bd
            lim = k // (i + 1)
            cand.append(c if lim >= SUBLANES else jnp.where(sub < lim, c, -jnp.inf))
        tail = [va[0] + vb[SUBLANES + q] for q in range(k - SUBLANES)]
        top16 = _merge_sublane_lists(cand)
        tau = top16[k - SUBLANES - 1]
        for q in range(k - SUBLANES):
            tau = jnp.minimum(tau, jnp.maximum(top16[k - 1 - q], tail[q]))
        top = va[0] + vb[0]
        zc = jnp.zeros((SUBLANES, lanes), F32)
        for c in cand:
            zc = zc + jnp.where(c >= tau, jnp.exp(c - top), 0.0)
        zt = jnp.zeros((SUBLANES, lanes), F32)
        for c in tail:
            zt = zt + jnp.where(c >= tau, jnp.exp(c - top), 0.0)
        z = jnp.sum(zc, axis=0, keepdims=True) + zt[0:1, :]
        tau_row = tau[0:1, :]
        b_rows = [x[0:1, :] for x in vb]
        cnt = jnp.zeros(s1.shape, F32)
        rank2 = jnp.zeros(s2.shape, F32)
        for j in range(k):
            cnt = cnt + jnp.where(s1 + b_rows[j] >= tau_row, 1.0, 0.0)
            rank2 = rank2 + jnp.where(b_rows[j] > s2, 1.0, 0.0)
        cnt_ref[hh] = cnt
        ar_ref[hh] = jnp.exp(s1 - va[0][0:1, :]) / z
        bt = jnp.exp(s2 - b_rows[0])
        for lb in range(lanes // LANES):
            ln = slice(lb * LANES, (lb + 1) * LANES)
            rk_ref[hh, lb] = pltpu.bitcast(rank2[:, ln].astype(BF16), jnp.uint32)
            bt_ref[hh, lb] = pltpu.bitcast(bt[:, ln].astype(BF16), jnp.uint32)
        return carry

    lax.fori_loop(0, PEER_HEADS, head, 0)


def peer_topk(s1t, s2t, *, tt):
    nh, nk, t = s1t.shape
    spec = pl.BlockSpec((nh, nk, tt), lambda i: (0, 0, i))
    shp = jax.ShapeDtypeStruct((nh, nk, t), F32)
    tspec = pl.BlockSpec((nh, tt // LANES, nk // 2, LANES), lambda i: (0, i, 0, 0))
    tshp = jax.ShapeDtypeStruct((nh, t // LANES, nk // 2, LANES), jnp.uint32)
    return pl.pallas_call(
        _peer_topk_kernel,
        grid=(t // tt,),
        in_specs=[spec, spec],
        out_specs=[spec, spec, tspec, tspec],
        out_shape=[shp, shp, tshp, tshp],
        compiler_params=_cparams(("parallel",)),
        name="peer_topk",
    )(s1t, s2t)


GELU_C0 = math.sqrt(2.0 / math.pi)
GELU_C1 = GELU_C0 * 0.044715


def _gelu(z):
    return 0.5 * z * (1.0 + jnp.tanh(z * (GELU_C0 + GELU_C1 * (z * z))))


def _peer_main_kernel(h_ref, xn_ref, u_ref, vt_ref, cnt_ref, ar_ref, rk_ref, bt_ref, o_ref,
                      acc_ref, zt_ref, a_ref, *, n_blocks, blocks_per_tile, lane_blocks_per_iter):
    g_step = pl.program_id(0)
    et, tt = zt_ref.shape
    n_e1 = et // N_KEYS
    jb = jnp.maximum(g_step - 1, 0) % blocks_per_tile

    @pl.when(g_step == 0)
    def _():
        a_ref[...] = jnp.zeros_like(a_ref)

    @pl.when(jb == 0)
    def _():
        acc_ref[...] = jnp.zeros_like(acc_ref)

    acc_ref[...] += jnp.dot(vt_ref[...], a_ref[...], preferred_element_type=F32)
    zt_ref[...] = lax.dot_general(u_ref[...].astype(BF16), xn_ref[...], NT_DIMS,
                                  preferred_element_type=F32)

    @pl.when((g_step >= 1) & (jb == blocks_per_tile - 1))
    def _():
        o_ref[...] = h_ref[...] + acc_ref[...].T

    def lane_group(li, carry):
        for sub in range(lane_blocks_per_iter):
            lb = li * lane_blocks_per_iter + sub
            ln = pl.ds(pl.multiple_of(lb * LANES, LANES), LANES)
            for c in range(n_e1):
                rows = slice(c * N_KEYS, (c + 1) * N_KEYS)
                g = jnp.zeros((N_KEYS, LANES), BF16)
                for hh in range(PEER_HEADS):
                    cnt = cnt_ref[hh, c:c + 1, ln].astype(BF16)
                    ar = ar_ref[hh, c:c + 1, ln].astype(BF16)
                    rk = pltpu.bitcast(rk_ref[hh, lb], BF16)
                    bt = pltpu.bitcast(bt_ref[hh, lb], BF16)
                    g = g + jnp.where(rk < cnt, bt * ar, 0)
                a_ref[rows, ln] = _gelu(zt_ref[rows, ln].astype(BF16)) * g
        return carry

    lax.fori_loop(0, tt // (LANES * lane_blocks_per_iter), lane_group, 0)


def _transpose_cast_kernel(x_ref, o_ref):
    o_ref[...] = x_ref[...].T.astype(o_ref.dtype)


def transpose_cast(stack, layer, *, rows):
    _, e, d = stack.shape
    return pl.pallas_call(
        _transpose_cast_kernel,
        grid=(e // rows,),
        in_specs=[pl.BlockSpec((None, rows, d), lambda j: (layer, j, 0))],
        out_specs=pl.BlockSpec((d, rows), lambda j: (0, j)),
        out_shape=jax.ShapeDtypeStruct((d, e), BF16),
        compiler_params=_cparams(("parallel",)),
        name="transpose_cast",
    )(stack)


def peer_main(h, xn, u_stack, layer, vt, cnt, ar, rank_tiles, bt_tiles, *, tt, et,
              lane_blocks_per_iter):
    t, d = h.shape
    ne = u_stack.shape[1]
    n_e1 = et // N_KEYS
    bpt = ne // et
    n_blocks = (t // tt) * bpt
    kern = functools.partial(_peer_main_kernel, n_blocks=n_blocks, blocks_per_tile=bpt,
                             lane_blocks_per_iter=lane_blocks_per_iter)

    def cur(g):
        return jnp.minimum(g, n_blocks - 1)

    def prev(g):
        return jnp.maximum(g - 1, 0)

    row_spec = pl.BlockSpec((PEER_HEADS, n_e1, tt), lambda g: (0, cur(g) % bpt, cur(g) // bpt))
    tok_spec = pl.BlockSpec((PEER_HEADS, tt // LANES, N_KEYS // 2, LANES),
                            lambda g: (0, cur(g) // bpt, 0, 0))
    return pl.pallas_call(
        kern,
        grid=(n_blocks + 1,),
        in_specs=[pl.BlockSpec((tt, d), lambda g: (prev(g) // bpt, 0)),
                  pl.BlockSpec((tt, d), lambda g: (cur(g) // bpt, 0)),
                  pl.BlockSpec((None, et, d), lambda g: (layer, cur(g) % bpt, 0)),
                  pl.BlockSpec((d, et), lambda g: (0, prev(g) % bpt)),
                  row_spec, row_spec, tok_spec, tok_spec],
        out_specs=pl.BlockSpec((tt, d), lambda g: (prev(g) // bpt, 0)),
        out_shape=jax.ShapeDtypeStruct((t, d), F32),
        scratch_shapes=[pltpu.VMEM((d, tt), F32), pltpu.VMEM((et, tt), F32),
                        pltpu.VMEM((et, tt), BF16)],
        compiler_params=_cparams(("arbitrary",)),
        name="peer_main",
    )(h, xn, u_stack, vt, cnt, ar, rank_tiles, bt_tiles)


def peer_layer(h, gain, wq, keys, u_stack, v_stack, layer):
    t = h.shape[0]
    xn, s1t, s2t = peer_prep(h, gain, wq.astype(BF16), keys, tm=min(512, t))
    cnt, ar, rank_tiles, bt_tiles = peer_topk(s1t, s2t, tt=min(256, t))
    vt = transpose_cast(v_stack, layer, rows=512)
    return peer_main(h, xn, u_stack, layer, vt, cnt, ar, rank_tiles, bt_tiles,
                     tt=min(512, t), et=2048, lane_blocks_per_iter=1)


def _rope_tables(s):
    half = ROT_DIM // 2
    inv = ROPE_THETA ** (-jnp.arange(half, dtype=F32) * 2.0 / ROT_DIM)
    ang = jnp.arange(s).astype(F32)[:, None] * inv[None, :]
    cos, sin = jnp.cos(ang), jnp.sin(ang)
    pad = HEAD - ROT_DIM
    ones = jnp.ones((s, pad), F32)
    zeros = jnp.zeros((s, pad), F32)
    zh = jnp.zeros((s, half), F32)
    ct = jnp.concatenate([cos, cos, ones], axis=1)
    at = jnp.concatenate([-sin, zh, zeros], axis=1)
    bt = jnp.concatenate([zh, sin, zeros], axis=1)
    return ct, at, bt


ATTN_QG = KV_PER_GROUP * Q_PER_KV * HEAD
ATTN_KG = KV_PER_GROUP * HEAD


def _attn_proj_kernel(h_ref, kvg_ref, ag_ref, wkv_ref, wq_ref, kn_ref, qn_ref,
                      ct_ref, at_ref, bt_ref, *refs):
    out_refs, (qs_ref, kvs_ref) = refs[:-2], refs[-2:]
    y = _rms(h_ref[...])
    kv = jnp.dot((y * kvg_ref[...]).astype(BF16), wkv_ref[...], preferred_element_type=F32)
    qq = jnp.dot((y * ag_ref[...]).astype(BF16), wq_ref[...], preferred_element_type=F32)
    ct, at, bt = ct_ref[...], at_ref[...], bt_ref[...]
    half = ROT_DIM // 2

    def head_norm_rope(x, gain):
        n = _rms(x) * gain
        return n * ct + pltpu.roll(n, HEAD - half, 1) * at + pltpu.roll(n, half, 1) * bt

    n_kh = kv.shape[1] // (2 * HEAD)
    for hd in range(n_kh):
        cs = slice(hd * HEAD, (hd + 1) * HEAD)
        kvs_ref[hd] = head_norm_rope(kv[:, cs], kn_ref[...])
        kvs_ref[n_kh + hd] = kv[:, n_kh * HEAD + hd * HEAD:n_kh * HEAD + (hd + 1) * HEAD]
    for hd in range(qq.shape[1] // HEAD):
        qs_ref[hd] = head_norm_rope(qq[:, hd * HEAD:(hd + 1) * HEAD], qn_ref[...])

    tm = qs_ref.shape[1]
    q_heads, k_heads = ATTN_QG // HEAD, ATTN_KG // HEAD
    for g, (_, dil) in enumerate(DIL_GROUPS):
        q_ref, k_ref, v_ref = out_refs[3 * g:3 * g + 3]
        for r in range(dil):
            rows = pl.ds(r, tm // dil, stride=dil) if dil > 1 else slice(None)
            for hd in range(q_heads):
                q_ref[:, (r * q_heads + hd) * HEAD:(r * q_heads + hd + 1) * HEAD] = (
                    qs_ref[g * q_heads + hd, rows, :].astype(q_ref.dtype))
            for hd in range(k_heads):
                cols = slice((r * k_heads + hd) * HEAD, (r * k_heads + hd + 1) * HEAD)
                k_ref[:, cols] = kvs_ref[g * k_heads + hd, rows, :].astype(k_ref.dtype)
                v_ref[:, cols] = kvs_ref[n_kh + g * k_heads + hd, rows, :].astype(v_ref.dtype)


def attn_proj(h, kv_gain, a_gain, wkv, wq, k_norm, q_norm, seq, *, tm):
    t, d = h.shape
    ct, at, bt = _rope_tables(seq)
    ns = seq // tm
    tab = pl.BlockSpec((tm, HEAD), lambda i: (i % ns, 0))
    vec = lambda n: pl.BlockSpec((1, n), lambda i: (0, 0))
    out_specs, out_shape = [], []
    for _, dil in DIL_GROUPS:
        for width in (ATTN_QG, ATTN_KG, ATTN_KG):
            out_specs.append(pl.BlockSpec((tm // dil, dil * width), lambda i: (i, 0)))
            out_shape.append(jax.ShapeDtypeStruct((t // dil, dil * width), BF16))
    outs = pl.pallas_call(
        _attn_proj_kernel,
        grid=(t // tm,),
        in_specs=[pl.BlockSpec((tm, d), lambda i: (i, 0)), vec(d), vec(d),
                  pl.BlockSpec(wkv.shape, lambda i: (0, 0)),
                  pl.BlockSpec(wq.shape, lambda i: (0, 0)),
                  vec(HEAD), vec(HEAD), tab, tab, tab],
        out_specs=out_specs,
        out_shape=out_shape,
        scratch_shapes=[pltpu.VMEM((wq.shape[1] // HEAD, tm, HEAD), F32),
                        pltpu.VMEM((wkv.shape[1] // HEAD, tm, HEAD), F32)],
        compiler_params=_cparams(("parallel",)),
        name="attn_proj",
    )(h, kv_gain.reshape(1, d), a_gain.reshape(1, d), wkv, wq,
      k_norm.reshape(1, HEAD), q_norm.reshape(1, HEAD), ct, at, bt)
    return [tuple(outs[3 * g:3 * g + 3]) for g in range(len(DIL_GROUPS))]


ATTN_EXP2_SCALE = HEAD ** -0.5 * math.log2(math.e)
MAX_WINDOW = max(w for w, _ in DIL_GROUPS)


def _attn_window_bias(w, dil):
    rows = Q_PER_KV * Q_BLOCK
    tq = w + (np.arange(rows)[:, None] % Q_BLOCK)
    diff = tq - np.arange(w + Q_BLOCK)[None, :]
    valid = (diff >= 0) & (diff <= w) & (diff % dil == 0)
    return np.where(valid, 0.0, -np.inf).astype(np.float32)


def _attn_kernel(q0_ref, q1_ref, q2_ref, k0_ref, k1_ref, k2_ref, v0_ref, v1_ref, v2_ref,
                 b0_ref, b1_ref, b2_ref, o_ref):
    qb = pl.program_id(2)
    t0 = qb * Q_BLOCK
    rows = Q_PER_KV * Q_BLOCK
    groups = list(zip(DIL_GROUPS, (q0_ref, q1_ref, q2_ref), (k0_ref, k1_ref, k2_ref),
                      (v0_ref, v1_ref, v2_ref), (b0_ref, b1_ref, b2_ref)))

    def raw_scores(q_ref, k_ref, ks, span):
        q2 = jnp.concatenate([q_ref[0, :, r * HEAD:(r + 1) * HEAD] for r in range(Q_PER_KV)],
                             axis=0)
        return lax.dot_general(q2, k_ref[0, pl.ds(ks, span), :], NT_DIMS,
                               preferred_element_type=F32)

    def finish(scores, starts):
        m = functools.reduce(jnp.maximum, [jnp.max(s, axis=-1, keepdims=True) for s in scores])
        den = jnp.zeros((rows, 1), F32)
        out = jnp.zeros((rows, HEAD), F32)
        for s, ks, ((w, _), _, _, v_ref, _) in zip(scores, starts, groups):
            p = jnp.exp2((s - m) * ATTN_EXP2_SCALE)
            den = den + jnp.sum(p, axis=-1, keepdims=True)
            out = out + jnp.dot(p.astype(BF16), v_ref[0, pl.ds(ks, w + Q_BLOCK), :],
                                preferred_element_type=F32)
        out = out / den
        for r in range(Q_PER_KV):
            o_ref[0, :, r * HEAD:(r + 1) * HEAD] = (
                out[r * Q_BLOCK:(r + 1) * Q_BLOCK].astype(o_ref.dtype))

    @pl.when(t0 >= MAX_WINDOW)
    def _():
        scores, starts = [], []
        for (w, dil), q_ref, k_ref, _, b_ref in groups:
            ks = pl.multiple_of(t0 - w, Q_BLOCK)
            scores.append(raw_scores(q_ref, k_ref, ks, w + Q_BLOCK) + b_ref[...])
            starts.append(ks)
        finish(scores, starts)

    @pl.when(t0 < MAX_WINDOW)
    def _():
        scores, starts = [], []
        for (w, dil), q_ref, k_ref, _, _ in groups:
            span = w + Q_BLOCK
            ks = pl.multiple_of(jnp.maximum(t0 - w, 0), Q_BLOCK)
            tq = t0 + (lax.broadcasted_iota(jnp.int32, (rows, span), 0) & (Q_BLOCK - 1))
            diff = tq - (ks + lax.broadcasted_iota(jnp.int32, (rows, span), 1))
            valid = (diff >= 0) & (diff <= w) & ((diff & (dil - 1)) == 0)
            scores.append(jnp.where(valid, raw_scores(q_ref, k_ref, ks, span), -jnp.inf))
            starts.append(ks)
        finish(scores, starts)


def dilated_attention(q, k, v):
    b, s, _ = q.shape
    qcols = Q_PER_KV * HEAD

    def qspec(g):
        return pl.BlockSpec((1, Q_BLOCK, qcols), lambda bi, kv, qb, g=g: (bi, qb, g * KV_PER_GROUP + kv))

    def kvspec(g):
        return pl.BlockSpec((1, s, HEAD), lambda bi, kv, qb, g=g: (bi, 0, g * KV_PER_GROUP + kv))

    ng = len(DIL_GROUPS)
    biases = [jnp.asarray(_attn_window_bias(w, dil)) for w, dil in DIL_GROUPS]
    return pl.pallas_call(
        _attn_kernel,
        grid=(b, KV_PER_GROUP, s // Q_BLOCK),
        in_specs=([qspec(g) for g in range(ng)] + [kvspec(g) for g in range(ng)] * 2
                  + [pl.BlockSpec(bias.shape, lambda bi, kv, qb: (0, 0)) for bias in biases]),
        out_specs=pl.BlockSpec((1, Q_BLOCK, qcols), lambda bi, kv, qb: (bi, qb, kv)),
        out_shape=jax.ShapeDtypeStruct((b, s, KV_PER_GROUP * qcols), BF16),
        compiler_params=_cparams(("parallel", "parallel", "arbitrary")),
        name="dilated_attention",
    )(q, q, q, k, k, k, v, v, v, *biases)


ATTN_SUB = 128


def _band_bias(first):
    rq = np.arange(Q_PER_KV * ATTN_SUB)[:, None] % ATTN_SUB
    ck = np.arange(2 * ATTN_SUB)[None, :]
    diff = rq - ck if first else rq + ATTN_SUB - ck
    return np.where((diff >= 0) & (diff <= ATTN_SUB), 0.0, -np.inf).astype(np.float32)


def _attn_group_kernel(q_ref, k_ref, v_ref, bfirst_ref, bband_ref, o_ref, lse_ref, *, qb):
    nb = pl.program_id(3)
    scale = HEAD ** -0.5
    for sb in range(qb // ATTN_SUB):
        rows = slice(sb * ATTN_SUB, (sb + 1) * ATTN_SUB)
        n0 = nb * qb + sb * ATTN_SUB
        ks = pl.multiple_of(jnp.maximum(n0 - ATTN_SUB, 0), ATTN_SUB)
        q2 = jnp.concatenate([q_ref[0, rows, r * HEAD:(r + 1) * HEAD] for r in range(Q_PER_KV)],
                             axis=0)
        s = lax.dot_general(q2, k_ref[0, pl.ds(ks, 2 * ATTN_SUB), :], NT_DIMS,
                            preferred_element_type=F32)
        s = s + jnp.where(n0 == 0, bfirst_ref[...], bband_ref[...])
        m = jnp.max(s, axis=-1, keepdims=True)
        p = jnp.exp2((s - m) * ATTN_EXP2_SCALE)
        den = jnp.sum(p, axis=-1, keepdims=True)
        out = jnp.dot(p.astype(BF16), v_ref[0, pl.ds(ks, 2 * ATTN_SUB), :],
                      preferred_element_type=F32) / den
        lse = m * scale + jnp.log(den)
        for r in range(Q_PER_KV):
            part = slice(r * ATTN_SUB, (r + 1) * ATTN_SUB)
            o_ref[0, rows, r * HEAD:(r + 1) * HEAD] = out[part].astype(o_ref.dtype)
            lse_ref[0, rows, r * HEAD:(r + 1) * HEAD] = jnp.broadcast_to(lse[part],
                                                                          (ATTN_SUB, HEAD))


def attn_group(q, k, v, g, batch, *, qb):
    w, dil = DIL_GROUPS[g]
    assert w == ATTN_SUB * dil
    sub = q.shape[0] // batch
    qb = min(qb, sub)
    qcols = Q_PER_KV * HEAD
    kern = functools.partial(_attn_group_kernel, qb=qb)
    const = lambda bi, r, kv, nb: (0, 0)
    q_map = lambda bi, r, kv, nb: (bi, nb, r * KV_PER_GROUP + kv)
    kv_map = lambda bi, r, kv, nb: (bi, 0, r * KV_PER_GROUP + kv)
    o, lse = pl.pallas_call(
        kern,
        grid=(batch, dil, KV_PER_GROUP, sub // qb),
        in_specs=[pl.BlockSpec((1, qb, qcols), q_map),
                  pl.BlockSpec((1, sub, HEAD), kv_map),
                  pl.BlockSpec((1, sub, HEAD), kv_map),
                  pl.BlockSpec((Q_PER_KV * ATTN_SUB, 2 * ATTN_SUB), const),
                  pl.BlockSpec((Q_PER_KV * ATTN_SUB, 2 * ATTN_SUB), const)],
        out_specs=[pl.BlockSpec((1, qb, qcols), q_map), pl.BlockSpec((1, qb, qcols), q_map)],
        out_shape=[jax.ShapeDtypeStruct((batch, sub, dil * ATTN_QG), BF16),
                   jax.ShapeDtypeStruct((batch, sub, dil * ATTN_QG), F32)],
        compiler_params=_cparams(("parallel", "parallel", "parallel", "arbitrary")),
        name="attn_group",
    )(q.reshape(batch, sub, -1), k.reshape(batch, sub, -1), v.reshape(batch, sub, -1),
      jnp.asarray(_band_bias(True)), jnp.asarray(_band_bias(False)))
    return o.reshape(batch * sub, -1), lse.reshape(batch * sub, -1)


def _combine_proj_kernel(o0_ref, o1_ref, o2_ref, l0_ref, l1_ref, l2_ref, w_ref, r_ref, out_ref,
                         osc_ref, lsc_ref):
    tm = out_ref.shape[0]
    ng = len(DIL_GROUPS)
    n_heads = ATTN_QG // HEAD
    for g, (o_ref, l_ref) in enumerate(zip((o0_ref, o1_ref, o2_ref), (l0_ref, l1_ref, l2_ref))):
        dil = DIL_GROUPS[g][1]
        for r in range(dil):
            rows = pl.ds(r, tm // dil, stride=dil) if dil > 1 else slice(None)
            for hd in range(n_heads):
                cols = slice((r * n_heads + hd) * HEAD, (r * n_heads + hd + 1) * HEAD)
                osc_ref[g * n_heads + hd, rows, :] = o_ref[:, cols].astype(F32)
                lsc_ref[g * n_heads + hd, rows, :] = l_ref[:, cols]
    heads = []
    for hd in range(n_heads):
        lses = [lsc_ref[g * n_heads + hd] for g in range(ng)]
        m = functools.reduce(jnp.maximum, lses)
        ws = [jnp.exp(x - m) for x in lses]
        num = functools.reduce(lambda a, b: a + b,
                               [ws[g] * osc_ref[g * n_heads + hd] for g in range(ng)])
        heads.append((num / functools.reduce(lambda a, b: a + b, ws)).astype(BF16))
    comb = jnp.concatenate(heads, axis=1)
    out_ref[...] = r_ref[...] + jnp.dot(comb, w_ref[...], preferred_element_type=F32)


def combine_proj(outs, lses, w, res, *, tm):
    t, n = res.shape
    k = w.shape[0]
    ng = len(DIL_GROUPS)
    packed = [pl.BlockSpec((tm // dil, dil * k), lambda i: (i, 0)) for _, dil in DIL_GROUPS]
    return pl.pallas_call(
        _combine_proj_kernel,
        grid=(t // tm,),
        in_specs=packed + packed + [pl.BlockSpec((k, n), lambda i: (0, 0)),
                                    pl.BlockSpec((tm, n), lambda i: (i, 0))],
        out_specs=pl.BlockSpec((tm, n), lambda i: (i, 0)),
        out_shape=jax.ShapeDtypeStruct((t, n), F32),
        scratch_shapes=[pltpu.VMEM((ng * k // HEAD, tm, HEAD), F32),
                        pltpu.VMEM((ng * k // HEAD, tm, HEAD), F32)],
        compiler_params=_cparams(("parallel",)),
        name="combine_proj",
    )(*outs, *lses, w, res)


def kernel(x, hgrn_norm, hgrn_w_in, hgrn_lb_logits, hgrn_out_norm, hgrn_w_out, kv_norm, w_kv,
           k_norm, attn_norm, w_q, q_norm, w_o, ffn_norm, peer_w_q, peer_sub_keys, peer_u, peer_v):
    b, s, d = x.shape
    t = b * s
    depth = ffn_norm.shape[0]
    n_a = hgrn_norm.shape[0]
    h = x.reshape(t, d)
    tm = min(1024, t)
    for layer in range(depth):
        if layer < n_a:
            o = hgrn_mixer(h.reshape(b, s, d), hgrn_norm[layer], hgrn_w_in[layer], hgrn_lb_logits,
                           hgrn_out_norm[layer], layer=layer, seq_block=min(1024, s))
            h = matmul_residual(o.reshape(t, d), hgrn_w_out[layer].astype(BF16), h, tm=tm)
        else:
            j = layer - n_a
            qkv = attn_proj(h, kv_norm, attn_norm[j], w_kv.astype(BF16), w_q[j].astype(BF16),
                            k_norm, q_norm[j], s, tm=min(512, s))
            if j == 0:
                kv_sh = [(kg, vg) for _, kg, vg in qkv]
            parts = [attn_group(qkv[g][0], kv_sh[g][0], kv_sh[g][1], g, b, qb=512)
                     for g in range(len(DIL_GROUPS))]
            h = combine_proj([o for o, _ in parts], [l for _, l in parts],
                             w_o[j].astype(BF16), h, tm=min(512, t))
        h = peer_layer(h, ffn_norm[layer], peer_w_q[layer], peer_sub_keys[layer],
                       peer_u, peer_v, layer)
    return h.reshape(b, s, d)
```

```python
import functools
import math

import numpy as np
import jax
import jax.numpy as jnp
from jax import lax
from jax.experimental import pallas as pl
from jax.experimental.pallas import tpu as pltpu

F32 = jnp.float32
BF16 = jnp.bfloat16
EPS = 1e-6

LANES = 128
HEAD = 128
HG_CHUNK = 128
HG_SAFE_EXPONENT = 80.0
DIL_GROUPS = ((128, 1), (512, 4), (2048, 16))
KV_PER_GROUP = 2
Q_PER_KV = 2
ROT_DIM = HEAD // 4
ROPE_THETA = 500000.0
Q_BLOCK = 128
PEER_HEADS = 8
PEER_TOPK = 16
N_KEYS = 128
VMEM_LIMIT = 56 * 1024 * 1024

NT_DIMS = (((1,), (1,)), ((), ()))


def _cparams(sem):
    return pltpu.CompilerParams(dimension_semantics=sem, vmem_limit_bytes=VMEM_LIMIT)


def _rms(x):
    return x * lax.rsqrt(jnp.mean(x * x, axis=-1, keepdims=True) + EPS)


def _sigmoid(x):
    return 1.0 / (1.0 + jnp.exp(-x))


def _norm_matmul_kernel(x_ref, g_ref, w_ref, o_ref, xn_ref):
    @pl.when(pl.program_id(1) == 0)
    def _():
        xn_ref[...] = (_rms(x_ref[...]) * g_ref[...]).astype(xn_ref.dtype)

    o_ref[...] = jnp.dot(xn_ref[...], w_ref[...],
                         preferred_element_type=F32).astype(o_ref.dtype)


def norm_matmul(x, gain, w, *, tm, tn, out_dtype=F32):
    t, d = x.shape
    n = w.shape[1]
    return pl.pallas_call(
        _norm_matmul_kernel,
        grid=(t // tm, n // tn),
        in_specs=[pl.BlockSpec((tm, d), lambda i, j: (i, 0)),
                  pl.BlockSpec((1, d), lambda i, j: (0, 0)),
                  pl.BlockSpec((d, tn), lambda i, j: (0, j))],
        out_specs=pl.BlockSpec((tm, tn), lambda i, j: (i, j)),
        out_shape=jax.ShapeDtypeStruct((t, n), out_dtype),
        scratch_shapes=[pltpu.VMEM((tm, d), BF16)],
        compiler_params=_cparams(("parallel", "arbitrary")),
        name="norm_matmul",
    )(x, gain.reshape(1, d), w)


def _matmul_res_kernel(a_ref, w_ref, r_ref, o_ref):
    o_ref[...] = r_ref[...] + jnp.dot(a_ref[...], w_ref[...], preferred_element_type=F32)


def matmul_residual(a, w, res, *, tm):
    t, k = a.shape
    n = w.shape[1]
    return pl.pallas_call(
        _matmul_res_kernel,
        grid=(t // tm,),
        in_specs=[pl.BlockSpec((tm, k), lambda i: (i, 0)),
                  pl.BlockSpec((k, n), lambda i: (0, 0)),
                  pl.BlockSpec((tm, n), lambda i: (i, 0))],
        out_specs=pl.BlockSpec((tm, n), lambda i: (i, 0)),
        out_shape=jax.ShapeDtypeStruct((t, n), F32),
        compiler_params=_cparams(("parallel",)),
        name="matmul_residual",
    )(a, w, res)


def _hgrn_levels(c):
    return int(math.log2(c))


def _hgrn_sum_matrix(c):
    nlev = _hgrn_levels(c)
    m = np.zeros(((nlev + 2) * c, c), np.float32)
    for l in range(nlev):
        h = c >> (l + 1)
        for r in range(c):
            mid = (r // (2 * h)) * 2 * h + h
            if r >= mid:
                m[l * c + r, mid:r + 1] = 1.0
            else:
                m[l * c + r, r + 1:mid] = 1.0
    for r in range(c):
        m[nlev * c + r, :r + 1] = 1.0
        m[(nlev + 1) * c + r, r + 1:] = 1.0
    return m


def _hgrn_kernel(x_ref, ng_ref, w_ref, lbl_ref, og_ref, m_ref, o_ref, xn_ref, y_ref, st_all_ref, *,
                 layer, chunk, n_chunks):
    c = chunk
    nlev = _hgrn_levels(c)
    head = pl.program_id(2)
    st_ref = st_all_ref.at[head]

    @pl.when(head == 0)
    def _():
        xn_ref[...] = (_rms(x_ref[0]) * ng_ref[...]).astype(xn_ref.dtype)

    @pl.when(pl.program_id(1) == 0)
    def _():
        st_ref[...] = jnp.zeros_like(st_ref)

    y_ref[...] = jnp.dot(xn_ref[...], w_ref[...], preferred_element_type=F32)

    lg = lbl_ref[...]
    e = jnp.exp(lg - jnp.max(lg, axis=0, keepdims=True))
    lb = jnp.sum(e[:layer + 1], axis=0, keepdims=True) / jnp.sum(e, axis=0, keepdims=True)
    og = og_ref[...]
    row = lax.broadcasted_iota(jnp.int32, (c, c), 0)
    col = lax.broadcasted_iota(jnp.int32, (c, c), 1)

    def body(ci, carry, *, mild):
        sl = pl.ds(pl.multiple_of(ci * c, c), c)
        q = y_ref[sl, 0:HEAD]
        fr = y_ref[sl, HEAD:2 * HEAD]
        v = y_ref[sl, 2 * HEAD:3 * HEAD]
        gt = y_ref[sl, 3 * HEAD:4 * HEAD]
        qa = q * _sigmoid(q)
        f = lb + (1.0 - lb) * _sigmoid(fr)
        k = 1.0 - f
        logf = jnp.log(f)
        hi = logf.astype(BF16)
        lo = (logf - hi.astype(F32)).astype(BF16)
        g2 = jnp.concatenate([hi, lo], axis=1)
        if mild:
            e2 = jnp.dot(m_ref[nlev * c:(nlev + 1) * c, :], g2, preferred_element_type=F32)
            b_incl = e2[:, :HEAD] + e2[:, HEAD:]
            dmid = b_incl - b_incl[c // 2 - 1:c // 2, :]
            p = lax.dot_general((qa * jnp.exp(dmid)).astype(BF16),
                                (k * jnp.exp(-dmid)).astype(BF16), NT_DIMS,
                                preferred_element_type=F32)
            s = jnp.where(row >= col, p, 0.0)
            rev = b_incl[c - 1:c, :] - b_incl
        else:
            e2 = jnp.dot(m_ref[...], g2, preferred_element_type=F32)
            ex = e2[:, :HEAD] + e2[:, HEAD:]
            s = jnp.where(row == col,
                          lax.dot_general(qa.astype(BF16), k.astype(BF16), NT_DIMS,
                                          preferred_element_type=F32),
                          0.0)
            for l in range(nlev):
                sh = nlev - 1 - l
                x = jnp.exp(ex[l * c:(l + 1) * c])
                p = lax.dot_general((qa * x).astype(BF16), (k * x).astype(BF16), NT_DIMS,
                                    preferred_element_type=F32)
                mask = (((row >> (sh + 1)) == (col >> (sh + 1)))
                        & (((row >> sh) & 1) == 1) & (((col >> sh) & 1) == 0))
                s = jnp.where(mask, p, s)
            b_incl = ex[nlev * c:(nlev + 1) * c]
            rev = ex[(nlev + 1) * c:(nlev + 2) * c]
        vb = v.astype(BF16)
        intra = jnp.dot(s.astype(BF16), vb, preferred_element_type=F32)
        st = st_ref[...]
        inter = lax.dot_general((qa * jnp.exp(b_incl)).astype(BF16), st.astype(BF16), NT_DIMS,
                                preferred_element_type=F32)
        o = inter + intra
        y = _rms(o) * og * (gt * _sigmoid(gt))
        o_ref[0, sl, :] = y.astype(o_ref.dtype)

        k2 = (k * jnp.exp(rev)).astype(BF16)
        upd = jnp.dot(v.T.astype(BF16), k2, preferred_element_type=F32)
        st_ref[...] = st * jnp.exp(b_incl[c - 1:c, :]) + upd
        return carry

    mild = jnp.min(lb) >= math.exp(-HG_SAFE_EXPONENT / (c // 2))

    @pl.when(mild)
    def _():
        lax.fori_loop(0, n_chunks, functools.partial(body, mild=True), 0, unroll=4)

    @pl.when(jnp.logical_not(mild))
    def _():
        lax.fori_loop(0, n_chunks, functools.partial(body, mild=False), 0, unroll=4)


def hgrn_mixer(x, norm_gain, w_in, lb_logits, out_gain, *, layer, seq_block):
    b, s, d = x.shape
    nh = d // HEAD
    c = min(HG_CHUNK, seq_block)
    msel = jnp.asarray(_hgrn_sum_matrix(c), BF16)
    nl = lb_logits.shape[0]
    w_heads = (w_in.astype(BF16).reshape(d, 4, nh, HEAD).transpose(0, 2, 1, 3)
               .reshape(d, nh * 4 * HEAD))
    kern = functools.partial(_hgrn_kernel, layer=layer, chunk=c, n_chunks=seq_block // c)
    return pl.pallas_call(
        kern,
        grid=(b, s // seq_block, nh),
        in_specs=[pl.BlockSpec((1, seq_block, d), lambda bi, si, hi: (bi, si, 0)),
                  pl.BlockSpec((1, d), lambda bi, si, hi: (0, 0)),
                  pl.BlockSpec((d, 4 * HEAD), lambda bi, si, hi: (0, hi)),
                  pl.BlockSpec((nl, HEAD), lambda bi, si, hi: (0, hi)),
                  pl.BlockSpec((1, HEAD), lambda bi, si, hi: (0, 0)),
                  pl.BlockSpec(msel.shape, lambda bi, si, hi: (0, 0))],
        out_specs=pl.BlockSpec((1, seq_block, HEAD), lambda bi, si, hi: (bi, si, hi)),
        out_shape=jax.ShapeDtypeStruct((b, s, d), BF16),
        scratch_shapes=[pltpu.VMEM((seq_block, d), BF16), pltpu.VMEM((seq_block, 4 * HEAD), F32),
                        pltpu.VMEM((nh, HEAD, HEAD), F32)],
        compiler_params=_cparams(("parallel", "arbitrary", "arbitrary")),
        name="hgrn_mixer",
    )(x, norm_gain.reshape(1, d), w_heads, lb_logits, out_gain.reshape(1, HEAD), msel)


def _peer_prep_kernel(h_ref, g_ref, wq_ref, keys_ref, xn_ref, s1_ref, s2_ref):
    xn = (_rms(h_ref[...]) * g_ref[...]).astype(BF16)
    xn_ref[...] = xn
    q = jnp.dot(xn, wq_ref[...], preferred_element_type=F32)
    k0 = keys_ref[0].astype(BF16)
    k1 = keys_ref[1].astype(BF16)
    for hh in range(PEER_HEADS):
        base = hh * 2 * HEAD
        q1 = q[:, base:base + HEAD].astype(BF16)
        q2 = q[:, base + HEAD:base + 2 * HEAD].astype(BF16)
        s1_ref[hh] = lax.dot_general(k0, q1, NT_DIMS, preferred_element_type=F32)
        s2_ref[hh] = lax.dot_general(k1, q2, NT_DIMS, preferred_element_type=F32)


def peer_prep(h, gain, wq, keys, *, tm):
    t, d = h.shape
    nq = wq.shape[1]
    return pl.pallas_call(
        _peer_prep_kernel,
        grid=(t // tm,),
        in_specs=[pl.BlockSpec((tm, d), lambda i: (i, 0)),
                  pl.BlockSpec((1, d), lambda i: (0, 0)),
                  pl.BlockSpec((d, nq), lambda i: (0, 0)),
                  pl.BlockSpec(keys.shape, lambda i: (0, 0, 0))],
        out_specs=[pl.BlockSpec((tm, d), lambda i: (i, 0)),
                   pl.BlockSpec((PEER_HEADS, N_KEYS, tm), lambda i: (0, 0, i)),
                   pl.BlockSpec((PEER_HEADS, N_KEYS, tm), lambda i: (0, 0, i))],
        out_shape=[jax.ShapeDtypeStruct((t, d), BF16),
                   jax.ShapeDtypeStruct((PEER_HEADS, N_KEYS, t), F32),
                   jax.ShapeDtypeStruct((PEER_HEADS, N_KEYS, t), F32)],
        compiler_params=_cparams(("parallel",)),
        name="peer_prep",
    )(h, gain.reshape(1, d), wq, keys)


def _oddeven_merge(lo, hi, r):
    step = r * 2
    if step < hi - lo:
        yield from _oddeven_merge(lo, hi, step)
        yield from _oddeven_merge(lo + r, hi, step)
        yield from [(i, i + r) for i in range(lo + r, hi - r, step)]
    else:
        yield (lo, lo + r)


def _oddeven_sort(lo, hi):
    if hi - lo >= 1:
        mid = lo + (hi - lo) // 2
        yield from _oddeven_sort(lo, mid)
        yield from _oddeven_sort(mid + 1, hi)
        yield from _oddeven_merge(lo, hi, 1)


SORT16 = tuple(_oddeven_sort(0, PEER_TOPK - 1))
BITONIC16 = tuple((i, i + h) for h in (8, 4, 2, 1) for i in range(PEER_TOPK) if i % (2 * h) < h)
SUBLANES = 8


def _compare_exchange(v, pairs):
    v = list(v)
    for i, j in pairs:
        v[i], v[j] = jnp.maximum(v[i], v[j]), jnp.minimum(v[i], v[j])
    return v


def _merge_sublane_lists(v):
    n = len(v)
    for shift in (4, 2, 1):
        w = [pltpu.roll(x, shift, 0) for x in v]
        v = _compare_exchange([jnp.maximum(v[k], w[n - 1 - k]) for k in range(n)], BITONIC16)
    return v


def _top16(x):
    slabs = [x[SUBLANES * k:SUBLANES * (k + 1), :] for k in range(PEER_TOPK)]
    return _merge_sublane_lists(_compare_exchange(slabs, SORT16))


def _peer_topk_kernel(s1_ref, s2_ref, cnt_ref, ar_ref, rk_ref, bt_ref):
    k = PEER_TOPK
    lanes = s1_ref.shape[2]
    sub = lax.broadcasted_iota(jnp.int32, (SUBLANES, lanes), 0)

    def head(hh, carry):
        s1 = s1_ref[hh]
        s2 = s2_ref[hh]
        va = _top16(s1)
        vb = _top16(s2)
        bd = vb[0]
        for p in range(1, SUBLANES):
            bd = jnp.where(sub == p, vb[p], bd)
        cand = []
        for i in range(k):
            c = va[i] + bd
            lim = k // (i + 1)
            cand.append(c if lim >= SUBLANES else jnp.where(sub < lim, c, -jnp.inf))
        tail = [va[0] + vb[SUBLANES + q] for q in range(k - SUBLANES)]
        top16 = _merge_sublane_lists(cand)
        tau = top16[k - SUBLANES - 1]
        for q in range(k - SUBLANES):
            tau = jnp.minimum(tau, jnp.maximum(top16[k - 1 - q], tail[q]))
        top = va[0] + vb[0]
        zc = jnp.zeros((SUBLANES, lanes), F32)
        for c in cand:
            zc = zc + jnp.where(c >= tau, jnp.exp(c - top), 0.0)
        zt = jnp.zeros((SUBLANES, lanes), F32)
        for c in tail:
            zt = zt + jnp.where(c >= tau, jnp.exp(c - top), 0.0)
        z = jnp.sum(zc, axis=0, keepdims=True) + zt[0:1, :]
        tau_row = tau[0:1, :]
        b_rows = [x[0:1, :] for x in vb]
        cnt = jnp.zeros(s1.shape, F32)
        rank2 = jnp.zeros(s2.shape, F32)
        for j in range(k):
            cnt = cnt + jnp.where(s1 + b_rows[j] >= tau_row, 1.0, 0.0)
            rank2 = rank2 + jnp.where(b_rows[j] > s2, 1.0, 0.0)
        cnt_ref[hh] = cnt
        ar_ref[hh] = jnp.exp(s1 - va[0][0:1, :]) / z
        bt = jnp.exp(s2 - b_rows[0])
        for lb in range(lanes // LANES):
            ln = slice(lb * LANES, (lb + 1) * LANES)
            rk_ref[hh, lb] = pltpu.bitcast(rank2[:, ln].astype(BF16), jnp.uint32)
            bt_ref[hh, lb] = pltpu.bitcast(bt[:, ln].astype(BF16), jnp.uint32)
        return carry

    lax.fori_loop(0, PEER_HEADS, head, 0)


def peer_topk(s1t, s2t, *, tt):
    nh, nk, t = s1t.shape
    spec = pl.BlockSpec((nh, nk, tt), lambda i: (0, 0, i))
    shp = jax.ShapeDtypeStruct((nh, nk, t), F32)
    tspec = pl.BlockSpec((nh, tt // LANES, nk // 2, LANES), lambda i: (0, i, 0, 0))
    tshp = jax.ShapeDtypeStruct((nh, t // LANES, nk // 2, LANES), jnp.uint32)
    return pl.pallas_call(
        _peer_topk_kernel,
        grid=(t // tt,),
        in_specs=[spec, spec],
        out_specs=[spec, spec, tspec, tspec],
        out_shape=[shp, shp, tshp, tshp],
        compiler_params=_cparams(("parallel",)),
        name="peer_topk",
    )(s1t, s2t)


GELU_C0 = math.sqrt(2.0 / math.pi)
GELU_C1 = GELU_C0 * 0.044715


def _gelu(z):
    return 0.5 * z * (1.0 + jnp.tanh(z * (GELU_C0 + GELU_C1 * (z * z))))


def _peer_main_kernel(h_ref, xn_ref, u_ref, vt_ref, cnt_ref, ar_ref, rk_ref, bt_ref, o_ref,
                      acc_ref, zt_ref, a_ref, *, n_blocks, blocks_per_tile, lane_blocks_per_iter):
    g_step = pl.program_id(0)
    et, tt = zt_ref.shape
    n_e1 = et // N_KEYS
    jb = jnp.maximum(g_step - 1, 0) % blocks_per_tile

    @pl.when(g_step == 0)
    def _():
        a_ref[...] = jnp.zeros_like(a_ref)

    @pl.when(jb == 0)
    def _():
        acc_ref[...] = jnp.zeros_like(acc_ref)

    acc_ref[...] += jnp.dot(vt_ref[...], a_ref[...], preferred_element_type=F32)
    zt_ref[...] = lax.dot_general(u_ref[...].astype(BF16), xn_ref[...], NT_DIMS,
                                  preferred_element_type=F32)

    @pl.when((g_step >= 1) & (jb == blocks_per_tile - 1))
    def _():
        o_ref[...] = h_ref[...] + acc_ref[...].T

    def lane_group(li, carry):
        for sub in range(lane_blocks_per_iter):
            lb = li * lane_blocks_per_iter + sub
            ln = pl.ds(pl.multiple_of(lb * LANES, LANES), LANES)
            for c in range(n_e1):
                rows = slice(c * N_KEYS, (c + 1) * N_KEYS)
                g = jnp.zeros((N_KEYS, LANES), BF16)
                for hh in range(PEER_HEADS):
                    cnt = cnt_ref[hh, c:c + 1, ln].astype(BF16)
                    ar = ar_ref[hh, c:c + 1, ln].astype(BF16)
                    rk = pltpu.bitcast(rk_ref[hh, lb], BF16)
                    bt = pltpu.bitcast(bt_ref[hh, lb], BF16)
                    g = g + jnp.where(rk < cnt, bt * ar, 0)
                a_ref[rows, ln] = _gelu(zt_ref[rows, ln].astype(BF16)) * g
        return carry

    lax.fori_loop(0, tt // (LANES * lane_blocks_per_iter), lane_group, 0)


def _transpose_cast_kernel(x_ref, o_ref):
    o_ref[...] = x_ref[...].T.astype(o_ref.dtype)


def transpose_cast(stack, layer, *, rows):
    _, e, d = stack.shape
    return pl.pallas_call(
        _transpose_cast_kernel,
        grid=(e // rows,),
        in_specs=[pl.BlockSpec((None, rows, d), lambda j: (layer, j, 0))],
        out_specs=pl.BlockSpec((d, rows), lambda j: (0, j)),
        out_shape=jax.ShapeDtypeStruct((d, e), BF16),
        compiler_params=_cparams(("parallel",)),
        name="transpose_cast",
    )(stack)


def peer_main(h, xn, u_stack, layer, vt, cnt, ar, rank_tiles, bt_tiles, *, tt, et,
              lane_blocks_per_iter):
    t, d = h.shape
    ne = u_stack.shape[1]
    n_e1 = et // N_KEYS
    bpt = ne // et
    n_blocks = (t // tt) * bpt
    kern = functools.partial(_peer_main_kernel, n_blocks=n_blocks, blocks_per_tile=bpt,
                             lane_blocks_per_iter=lane_blocks_per_iter)

    def cur(g):
        return jnp.minimum(g, n_blocks - 1)

    def prev(g):
        return jnp.maximum(g - 1, 0)

    row_spec = pl.BlockSpec((PEER_HEADS, n_e1, tt), lambda g: (0, cur(g) % bpt, cur(g) // bpt))
    tok_spec = pl.BlockSpec((PEER_HEADS, tt // LANES, N_KEYS // 2, LANES),
                            lambda g: (0, cur(g) // bpt, 0, 0))
    return pl.pallas_call(
        kern,
        grid=(n_blocks + 1,),
        in_specs=[pl.BlockSpec((tt, d), lambda g: (prev(g) // bpt, 0)),
                  pl.BlockSpec((tt, d), lambda g: (cur(g) // bpt, 0)),
                  pl.BlockSpec((None, et, d), lambda g: (layer, cur(g) % bpt, 0)),
                  pl.BlockSpec((d, et), lambda g: (0, prev(g) % bpt)),
                  row_spec, row_spec, tok_spec, tok_spec],
        out_specs=pl.BlockSpec((tt, d), lambda g: (prev(g) // bpt, 0)),
        out_shape=jax.ShapeDtypeStruct((t, d), F32),
        scratch_shapes=[pltpu.VMEM((d, tt), F32), pltpu.VMEM((et, tt), F32),
                        pltpu.VMEM((et, tt), BF16)],
        compiler_params=_cparams(("arbitrary",)),
        name="peer_main",
    )(h, xn, u_stack, vt, cnt, ar, rank_tiles, bt_tiles)


def peer_layer(h, gain, wq, keys, u_stack, v_stack, layer):
    t = h.shape[0]
    xn, s1t, s2t = peer_prep(h, gain, wq.astype(BF16), keys, tm=min(1024, t))
    cnt, ar, rank_tiles, bt_tiles = peer_topk(s1t, s2t, tt=min(256, t))
    vt = transpose_cast(v_stack, layer, rows=512)
    return peer_main(h, xn, u_stack, layer, vt, cnt, ar, rank_tiles, bt_tiles,
                     tt=min(512, t), et=2048, lane_blocks_per_iter=1)


def _rope_tables(s):
    half = ROT_DIM // 2
    inv = ROPE_THETA ** (-jnp.arange(half, dtype=F32) * 2.0 / ROT_DIM)
    ang = jnp.arange(s).astype(F32)[:, None] * inv[None, :]
    cos, sin = jnp.cos(ang), jnp.sin(ang)
    pad = HEAD - ROT_DIM
    ones = jnp.ones((s, pad), F32)
    zeros = jnp.zeros((s, pad), F32)
    zh = jnp.zeros((s, half), F32)
    ct = jnp.concatenate([cos, cos, ones], axis=1)
    at = jnp.concatenate([-sin, zh, zeros], axis=1)
    bt = jnp.concatenate([zh, sin, zeros], axis=1)
    return ct, at, bt


ATTN_QG = KV_PER_GROUP * Q_PER_KV * HEAD
ATTN_KG = KV_PER_GROUP * HEAD


def _attn_proj_kernel(h_ref, kvg_ref, ag_ref, wkv_ref, wq_ref, kn_ref, qn_ref,
                      ct_ref, at_ref, bt_ref, *refs):
    out_refs, (qs_ref, kvs_ref) = refs[:-2], refs[-2:]
    y = _rms(h_ref[...])
    kv = jnp.dot((y * kvg_ref[...]).astype(BF16), wkv_ref[...], preferred_element_type=F32)
    qq = jnp.dot((y * ag_ref[...]).astype(BF16), wq_ref[...], preferred_element_type=F32)
    ct, at, bt = ct_ref[...], at_ref[...], bt_ref[...]
    half = ROT_DIM // 2

    def head_norm_rope(x, gain):
        n = _rms(x) * gain
        return n * ct + pltpu.roll(n, HEAD - half, 1) * at + pltpu.roll(n, half, 1) * bt

    n_kh = kv.shape[1] // (2 * HEAD)
    for hd in range(n_kh):
        cs = slice(hd * HEAD, (hd + 1) * HEAD)
        kvs_ref[hd] = head_norm_rope(kv[:, cs], kn_ref[...])
        kvs_ref[n_kh + hd] = kv[:, n_kh * HEAD + hd * HEAD:n_kh * HEAD + (hd + 1) * HEAD]
    for hd in range(qq.shape[1] // HEAD):
        qs_ref[hd] = head_norm_rope(qq[:, hd * HEAD:(hd + 1) * HEAD], qn_ref[...])

    tm = qs_ref.shape[1]
    q_heads, k_heads = ATTN_QG // HEAD, ATTN_KG // HEAD
    for g, (_, dil) in enumerate(DIL_GROUPS):
        q_ref, k_ref, v_ref = out_refs[3 * g:3 * g + 3]
        for r in range(dil):
            rows = pl.ds(r, tm // dil, stride=dil) if dil > 1 else slice(None)
            for hd in range(q_heads):
                q_ref[:, (r * q_heads + hd) * HEAD:(r * q_heads + hd + 1) * HEAD] = (
                    qs_ref[g * q_heads + hd, rows, :].astype(q_ref.dtype))
            for hd in range(k_heads):
                cols = slice((r * k_heads + hd) * HEAD, (r * k_heads + hd + 1) * HEAD)
                k_ref[:, cols] = kvs_ref[g * k_heads + hd, rows, :].astype(k_ref.dtype)
                v_ref[:, cols] = kvs_ref[n_kh + g * k_heads + hd, rows, :].astype(v_ref.dtype)


def attn_proj(h, kv_gain, a_gain, wkv, wq, k_norm, q_norm, seq, *, tm):
    t, d = h.shape
    ct, at, bt = _rope_tables(seq)
    ns = seq // tm
    tab = pl.BlockSpec((tm, HEAD), lambda i: (i % ns, 0))
    vec = lambda n: pl.BlockSpec((1, n), lambda i: (0, 0))
    out_specs, out_shape = [], []
    for _, dil in DIL_GROUPS:
        for width in (ATTN_QG, ATTN_KG, ATTN_KG):
            out_specs.append(pl.BlockSpec((tm // dil, dil * width), lambda i: (i, 0)))
            out_shape.append(jax.ShapeDtypeStruct((t // dil, dil * width), BF16))
    outs = pl.pallas_call(
        _attn_proj_kernel,
        grid=(t // tm,),
        in_specs=[pl.BlockSpec((tm, d), lambda i: (i, 0)), vec(d), vec(d),
                  pl.BlockSpec(wkv.shape, lambda i: (0, 0)),
                  pl.BlockSpec(wq.shape, lambda i: (0, 0)),
                  vec(HEAD), vec(HEAD), tab, tab, tab],
        out_specs=out_specs,
        out_shape=out_shape,
        scratch_shapes=[pltpu.VMEM((wq.shape[1] // HEAD, tm, HEAD), F32),
                        pltpu.VMEM((wkv.shape[1] // HEAD, tm, HEAD), F32)],
        compiler_params=_cparams(("parallel",)),
        name="attn_proj",
    )(h, kv_gain.reshape(1, d), a_gain.reshape(1, d), wkv, wq,
      k_norm.reshape(1, HEAD), q_norm.reshape(1, HEAD), ct, at, bt)
    return [tuple(outs[3 * g:3 * g + 3]) for g in range(len(DIL_GROUPS))]


ATTN_EXP2_SCALE = HEAD ** -0.5 * math.log2(math.e)
MAX_WINDOW = max(w for w, _ in DIL_GROUPS)


def _attn_window_bias(w, dil):
    rows = Q_PER_KV * Q_BLOCK
    tq = w + (np.arange(rows)[:, None] % Q_BLOCK)
    diff = tq - np.arange(w + Q_BLOCK)[None, :]
    valid = (diff >= 0) & (diff <= w) & (diff % dil == 0)
    return np.where(valid, 0.0, -np.inf).astype(np.float32)


def _attn_kernel(q0_ref, q1_ref, q2_ref, k0_ref, k1_ref, k2_ref, v0_ref, v1_ref, v2_ref,
                 b0_ref, b1_ref, b2_ref, o_ref):
    qb = pl.program_id(2)
    t0 = qb * Q_BLOCK
    rows = Q_PER_KV * Q_BLOCK
    groups = list(zip(DIL_GROUPS, (q0_ref, q1_ref, q2_ref), (k0_ref, k1_ref, k2_ref),
                      (v0_ref, v1_ref, v2_ref), (b0_ref, b1_ref, b2_ref)))

    def raw_scores(q_ref, k_ref, ks, span):
        q2 = jnp.concatenate([q_ref[0, :, r * HEAD:(r + 1) * HEAD] for r in range(Q_PER_KV)],
                             axis=0)
        return lax.dot_general(q2, k_ref[0, pl.ds(ks, span), :], NT_DIMS,
                               preferred_element_type=F32)

    def finish(scores, starts):
        m = functools.reduce(jnp.maximum, [jnp.max(s, axis=-1, keepdims=True) for s in scores])
        den = jnp.zeros((rows, 1), F32)
        out = jnp.zeros((rows, HEAD), F32)
        for s, ks, ((w, _), _, _, v_ref, _) in zip(scores, starts, groups):
            p = jnp.exp2((s - m) * ATTN_EXP2_SCALE)
            den = den + jnp.sum(p, axis=-1, keepdims=True)
            out = out + jnp.dot(p.astype(BF16), v_ref[0, pl.ds(ks, w + Q_BLOCK), :],
                                preferred_element_type=F32)
        out = out / den
        for r in range(Q_PER_KV):
            o_ref[0, :, r * HEAD:(r + 1) * HEAD] = (
                out[r * Q_BLOCK:(r + 1) * Q_BLOCK].astype(o_ref.dtype))

    @pl.when(t0 >= MAX_WINDOW)
    def _():
        scores, starts = [], []
        for (w, dil), q_ref, k_ref, _, b_ref in groups:
            ks = pl.multiple_of(t0 - w, Q_BLOCK)
            scores.append(raw_scores(q_ref, k_ref, ks, w + Q_BLOCK) + b_ref[...])
            starts.append(ks)
        finish(scores, starts)

    @pl.when(t0 < MAX_WINDOW)
    def _():
        scores, starts = [], []
        for (w, dil), q_ref, k_ref, _, _ in groups:
            span = w + Q_BLOCK
            ks = pl.multiple_of(jnp.maximum(t0 - w, 0), Q_BLOCK)
            tq = t0 + (lax.broadcasted_iota(jnp.int32, (rows, span), 0) & (Q_BLOCK - 1))
            diff = tq - (ks + lax.broadcasted_iota(jnp.int32, (rows, span), 1))
            valid = (diff >= 0) & (diff <= w) & ((diff & (dil - 1)) == 0)
            scores.append(jnp.where(valid, raw_scores(q_ref, k_ref, ks, span), -jnp.inf))
            starts.append(ks)
        finish(scores, starts)


def dilated_attention(q, k, v):
    b, s, _ = q.shape
    qcols = Q_PER_KV * HEAD

    def qspec(g):
        return pl.BlockSpec((1, Q_BLOCK, qcols), lambda bi, kv, qb, g=g: (bi, qb, g * KV_PER_GROUP + kv))

    def kvspec(g):
        return pl.BlockSpec((1, s, HEAD), lambda bi, kv, qb, g=g: (bi, 0, g * KV_PER_GROUP + kv))

    ng = len(DIL_GROUPS)
    biases = [jnp.asarray(_attn_window_bias(w, dil)) for w, dil in DIL_GROUPS]
    return pl.pallas_call(
        _attn_kernel,
        grid=(b, KV_PER_GROUP, s // Q_BLOCK),
        in_specs=([qspec(g) for g in range(ng)] + [kvspec(g) for g in range(ng)] * 2
                  + [pl.BlockSpec(bias.shape, lambda bi, kv, qb: (0, 0)) for bias in biases]),
        out_specs=pl.BlockSpec((1, Q_BLOCK, qcols), lambda bi, kv, qb: (bi, qb, kv)),
        out_shape=jax.ShapeDtypeStruct((b, s, KV_PER_GROUP * qcols), BF16),
        compiler_params=_cparams(("parallel", "parallel", "arbitrary")),
        name="dilated_attention",
    )(q, q, q, k, k, k, v, v, v, *biases)


ATTN_SUB = 128


def _band_bias(first):
    rq = np.arange(Q_PER_KV * ATTN_SUB)[:, None] % ATTN_SUB
    ck = np.arange(2 * ATTN_SUB)[None, :]
    diff = rq - ck if first else rq + ATTN_SUB - ck
    return np.where((diff >= 0) & (diff <= ATTN_SUB), 0.0, -np.inf).astype(np.float32)


def _attn_group_kernel(q_ref, k_ref, v_ref, bfirst_ref, bband_ref, o_ref, lse_ref, *, qb):
    nb = pl.program_id(3)
    scale = HEAD ** -0.5
    for sb in range(qb // ATTN_SUB):
        rows = slice(sb * ATTN_SUB, (sb + 1) * ATTN_SUB)
        n0 = nb * qb + sb * ATTN_SUB
        ks = pl.multiple_of(jnp.maximum(n0 - ATTN_SUB, 0), ATTN_SUB)
        q2 = jnp.concatenate([q_ref[0, rows, r * HEAD:(r + 1) * HEAD] for r in range(Q_PER_KV)],
                             axis=0)
        s = lax.dot_general(q2, k_ref[0, pl.ds(ks, 2 * ATTN_SUB), :], NT_DIMS,
                            preferred_element_type=F32)
        s = s + jnp.where(n0 == 0, bfirst_ref[...], bband_ref[...])
        m = jnp.max(s, axis=-1, keepdims=True)
        p = jnp.exp2((s - m) * ATTN_EXP2_SCALE)
        den = jnp.sum(p, axis=-1, keepdims=True)
        out = jnp.dot(p.astype(BF16), v_ref[0, pl.ds(ks, 2 * ATTN_SUB), :],
                      preferred_element_type=F32) / den
        lse = m * scale + jnp.log(den)
        for r in range(Q_PER_KV):
            part = slice(r * ATTN_SUB, (r + 1) * ATTN_SUB)
            o_ref[0, rows, r * HEAD:(r + 1) * HEAD] = out[part].astype(o_ref.dtype)
            lse_ref[0, rows, r * HEAD:(r + 1) * HEAD] = jnp.broadcast_to(lse[part],
                                                                          (ATTN_SUB, HEAD))


def attn_group(q, k, v, g, batch, *, qb):
    w, dil = DIL_GROUPS[g]
    assert w == ATTN_SUB * dil
    sub = q.shape[0] // batch
    qb = min(qb, sub)
    qcols = Q_PER_KV * HEAD
    kern = functools.partial(_attn_group_kernel, qb=qb)
    const = lambda bi, r, kv, nb: (0, 0)
    q_map = lambda bi, r, kv, nb: (bi, nb, r * KV_PER_GROUP + kv)
    kv_map = lambda bi, r, kv, nb: (bi, 0, r * KV_PER_GROUP + kv)
    o, lse = pl.pallas_call(
        kern,
        grid=(batch, dil, KV_PER_GROUP, sub // qb),
        in_specs=[pl.BlockSpec((1, qb, qcols), q_map),
                  pl.BlockSpec((1, sub, HEAD), kv_map),
                  pl.BlockSpec((1, sub, HEAD), kv_map),
                  pl.BlockSpec((Q_PER_KV * ATTN_SUB, 2 * ATTN_SUB), const),
                  pl.BlockSpec((Q_PER_KV * ATTN_SUB, 2 * ATTN_SUB), const)],
        out_specs=[pl.BlockSpec((1, qb, qcols), q_map), pl.BlockSpec((1, qb, qcols), q_map)],
        out_shape=[jax.ShapeDtypeStruct((batch, sub, dil * ATTN_QG), BF16),
                   jax.ShapeDtypeStruct((batch, sub, dil * ATTN_QG), F32)],
        compiler_params=_cparams(("parallel", "parallel", "parallel", "arbitrary")),
        name="attn_group",
    )(q.reshape(batch, sub, -1), k.reshape(batch, sub, -1), v.reshape(batch, sub, -1),
      jnp.asarray(_band_bias(True)), jnp.asarray(_band_bias(False)))
    return o.reshape(batch * sub, -1), lse.reshape(batch * sub, -1)


def _combine_proj_kernel(o0_ref, o1_ref, o2_ref, l0_ref, l1_ref, l2_ref, w_ref, r_ref, out_ref,
                         osc_ref, lsc_ref):
    tm = out_ref.shape[0]
    ng = len(DIL_GROUPS)
    n_heads = ATTN_QG // HEAD
    for g, (o_ref, l_ref) in enumerate(zip((o0_ref, o1_ref, o2_ref), (l0_ref, l1_ref, l2_ref))):
        dil = DIL_GROUPS[g][1]
        for r in range(dil):
            rows = pl.ds(r, tm // dil, stride=dil) if dil > 1 else slice(None)
            for hd in range(n_heads):
                cols = slice((r * n_heads + hd) * HEAD, (r * n_heads + hd + 1) * HEAD)
                osc_ref[g * n_heads + hd, rows, :] = o_ref[:, cols].astype(F32)
                lsc_ref[g * n_heads + hd, rows, :] = l_ref[:, cols]
    heads = []
    for hd in range(n_heads):
        lses = [lsc_ref[g * n_heads + hd] for g in range(ng)]
        m = functools.reduce(jnp.maximum, lses)
        ws = [jnp.exp(x - m) for x in lses]
        num = functools.reduce(lambda a, b: a + b,
                               [ws[g] * osc_ref[g * n_heads + hd] for g in range(ng)])
        heads.append((num / functools.reduce(lambda a, b: a + b, ws)).astype(BF16))
    comb = jnp.concatenate(heads, axis=1)
    out_ref[...] = r_ref[...] + jnp.dot(comb, w_ref[...], preferred_element_type=F32)


def combine_proj(outs, lses, w, res, *, tm):
    t, n = res.shape
    k = w.shape[0]
    ng = len(DIL_GROUPS)
    packed = [pl.BlockSpec((tm // dil, dil * k), lambda i: (i, 0)) for _, dil in DIL_GROUPS]
    return pl.pallas_call(
        _combine_proj_kernel,
        grid=(t // tm,),
        in_specs=packed + packed + [pl.BlockSpec((k, n), lambda i: (0, 0)),
                                    pl.BlockSpec((tm, n), lambda i: (i, 0))],
        out_specs=pl.BlockSpec((tm, n), lambda i: (i, 0)),
        out_shape=jax.ShapeDtypeStruct((t, n), F32),
        scratch_shapes=[pltpu.VMEM((ng * k // HEAD, tm, HEAD), F32),
                        pltpu.VMEM((ng * k // HEAD, tm, HEAD), F32)],
        compiler_params=_cparams(("parallel",)),
        name="combine_proj",
    )(*outs, *lses, w, res)


def kernel(x, hgrn_norm, hgrn_w_in, hgrn_lb_logits, hgrn_out_norm, hgrn_w_out, kv_norm, w_kv,
           k_norm, attn_norm, w_q, q_norm, w_o, ffn_norm, peer_w_q, peer_sub_keys, peer_u, peer_v):
    b, s, d = x.shape
    t = b * s
    depth = ffn_norm.shape[0]
    n_a = hgrn_norm.shape[0]
    h = x.reshape(t, d)
    tm = min(1024, t)
    for layer in range(depth):
        if layer < n_a:
            o = hgrn_mixer(h.reshape(b, s, d), hgrn_norm[layer], hgrn_w_in[layer], hgrn_lb_logits,
                           hgrn_out_norm[layer], layer=layer, seq_block=min(2048, s))
            h = matmul_residual(o.reshape(t, d), hgrn_w_out[layer].astype(BF16), h, tm=tm)
        else:
            j = layer - n_a
            qkv = attn_proj(h, kv_norm, attn_norm[j], w_kv.astype(BF16), w_q[j].astype(BF16),
                            k_norm, q_norm[j], s, tm=min(256, s))
            if j == 0:
                kv_sh = [(kg, vg) for _, kg, vg in qkv]
            parts = [attn_group(qkv[g][0], kv_sh[g][0], kv_sh[g][1], g, b, qb=512)
                     for g in range(len(DIL_GROUPS))]
            h = combine_proj([o for o, _ in parts], [l for _, l in parts],
                             w_o[j].astype(BF16), h, tm=min(512, t))
        h = peer_layer(h, ffn_norm[layer], peer_w_q[layer], peer_sub_keys[layer],
                       peer_u, peer_v, layer)
    return h.reshape(b, s, d)
```

```python
import functools
import math

import numpy as np
import jax
import jax.numpy as jnp
from jax import lax
from jax.experimental import pallas as pl
from jax.experimental.pallas import tpu as pltpu

F32 = jnp.float32
BF16 = jnp.bfloat16
EPS = 1e-6

LANES = 128
HEAD = 128
HG_CHUNK = 128
HG_SAFE_EXPONENT = 80.0
DIL_GROUPS = ((128, 1), (512, 4), (2048, 16))
KV_PER_GROUP = 2
Q_PER_KV = 2
ROT_DIM = HEAD // 4
ROPE_THETA = 500000.0
Q_BLOCK = 128
PEER_HEADS = 8
PEER_TOPK = 16
N_KEYS = 128
VMEM_LIMIT = 56 * 1024 * 1024

NT_DIMS = (((1,), (1,)), ((), ()))


def _cparams(sem):
    return pltpu.CompilerParams(dimension_semantics=sem, vmem_limit_bytes=VMEM_LIMIT)


def _rms(x):
    return x * lax.rsqrt(jnp.mean(x * x, axis=-1, keepdims=True) + EPS)


def _sigmoid(x):
    return 1.0 / (1.0 + jnp.exp(-x))


def _norm_matmul_kernel(x_ref, g_ref, w_ref, o_ref, xn_ref):
    @pl.when(pl.program_id(1) == 0)
    def _():
        xn_ref[...] = (_rms(x_ref[...]) * g_ref[...]).astype(xn_ref.dtype)

    o_ref[...] = jnp.dot(xn_ref[...], w_ref[...],
                         preferred_element_type=F32).astype(o_ref.dtype)


def norm_matmul(x, gain, w, *, tm, tn, out_dtype=F32):
    t, d = x.shape
    n = w.shape[1]
    return pl.pallas_call(
        _norm_matmul_kernel,
        grid=(t // tm, n // tn),
        in_specs=[pl.BlockSpec((tm, d), lambda i, j: (i, 0)),
                  pl.BlockSpec((1, d), lambda i, j: (0, 0)),
                  pl.BlockSpec((d, tn), lambda i, j: (0, j))],
        out_specs=pl.BlockSpec((tm, tn), lambda i, j: (i, j)),
        out_shape=jax.ShapeDtypeStruct((t, n), out_dtype),
        scratch_shapes=[pltpu.VMEM((tm, d), BF16)],
        compiler_params=_cparams(("parallel", "arbitrary")),
        name="norm_matmul",
    )(x, gain.reshape(1, d), w)


def _matmul_res_kernel(a_ref, w_ref, r_ref, o_ref):
    o_ref[...] = r_ref[...] + jnp.dot(a_ref[...], w_ref[...], preferred_element_type=F32)


def matmul_residual(a, w, res, *, tm):
    t, k = a.shape
    n = w.shape[1]
    return pl.pallas_call(
        _matmul_res_kernel,
        grid=(t // tm,),
        in_specs=[pl.BlockSpec((tm, k), lambda i: (i, 0)),
                  pl.BlockSpec((k, n), lambda i: (0, 0)),
                  pl.BlockSpec((tm, n), lambda i: (i, 0))],
        out_specs=pl.BlockSpec((tm, n), lambda i: (i, 0)),
        out_shape=jax.ShapeDtypeStruct((t, n), F32),
        compiler_params=_cparams(("parallel",)),
        name="matmul_residual",
    )(a, w, res)


def _hgrn_levels(c):
    return int(math.log2(c))


def _hgrn_sum_matrix(c):
    nlev = _hgrn_levels(c)
    m = np.zeros(((nlev + 2) * c, c), np.float32)
    for l in range(nlev):
        h = c >> (l + 1)
        for r in range(c):
            mid = (r // (2 * h)) * 2 * h + h
            if r >= mid:
                m[l * c + r, mid:r + 1] = 1.0
            else:
                m[l * c + r, r + 1:mid] = 1.0
    for r in range(c):
        m[nlev * c + r, :r + 1] = 1.0
        m[(nlev + 1) * c + r, r + 1:] = 1.0
    return m


def _hgrn_kernel(x_ref, ng_ref, w_ref, lbl_ref, og_ref, m_ref, o_ref, xn_ref, y_ref, st_all_ref, *,
                 layer, chunk, n_chunks):
    c = chunk
    nlev = _hgrn_levels(c)
    head = pl.program_id(2)
    st_ref = st_all_ref.at[head]

    @pl.when(head == 0)
    def _():
        xn_ref[...] = (_rms(x_ref[0]) * ng_ref[...]).astype(xn_ref.dtype)

    @pl.when(pl.program_id(1) == 0)
    def _():
        st_ref[...] = jnp.zeros_like(st_ref)

    y_ref[...] = jnp.dot(xn_ref[...], w_ref[...], preferred_element_type=F32)

    lg = lbl_ref[...]
    e = jnp.exp(lg - jnp.max(lg, axis=0, keepdims=True))
    lb = jnp.sum(e[:layer + 1], axis=0, keepdims=True) / jnp.sum(e, axis=0, keepdims=True)
    og = og_ref[...]
    row = lax.broadcasted_iota(jnp.int32, (c, c), 0)
    col = lax.broadcasted_iota(jnp.int32, (c, c), 1)

    def body(ci, carry, *, mild):
        sl = pl.ds(pl.multiple_of(ci * c, c), c)
        q = y_ref[sl, 0:HEAD]
        fr = y_ref[sl, HEAD:2 * HEAD]
        v = y_ref[sl, 2 * HEAD:3 * HEAD]
        gt = y_ref[sl, 3 * HEAD:4 * HEAD]
        qa = q * _sigmoid(q)
        f = lb + (1.0 - lb) * _sigmoid(fr)
        k = 1.0 - f
        logf = jnp.log(f)
        hi = logf.astype(BF16)
        lo = (logf - hi.astype(F32)).astype(BF16)
        g2 = jnp.concatenate([hi, lo], axis=1)
        if mild:
            e2 = jnp.dot(m_ref[nlev * c:(nlev + 1) * c, :], g2, preferred_element_type=F32)
            b_incl = e2[:, :HEAD] + e2[:, HEAD:]
            dmid = b_incl - b_incl[c // 2 - 1:c // 2, :]
            p = lax.dot_general((qa * jnp.exp(dmid)).astype(BF16),
                                (k * jnp.exp(-dmid)).astype(BF16), NT_DIMS,
                                preferred_element_type=F32)
            s = jnp.where(row >= col, p, 0.0)
            rev = b_incl[c - 1:c, :] - b_incl
        else:
            e2 = jnp.dot(m_ref[...], g2, preferred_element_type=F32)
            ex = e2[:, :HEAD] + e2[:, HEAD:]
            s = jnp.where(row == col,
                          lax.dot_general(qa.astype(BF16), k.astype(BF16), NT_DIMS,
                                          preferred_element_type=F32),
                          0.0)
            for l in range(nlev):
                sh = nlev - 1 - l
                x = jnp.exp(ex[l * c:(l + 1) * c])
                p = lax.dot_general((qa * x).astype(BF16), (k * x).astype(BF16), NT_DIMS,
                                    preferred_element_type=F32)
                mask = (((row >> (sh + 1)) == (col >> (sh + 1)))
                        & (((row >> sh) & 1) == 1) & (((col >> sh) & 1) == 0))
                s = jnp.where(mask, p, s)
            b_incl = ex[nlev * c:(nlev + 1) * c]
            rev = ex[(nlev + 1) * c:(nlev + 2) * c]
        vb = v.astype(BF16)
        intra = jnp.dot(s.astype(BF16), vb, preferred_element_type=F32)
        st = st_ref[...]
        inter = lax.dot_general((qa * jnp.exp(b_incl)).astype(BF16), st.astype(BF16), NT_DIMS,
                                preferred_element_type=F32)
        o = inter + intra
        y = _rms(o) * og * (gt * _sigmoid(gt))
        o_ref[0, sl, :] = y.astype(o_ref.dtype)

        k2 = (k * jnp.exp(rev)).astype(BF16)
        upd = jnp.dot(v.T.astype(BF16), k2, preferred_element_type=F32)
        st_ref[...] = st * jnp.exp(b_incl[c - 1:c, :]) + upd
        return carry

    mild = jnp.min(lb) >= math.exp(-HG_SAFE_EXPONENT / (c // 2))

    @pl.when(mild)
    def _():
        lax.fori_loop(0, n_chunks, functools.partial(body, mild=True), 0, unroll=4)

    @pl.when(jnp.logical_not(mild))
    def _():
        lax.fori_loop(0, n_chunks, functools.partial(body, mild=False), 0, unroll=4)


def hgrn_mixer(x, norm_gain, w_in, lb_logits, out_gain, *, layer, seq_block):
    b, s, d = x.shape
    nh = d // HEAD
    c = min(HG_CHUNK, seq_block)
    msel = jnp.asarray(_hgrn_sum_matrix(c), BF16)
    nl = lb_logits.shape[0]
    w_heads = (w_in.astype(BF16).reshape(d, 4, nh, HEAD).transpose(0, 2, 1, 3)
               .reshape(d, nh * 4 * HEAD))
    kern = functools.partial(_hgrn_kernel, layer=layer, chunk=c, n_chunks=seq_block // c)
    return pl.pallas_call(
        kern,
        grid=(b, s // seq_block, nh),
        in_specs=[pl.BlockSpec((1, seq_block, d), lambda bi, si, hi: (bi, si, 0)),
                  pl.BlockSpec((1, d), lambda bi, si, hi: (0, 0)),
                  pl.BlockSpec((d, 4 * HEAD), lambda bi, si, hi: (0, hi)),
                  pl.BlockSpec((nl, HEAD), lambda bi, si, hi: (0, hi)),
                  pl.BlockSpec((1, HEAD), lambda bi, si, hi: (0, 0)),
                  pl.BlockSpec(msel.shape, lambda bi, si, hi: (0, 0))],
        out_specs=pl.BlockSpec((1, seq_block, HEAD), lambda bi, si, hi: (bi, si, hi)),
        out_shape=jax.ShapeDtypeStruct((b, s, d), BF16),
        scratch_shapes=[pltpu.VMEM((seq_block, d), BF16), pltpu.VMEM((seq_block, 4 * HEAD), F32),
                        pltpu.VMEM((nh, HEAD, HEAD), F32)],
        compiler_params=_cparams(("parallel", "arbitrary", "arbitrary")),
        name="hgrn_mixer",
    )(x, norm_gain.reshape(1, d), w_heads, lb_logits, out_gain.reshape(1, HEAD), msel)


def _peer_prep_kernel(h_ref, g_ref, wq_ref, keys_ref, xn_ref, s1_ref, s2_ref):
    xn = (_rms(h_ref[...]) * g_ref[...]).astype(BF16)
    xn_ref[...] = xn
    q = jnp.dot(xn, wq_ref[...], preferred_element_type=F32)
    k0 = keys_ref[0].astype(BF16)
    k1 = keys_ref[1].astype(BF16)
    for hh in range(PEER_HEADS):
        base = hh * 2 * HEAD
        q1 = q[:, base:base + HEAD].astype(BF16)
        q2 = q[:, base + HEAD:base + 2 * HEAD].astype(BF16)
        s1_ref[hh] = lax.dot_general(k0, q1, NT_DIMS, preferred_element_type=F32)
        s2_ref[hh] = lax.dot_general(k1, q2, NT_DIMS, preferred_element_type=F32)


def peer_prep(h, gain, wq, keys, *, tm):
    t, d = h.shape
    nq = wq.shape[1]
    return pl.pallas_call(
        _peer_prep_kernel,
        grid=(t // tm,),
        in_specs=[pl.BlockSpec((tm, d), lambda i: (i, 0)),
                  pl.BlockSpec((1, d), lambda i: (0, 0)),
                  pl.BlockSpec((d, nq), lambda i: (0, 0)),
                  pl.BlockSpec(keys.shape, lambda i: (0, 0, 0))],
        out_specs=[pl.BlockSpec((tm, d), lambda i: (i, 0)),
                   pl.BlockSpec((PEER_HEADS, N_KEYS, tm), lambda i: (0, 0, i)),
                   pl.BlockSpec((PEER_HEADS, N_KEYS, tm), lambda i: (0, 0, i))],
        out_shape=[jax.ShapeDtypeStruct((t, d), BF16),
                   jax.ShapeDtypeStruct((PEER_HEADS, N_KEYS, t), F32),
                   jax.ShapeDtypeStruct((PEER_HEADS, N_KEYS, t), F32)],
        compiler_params=_cparams(("parallel",)),
        name="peer_prep",
    )(h, gain.reshape(1, d), wq, keys)


def _oddeven_merge(lo, hi, r):
    step = r * 2
    if step < hi - lo:
        yield from _oddeven_merge(lo, hi, step)
        yield from _oddeven_merge(lo + r, hi, step)
        yield from [(i, i + r) for i in range(lo + r, hi - r, step)]
    else:
        yield (lo, lo + r)


def _oddeven_sort(lo, hi):
    if hi - lo >= 1:
        mid = lo + (hi - lo) // 2
        yield from _oddeven_sort(lo, mid)
        yield from _oddeven_sort(mid + 1, hi)
        yield from _oddeven_merge(lo, hi, 1)


SORT16 = tuple(_oddeven_sort(0, PEER_TOPK - 1))
BITONIC16 = tuple((i, i + h) for h in (8, 4, 2, 1) for i in range(PEER_TOPK) if i % (2 * h) < h)
SUBLANES = 8


def _compare_exchange(v, pairs):
    v = list(v)
    for i, j in pairs:
        v[i], v[j] = jnp.maximum(v[i], v[j]), jnp.minimum(v[i], v[j])
    return v


def _merge_sublane_lists(v):
    n = len(v)
    for shift in (4, 2, 1):
        w = [pltpu.roll(x, shift, 0) for x in v]
        v = _compare_exchange([jnp.maximum(v[k], w[n - 1 - k]) for k in range(n)], BITONIC16)
    return v


def _top16(x):
    slabs = [x[SUBLANES * k:SUBLANES * (k + 1), :] for k in range(PEER_TOPK)]
    return _merge_sublane_lists(_compare_exchange(slabs, SORT16))


def _peer_topk_kernel(s1_ref, s2_ref, cnt_ref, ar_ref, rk_ref, bt_ref):
    k = PEER_TOPK
    lanes = s1_ref.shape[2]
    sub = lax.broadcasted_iota(jnp.int32, (SUBLANES, lanes), 0)

    def head(hh, carry):
        s1 = s1_ref[hh]
        s2 = s2_ref[hh]
        va = _top16(s1)
        vb = _top16(s2)
        bd = vb[0]
        for p in range(1, SUBLANES):
            bd = jnp.where(sub == p, vb[p], bd)
        cand = []
        for i in range(k):
            c = va[i] + bd
            lim = k // (i + 1)
            cand.append(c if lim >= SUBLANES else jnp.where(sub < lim, c, -jnp.inf))
        tail = [va[0] + vb[SUBLANES + q] for q in range(k - SUBLANES)]
        top16 = _merge_sublane_lists(cand)
        tau = top16[k - SUBLANES - 1]
        for q in range(k - SUBLANES):
            tau = jnp.minimum(tau, jnp.maximum(top16[k - 1 - q], tail[q]))
        top = va[0] + vb[0]
        zc = jnp.zeros((SUBLANES, lanes), F32)
        for c in cand:
            zc = zc + jnp.where(c >= tau, jnp.exp(c - top), 0.0)
        zt = jnp.zeros((SUBLANES, lanes), F32)
        for c in tail:
            zt = zt + jnp.where(c >= tau, jnp.exp(c - top), 0.0)
        z = jnp.sum(zc, axis=0, keepdims=True) + zt[0:1, :]
        tau_row = tau[0:1, :]
        b_rows = [x[0:1, :] for x in vb]
        cnt = jnp.zeros(s1.shape, F32)
        rank2 = jnp.zeros(s2.shape, F32)
        for j in range(k):
            cnt = jnp.where(s1 + b_rows[j] >= tau_row, float(j + 1), cnt)
            rank2 = jnp.where(b_rows[j] > s2, float(j + 1), rank2)
        cnt_ref[hh] = cnt
        ar_ref[hh] = jnp.exp(s1 - va[0][0:1, :]) / z
        bt = jnp.exp(s2 - b_rows[0])
        for lb in range(lanes // LANES):
            ln = slice(lb * LANES, (lb + 1) * LANES)
            rk_ref[hh, lb] = pltpu.bitcast(rank2[:, ln].astype(BF16), jnp.uint32)
            bt_ref[hh, lb] = pltpu.bitcast(bt[:, ln].astype(BF16), jnp.uint32)
        return carry

    lax.fori_loop(0, PEER_HEADS, head, 0)


def peer_topk(s1t, s2t, *, tt):
    nh, nk, t = s1t.shape
    spec = pl.BlockSpec((nh, nk, tt), lambda i: (0, 0, i))
    shp = jax.ShapeDtypeStruct((nh, nk, t), F32)
    tspec = pl.BlockSpec((nh, tt // LANES, nk // 2, LANES), lambda i: (0, i, 0, 0))
    tshp = jax.ShapeDtypeStruct((nh, t // LANES, nk // 2, LANES), jnp.uint32)
    return pl.pallas_call(
        _peer_topk_kernel,
        grid=(t // tt,),
        in_specs=[spec, spec],
        out_specs=[spec, spec, tspec, tspec],
        out_shape=[shp, shp, tshp, tshp],
        compiler_params=_cparams(("parallel",)),
        name="peer_topk",
    )(s1t, s2t)


GELU_C0 = math.sqrt(2.0 / math.pi)
GELU_C1 = GELU_C0 * 0.044715


def _gelu(z):
    return 0.5 * z * (1.0 + jnp.tanh(z * (GELU_C0 + GELU_C1 * (z * z))))


def _peer_main_kernel(h_ref, xn_ref, u_ref, vt_ref, cnt_ref, ar_ref, rk_ref, bt_ref, o_ref,
                      acc_ref, zt_ref, a_ref, *, n_blocks, blocks_per_tile, lane_blocks_per_iter):
    g_step = pl.program_id(0)
    et, tt = zt_ref.shape
    n_e1 = et // N_KEYS
    jb = jnp.maximum(g_step - 1, 0) % blocks_per_tile

    @pl.when(g_step == 0)
    def _():
        a_ref[...] = jnp.zeros_like(a_ref)

    @pl.when(jb == 0)
    def _():
        acc_ref[...] = jnp.zeros_like(acc_ref)

    acc_ref[...] += jnp.dot(vt_ref[...], a_ref[...], preferred_element_type=F32)
    zt_ref[...] = lax.dot_general(u_ref[...].astype(BF16), xn_ref[...], NT_DIMS,
                                  preferred_element_type=F32)

    @pl.when((g_step >= 1) & (jb == blocks_per_tile - 1))
    def _():
        o_ref[...] = h_ref[...] + acc_ref[...].T

    def lane_group(li, carry):
        for sub in range(lane_blocks_per_iter):
            lb = li * lane_blocks_per_iter + sub
            ln = pl.ds(pl.multiple_of(lb * LANES, LANES), LANES)
            for c in range(n_e1):
                rows = slice(c * N_KEYS, (c + 1) * N_KEYS)
                g = jnp.zeros((N_KEYS, LANES), BF16)
                for hh in range(PEER_HEADS):
                    cnt = cnt_ref[hh, c:c + 1, ln].astype(BF16)
                    ar = ar_ref[hh, c:c + 1, ln].astype(BF16)
                    rk = pltpu.bitcast(rk_ref[hh, lb], BF16)
                    bt = pltpu.bitcast(bt_ref[hh, lb], BF16)
                    g = g + jnp.where(rk < cnt, bt * ar, 0)
                a_ref[rows, ln] = _gelu(zt_ref[rows, ln].astype(BF16)) * g
        return carry

    lax.fori_loop(0, tt // (LANES * lane_blocks_per_iter), lane_group, 0)


def _transpose_cast_kernel(x_ref, o_ref):
    o_ref[...] = x_ref[...].T.astype(o_ref.dtype)


def transpose_cast(stack, layer, *, rows):
    _, e, d = stack.shape
    return pl.pallas_call(
        _transpose_cast_kernel,
        grid=(e // rows,),
        in_specs=[pl.BlockSpec((None, rows, d), lambda j: (layer, j, 0))],
        out_specs=pl.BlockSpec((d, rows), lambda j: (0, j)),
        out_shape=jax.ShapeDtypeStruct((d, e), BF16),
        compiler_params=_cparams(("parallel",)),
        name="transpose_cast",
    )(stack)


def peer_main(h, xn, u_stack, layer, vt, cnt, ar, rank_tiles, bt_tiles, *, tt, et,
              lane_blocks_per_iter):
    t, d = h.shape
    ne = u_stack.shape[1]
    n_e1 = et // N_KEYS
    bpt = ne // et
    n_blocks = (t // tt) * bpt
    kern = functools.partial(_peer_main_kernel, n_blocks=n_blocks, blocks_per_tile=bpt,
                             lane_blocks_per_iter=lane_blocks_per_iter)

    def cur(g):
        return jnp.minimum(g, n_blocks - 1)

    def prev(g):
        return jnp.maximum(g - 1, 0)

    row_spec = pl.BlockSpec((PEER_HEADS, n_e1, tt), lambda g: (0, cur(g) % bpt, cur(g) // bpt))
    tok_spec = pl.BlockSpec((PEER_HEADS, tt // LANES, N_KEYS // 2, LANES),
                            lambda g: (0, cur(g) // bpt, 0, 0))
    return pl.pallas_call(
        kern,
        grid=(n_blocks + 1,),
        in_specs=[pl.BlockSpec((tt, d), lambda g: (prev(g) // bpt, 0)),
                  pl.BlockSpec((tt, d), lambda g: (cur(g) // bpt, 0)),
                  pl.BlockSpec((None, et, d), lambda g: (layer, cur(g) % bpt, 0)),
                  pl.BlockSpec((d, et), lambda g: (0, prev(g) % bpt)),
                  row_spec, row_spec, tok_spec, tok_spec],
        out_specs=pl.BlockSpec((tt, d), lambda g: (prev(g) // bpt, 0)),
        out_shape=jax.ShapeDtypeStruct((t, d), F32),
        scratch_shapes=[pltpu.VMEM((d, tt), F32), pltpu.VMEM((et, tt), F32),
                        pltpu.VMEM((et, tt), BF16)],
        compiler_params=_cparams(("arbitrary",)),
        name="peer_main",
    )(h, xn, u_stack, vt, cnt, ar, rank_tiles, bt_tiles)


def peer_layer(h, gain, wq, keys, u_stack, v_stack, layer):
    t = h.shape[0]
    xn, s1t, s2t = peer_prep(h, gain, wq.astype(BF16), keys, tm=min(1024, t))
    cnt, ar, rank_tiles, bt_tiles = peer_topk(s1t, s2t, tt=min(256, t))
    vt = transpose_cast(v_stack, layer, rows=512)
    return peer_main(h, xn, u_stack, layer, vt, cnt, ar, rank_tiles, bt_tiles,
                     tt=min(512, t), et=2048, lane_blocks_per_iter=1)


def _rope_tables(s):
    half = ROT_DIM // 2
    inv = ROPE_THETA ** (-jnp.arange(half, dtype=F32) * 2.0 / ROT_DIM)
    ang = jnp.arange(s).astype(F32)[:, None] * inv[None, :]
    cos, sin = jnp.cos(ang), jnp.sin(ang)
    pad = HEAD - ROT_DIM
    ones = jnp.ones((s, pad), F32)
    zeros = jnp.zeros((s, pad), F32)
    zh = jnp.zeros((s, half), F32)
    ct = jnp.concatenate([cos, cos, ones], axis=1)
    at = jnp.concatenate([-sin, zh, zeros], axis=1)
    bt = jnp.concatenate([zh, sin, zeros], axis=1)
    return ct, at, bt


ATTN_QG = KV_PER_GROUP * Q_PER_KV * HEAD
ATTN_KG = KV_PER_GROUP * HEAD


def _attn_proj_kernel(h_ref, kvg_ref, ag_ref, wkv_ref, wq_ref, kn_ref, qn_ref,
                      ct_ref, at_ref, bt_ref, *refs):
    out_refs, (qs_ref, kvs_ref) = refs[:-2], refs[-2:]
    y = _rms(h_ref[...])
    kv = jnp.dot((y * kvg_ref[...]).astype(BF16), wkv_ref[...], preferred_element_type=F32)
    qq = jnp.dot((y * ag_ref[...]).astype(BF16), wq_ref[...], preferred_element_type=F32)
    ct, at, bt = ct_ref[...], at_ref[...], bt_ref[...]
    half = ROT_DIM // 2

    def head_norm_rope(x, gain):
        n = _rms(x) * gain
        return n * ct + pltpu.roll(n, HEAD - half, 1) * at + pltpu.roll(n, half, 1) * bt

    n_kh = kv.shape[1] // (2 * HEAD)
    for hd in range(n_kh):
        cs = slice(hd * HEAD, (hd + 1) * HEAD)
        kvs_ref[hd] = head_norm_rope(kv[:, cs], kn_ref[...])
        kvs_ref[n_kh + hd] = kv[:, n_kh * HEAD + hd * HEAD:n_kh * HEAD + (hd + 1) * HEAD]
    for hd in range(qq.shape[1] // HEAD):
        qs_ref[hd] = head_norm_rope(qq[:, hd * HEAD:(hd + 1) * HEAD], qn_ref[...])

    tm = qs_ref.shape[1]
    q_heads, k_heads = ATTN_QG // HEAD, ATTN_KG // HEAD
    for g, (_, dil) in enumerate(DIL_GROUPS):
        q_ref, k_ref, v_ref = out_refs[3 * g:3 * g + 3]
        for r in range(dil):
            rows = pl.ds(r, tm // dil, stride=dil) if dil > 1 else slice(None)
            for hd in range(q_heads):
                q_ref[:, (r * q_heads + hd) * HEAD:(r * q_heads + hd + 1) * HEAD] = (
                    qs_ref[g * q_heads + hd, rows, :].astype(q_ref.dtype))
            for hd in range(k_heads):
                cols = slice((r * k_heads + hd) * HEAD, (r * k_heads + hd + 1) * HEAD)
                k_ref[:, cols] = kvs_ref[g * k_heads + hd, rows, :].astype(k_ref.dtype)
                v_ref[:, cols] = kvs_ref[n_kh + g * k_heads + hd, rows, :].astype(v_ref.dtype)


def attn_proj(h, kv_gain, a_gain, wkv, wq, k_norm, q_norm, seq, *, tm):
    t, d = h.shape
    ct, at, bt = _rope_tables(seq)
    ns = seq // tm
    tab = pl.BlockSpec((tm, HEAD), lambda i: (i % ns, 0))
    vec = lambda n: pl.BlockSpec((1, n), lambda i: (0, 0))
    out_specs, out_shape = [], []
    for _, dil in DIL_GROUPS:
        for width in (ATTN_QG, ATTN_KG, ATTN_KG):
            out_specs.append(pl.BlockSpec((tm // dil, dil * width), lambda i: (i, 0)))
            out_shape.append(jax.ShapeDtypeStruct((t // dil, dil * width), BF16))
    outs = pl.pallas_call(
        _attn_proj_kernel,
        grid=(t // tm,),
        in_specs=[pl.BlockSpec((tm, d), lambda i: (i, 0)), vec(d), vec(d),
                  pl.BlockSpec(wkv.shape, lambda i: (0, 0)),
                  pl.BlockSpec(wq.shape, lambda i: (0, 0)),
                  vec(HEAD), vec(HEAD), tab, tab, tab],
        out_specs=out_specs,
        out_shape=out_shape,
        scratch_shapes=[pltpu.VMEM((wq.shape[1] // HEAD, tm, HEAD), F32),
                        pltpu.VMEM((wkv.shape[1] // HEAD, tm, HEAD), F32)],
        compiler_params=_cparams(("parallel",)),
        name="attn_proj",
    )(h, kv_gain.reshape(1, d), a_gain.reshape(1, d), wkv, wq,
      k_norm.reshape(1, HEAD), q_norm.reshape(1, HEAD), ct, at, bt)
    return [tuple(outs[3 * g:3 * g + 3]) for g in range(len(DIL_GROUPS))]


ATTN_EXP2_SCALE = HEAD ** -0.5 * math.log2(math.e)
MAX_WINDOW = max(w for w, _ in DIL_GROUPS)


def _attn_window_bias(w, dil):
    rows = Q_PER_KV * Q_BLOCK
    tq = w + (np.arange(rows)[:, None] % Q_BLOCK)
    diff = tq - np.arange(w + Q_BLOCK)[None, :]
    valid = (diff >= 0) & (diff <= w) & (diff % dil == 0)
    return np.where(valid, 0.0, -np.inf).astype(np.float32)


def _attn_kernel(q0_ref, q1_ref, q2_ref, k0_ref, k1_ref, k2_ref, v0_ref, v1_ref, v2_ref,
                 b0_ref, b1_ref, b2_ref, o_ref):
    qb = pl.program_id(2)
    t0 = qb * Q_BLOCK
    rows = Q_PER_KV * Q_BLOCK
    groups = list(zip(DIL_GROUPS, (q0_ref, q1_ref, q2_ref), (k0_ref, k1_ref, k2_ref),
                      (v0_ref, v1_ref, v2_ref), (b0_ref, b1_ref, b2_ref)))

    def raw_scores(q_ref, k_ref, ks, span):
        q2 = jnp.concatenate([q_ref[0, :, r * HEAD:(r + 1) * HEAD] for r in range(Q_PER_KV)],
                             axis=0)
        return lax.dot_general(q2, k_ref[0, pl.ds(ks, span), :], NT_DIMS,
                               preferred_element_type=F32)

    def finish(scores, starts):
        m = functools.reduce(jnp.maximum, [jnp.max(s, axis=-1, keepdims=True) for s in scores])
        den = jnp.zeros((rows, 1), F32)
        out = jnp.zeros((rows, HEAD), F32)
        for s, ks, ((w, _), _, _, v_ref, _) in zip(scores, starts, groups):
            p = jnp.exp2((s - m) * ATTN_EXP2_SCALE)
            den = den + jnp.sum(p, axis=-1, keepdims=True)
            out = out + jnp.dot(p.astype(BF16), v_ref[0, pl.ds(ks, w + Q_BLOCK), :],
                                preferred_element_type=F32)
        out = out / den
        for r in range(Q_PER_KV):
            o_ref[0, :, r * HEAD:(r + 1) * HEAD] = (
                out[r * Q_BLOCK:(r + 1) * Q_BLOCK].astype(o_ref.dtype))

    @pl.when(t0 >= MAX_WINDOW)
    def _():
        scores, starts = [], []
        for (w, dil), q_ref, k_ref, _, b_ref in groups:
            ks = pl.multiple_of(t0 - w, Q_BLOCK)
            scores.append(raw_scores(q_ref, k_ref, ks, w + Q_BLOCK) + b_ref[...])
            starts.append(ks)
        finish(scores, starts)

    @pl.when(t0 < MAX_WINDOW)
    def _():
        scores, starts = [], []
        for (w, dil), q_ref, k_ref, _, _ in groups:
            span = w + Q_BLOCK
            ks = pl.multiple_of(jnp.maximum(t0 - w, 0), Q_BLOCK)
            tq = t0 + (lax.broadcasted_iota(jnp.int32, (rows, span), 0) & (Q_BLOCK - 1))
            diff = tq - (ks + lax.broadcasted_iota(jnp.int32, (rows, span), 1))
            valid = (diff >= 0) & (diff <= w) & ((diff & (dil - 1)) == 0)
            scores.append(jnp.where(valid, raw_scores(q_ref, k_ref, ks, span), -jnp.inf))
            starts.append(ks)
        finish(scores, starts)


def dilated_attention(q, k, v):
    b, s, _ = q.shape
    qcols = Q_PER_KV * HEAD

    def qspec(g):
        return pl.BlockSpec((1, Q_BLOCK, qcols), lambda bi, kv, qb, g=g: (bi, qb, g * KV_PER_GROUP + kv))

    def kvspec(g):
        return pl.BlockSpec((1, s, HEAD), lambda bi, kv, qb, g=g: (bi, 0, g * KV_PER_GROUP + kv))

    ng = len(DIL_GROUPS)
    biases = [jnp.asarray(_attn_window_bias(w, dil)) for w, dil in DIL_GROUPS]
    return pl.pallas_call(
        _attn_kernel,
        grid=(b, KV_PER_GROUP, s // Q_BLOCK),
        in_specs=([qspec(g) for g in range(ng)] + [kvspec(g) for g in range(ng)] * 2
                  + [pl.BlockSpec(bias.shape, lambda bi, kv, qb: (0, 0)) for bias in biases]),
        out_specs=pl.BlockSpec((1, Q_BLOCK, qcols), lambda bi, kv, qb: (bi, qb, kv)),
        out_shape=jax.ShapeDtypeStruct((b, s, KV_PER_GROUP * qcols), BF16),
        compiler_params=_cparams(("parallel", "parallel", "arbitrary")),
        name="dilated_attention",
    )(q, q, q, k, k, k, v, v, v, *biases)


ATTN_SUB = 128


def _band_bias(first):
    rq = np.arange(Q_PER_KV * ATTN_SUB)[:, None] % ATTN_SUB
    ck = np.arange(2 * ATTN_SUB)[None, :]
    diff = rq - ck if first else rq + ATTN_SUB - ck
    return np.where((diff >= 0) & (diff <= ATTN_SUB), 0.0, -np.inf).astype(np.float32)


def _attn_group_kernel(q_ref, k_ref, v_ref, bfirst_ref, bband_ref, o_ref, lse_ref, *, qb):
    nb = pl.program_id(3)
    scale = HEAD ** -0.5
    for sb in range(qb // ATTN_SUB):
        rows = slice(sb * ATTN_SUB, (sb + 1) * ATTN_SUB)
        n0 = nb * qb + sb * ATTN_SUB
        ks = pl.multiple_of(jnp.maximum(n0 - ATTN_SUB, 0), ATTN_SUB)
        q2 = jnp.concatenate([q_ref[0, rows, r * HEAD:(r + 1) * HEAD] for r in range(Q_PER_KV)],
                             axis=0)
        s = lax.dot_general(q2, k_ref[0, pl.ds(ks, 2 * ATTN_SUB), :], NT_DIMS,
                            preferred_element_type=F32)
        s = s + jnp.where(n0 == 0, bfirst_ref[...], bband_ref[...])
        m = jnp.max(s, axis=-1, keepdims=True)
        p = jnp.exp2((s - m) * ATTN_EXP2_SCALE)
        den = jnp.sum(p, axis=-1, keepdims=True)
        out = jnp.dot(p.astype(BF16), v_ref[0, pl.ds(ks, 2 * ATTN_SUB), :],
                      preferred_element_type=F32) / den
        lse = m * scale + jnp.log(den)
        for r in range(Q_PER_KV):
            part = slice(r * ATTN_SUB, (r + 1) * ATTN_SUB)
            o_ref[0, rows, r * HEAD:(r + 1) * HEAD] = out[part].astype(o_ref.dtype)
            lse_ref[0, rows, r * HEAD:(r + 1) * HEAD] = jnp.broadcast_to(lse[part],
                                                                          (ATTN_SUB, HEAD))


def attn_group(q, k, v, g, batch, *, qb):
    w, dil = DIL_GROUPS[g]
    assert w == ATTN_SUB * dil
    sub = q.shape[0] // batch
    qb = min(qb, sub)
    qcols = Q_PER_KV * HEAD
    kern = functools.partial(_attn_group_kernel, qb=qb)
    const = lambda bi, r, kv, nb: (0, 0)
    q_map = lambda bi, r, kv, nb: (bi, nb, r * KV_PER_GROUP + kv)
    kv_map = lambda bi, r, kv, nb: (bi, 0, r * KV_PER_GROUP + kv)
    o, lse = pl.pallas_call(
        kern,
        grid=(batch, dil, KV_PER_GROUP, sub // qb),
        in_specs=[pl.BlockSpec((1, qb, qcols), q_map),
                  pl.BlockSpec((1, sub, HEAD), kv_map),
                  pl.BlockSpec((1, sub, HEAD), kv_map),
                  pl.BlockSpec((Q_PER_KV * ATTN_SUB, 2 * ATTN_SUB), const),
                  pl.BlockSpec((Q_PER_KV * ATTN_SUB, 2 * ATTN_SUB), const)],
        out_specs=[pl.BlockSpec((1, qb, qcols), q_map), pl.BlockSpec((1, qb, qcols), q_map)],
        out_shape=[jax.ShapeDtypeStruct((batch, sub, dil * ATTN_QG), BF16),
                   jax.ShapeDtypeStruct((batch, sub, dil * ATTN_QG), F32)],
        compiler_params=_cparams(("parallel", "parallel", "parallel", "arbitrary")),
        name="attn_group",
    )(q.reshape(batch, sub, -1), k.reshape(batch, sub, -1), v.reshape(batch, sub, -1),
      jnp.asarray(_band_bias(True)), jnp.asarray(_band_bias(False)))
    return o.reshape(batch * sub, -1), lse.reshape(batch * sub, -1)


def _combine_proj_kernel(o0_ref, o1_ref, o2_ref, l0_ref, l1_ref, l2_ref, w_ref, r_ref, out_ref,
                         osc_ref, lsc_ref):
    tm = out_ref.shape[0]
    ng = len(DIL_GROUPS)
    n_heads = ATTN_QG // HEAD
    for g, (o_ref, l_ref) in enumerate(zip((o0_ref, o1_ref, o2_ref), (l0_ref, l1_ref, l2_ref))):
        dil = DIL_GROUPS[g][1]
        for r in range(dil):
            rows = pl.ds(r, tm // dil, stride=dil) if dil > 1 else slice(None)
            for hd in range(n_heads):
                cols = slice((r * n_heads + hd) * HEAD, (r * n_heads + hd + 1) * HEAD)
                osc_ref[g * n_heads + hd, rows, :] = o_ref[:, cols].astype(F32)
                lsc_ref[g * n_heads + hd, rows, :] = l_ref[:, cols]
    heads = []
    for hd in range(n_heads):
        lses = [lsc_ref[g * n_heads + hd] for g in range(ng)]
        m = functools.reduce(jnp.maximum, lses)
        ws = [jnp.exp(x - m) for x in lses]
        num = functools.reduce(lambda a, b: a + b,
                               [ws[g] * osc_ref[g * n_heads + hd] for g in range(ng)])
        heads.append((num / functools.reduce(lambda a, b: a + b, ws)).astype(BF16))
    comb = jnp.concatenate(heads, axis=1)
    out_ref[...] = r_ref[...] + jnp.dot(comb, w_ref[...], preferred_element_type=F32)


def combine_proj(outs, lses, w, res, *, tm):
    t, n = res.shape
    k = w.shape[0]
    ng = len(DIL_GROUPS)
    packed = [pl.BlockSpec((tm // dil, dil * k), lambda i: (i, 0)) for _, dil in DIL_GROUPS]
    return pl.pallas_call(
        _combine_proj_kernel,
        grid=(t // tm,),
        in_specs=packed + packed + [pl.BlockSpec((k, n), lambda i: (0, 0)),
                                    pl.BlockSpec((tm, n), lambda i: (i, 0))],
        out_specs=pl.BlockSpec((tm, n), lambda i: (i, 0)),
        out_shape=jax.ShapeDtypeStruct((t, n), F32),
        scratch_shapes=[pltpu.VMEM((ng * k // HEAD, tm, HEAD), F32),
                        pltpu.VMEM((ng * k // HEAD, tm, HEAD), F32)],
        compiler_params=_cparams(("parallel",)),
        name="combine_proj",
    )(*outs, *lses, w, res)


def kernel(x, hgrn_norm, hgrn_w_in, hgrn_lb_logits, hgrn_out_norm, hgrn_w_out, kv_norm, w_kv,
           k_norm, attn_norm, w_q, q_norm, w_o, ffn_norm, peer_w_q, peer_sub_keys, peer_u, peer_v):
    b, s, d = x.shape
    t = b * s
    depth = ffn_norm.shape[0]
    n_a = hgrn_norm.shape[0]
    h = x.reshape(t, d)
    tm = min(1024, t)
    for layer in range(depth):
        if layer < n_a:
            o = hgrn_mixer(h.reshape(b, s, d), hgrn_norm[layer], hgrn_w_in[layer], hgrn_lb_logits,
                           hgrn_out_norm[layer], layer=layer, seq_block=min(2048, s))
            h = matmul_residual(o.reshape(t, d), hgrn_w_out[layer].astype(BF16), h, tm=tm)
        else:
            j = layer - n_a
            qkv = attn_proj(h, kv_norm, attn_norm[j], w_kv.astype(BF16), w_q[j].astype(BF16),
                            k_norm, q_norm[j], s, tm=min(256, s))
            if j == 0:
                kv_sh = [(kg, vg) for _, kg, vg in qkv]
            parts = [attn_group(qkv[g][0], kv_sh[g][0], kv_sh[g][1], g, b, qb=512)
                     for g in range(len(DIL_GROUPS))]
            h = combine_proj([o for o, _ in parts], [l for _, l in parts],
                             w_o[j].astype(BF16), h, tm=min(512, t))
        h = peer_layer(h, ffn_norm[layer], peer_w_q[layer], peer_sub_keys[layer],
                       peer_u, peer_v, layer)
    return h.reshape(b, s, d)
```

```python
import functools
import math

import numpy as np
import jax
import jax.numpy as jnp
from jax import lax
from jax.experimental import pallas as pl
from jax.experimental.pallas import tpu as pltpu

F32 = jnp.float32
BF16 = jnp.bfloat16
EPS = 1e-6

LANES = 128
HEAD = 128
HG_CHUNK = 128
HG_SAFE_EXPONENT = 80.0
DIL_GROUPS = ((128, 1), (512, 4), (2048, 16))
KV_PER_GROUP = 2
Q_PER_KV = 2
ROT_DIM = HEAD // 4
ROPE_THETA = 500000.0
PEER_HEADS = 8
PEER_TOPK = 16
N_KEYS = 128
VMEM_LIMIT = 56 * 1024 * 1024

NT_DIMS = (((1,), (1,)), ((), ()))


def _cparams(sem):
    return pltpu.CompilerParams(dimension_semantics=sem, vmem_limit_bytes=VMEM_LIMIT)


def _rms(x):
    return x * lax.rsqrt(jnp.mean(x * x, axis=-1, keepdims=True) + EPS)


def _sigmoid(x):
    return 1.0 / (1.0 + jnp.exp(-x))


def _matmul_res_kernel(a_ref, w_ref, r_ref, o_ref):
    o_ref[...] = r_ref[...] + jnp.dot(a_ref[...], w_ref[...], preferred_element_type=F32)


def matmul_residual(a, w, res, *, tm):
    t, k = a.shape
    n = w.shape[1]
    return pl.pallas_call(
        _matmul_res_kernel,
        grid=(t // tm,),
        in_specs=[pl.BlockSpec((tm, k), lambda i: (i, 0)),
                  pl.BlockSpec((k, n), lambda i: (0, 0)),
                  pl.BlockSpec((tm, n), lambda i: (i, 0))],
        out_specs=pl.BlockSpec((tm, n), lambda i: (i, 0)),
        out_shape=jax.ShapeDtypeStruct((t, n), F32),
        compiler_params=_cparams(("parallel",)),
        name="matmul_residual",
    )(a, w, res)


def _hgrn_levels(c):
    return int(math.log2(c))


def _hgrn_sum_matrix(c):
    nlev = _hgrn_levels(c)
    m = np.zeros(((nlev + 2) * c, c), np.float32)
    for l in range(nlev):
        h = c >> (l + 1)
        for r in range(c):
            mid = (r // (2 * h)) * 2 * h + h
            if r >= mid:
                m[l * c + r, mid:r + 1] = 1.0
            else:
                m[l * c + r, r + 1:mid] = 1.0
    for r in range(c):
        m[nlev * c + r, :r + 1] = 1.0
        m[(nlev + 1) * c + r, r + 1:] = 1.0
    return m


def _hgrn_kernel(x_ref, ng_ref, w_ref, lbl_ref, og_ref, m_ref, o_ref, xn_ref, y_ref, st_all_ref, *,
                 layer, chunk, n_chunks):
    c = chunk
    nlev = _hgrn_levels(c)
    head = pl.program_id(2)
    st_ref = st_all_ref.at[head]

    @pl.when(head == 0)
    def _():
        xn_ref[...] = (_rms(x_ref[0]) * ng_ref[...]).astype(xn_ref.dtype)

    @pl.when(pl.program_id(1) == 0)
    def _():
        st_ref[...] = jnp.zeros_like(st_ref)

    y_ref[...] = jnp.dot(xn_ref[...], w_ref[...], preferred_element_type=F32)

    lg = lbl_ref[...]
    e = jnp.exp(lg - jnp.max(lg, axis=0, keepdims=True))
    lb = jnp.sum(e[:layer + 1], axis=0, keepdims=True) / jnp.sum(e, axis=0, keepdims=True)
    og = og_ref[...]
    row = lax.broadcasted_iota(jnp.int32, (c, c), 0)
    col = lax.broadcasted_iota(jnp.int32, (c, c), 1)

    def body(ci, carry, *, mild):
        sl = pl.ds(pl.multiple_of(ci * c, c), c)
        q = y_ref[sl, 0:HEAD]
        fr = y_ref[sl, HEAD:2 * HEAD]
        v = y_ref[sl, 2 * HEAD:3 * HEAD]
        gt = y_ref[sl, 3 * HEAD:4 * HEAD]
        qa = q * _sigmoid(q)
        f = lb + (1.0 - lb) * _sigmoid(fr)
        k = 1.0 - f
        logf = jnp.log(f)
        hi = logf.astype(BF16)
        lo = (logf - hi.astype(F32)).astype(BF16)
        g2 = jnp.concatenate([hi, lo], axis=1)
        if mild:
            e2 = jnp.dot(m_ref[nlev * c:(nlev + 1) * c, :], g2, preferred_element_type=F32)
            b_incl = e2[:, :HEAD] + e2[:, HEAD:]
            dmid = b_incl - b_incl[c // 2 - 1:c // 2, :]
            p = lax.dot_general((qa * jnp.exp(dmid)).astype(BF16),
                                (k * jnp.exp(-dmid)).astype(BF16), NT_DIMS,
                                preferred_element_type=F32)
            s = jnp.where(row >= col, p, 0.0)
            rev = b_incl[c - 1:c, :] - b_incl
        else:
            e2 = jnp.dot(m_ref[...], g2, preferred_element_type=F32)
            ex = e2[:, :HEAD] + e2[:, HEAD:]
            s = jnp.where(row == col,
                          lax.dot_general(qa.astype(BF16), k.astype(BF16), NT_DIMS,
                                          preferred_element_type=F32),
                          0.0)
            for l in range(nlev):
                sh = nlev - 1 - l
                x = jnp.exp(ex[l * c:(l + 1) * c])
                p = lax.dot_general((qa * x).astype(BF16), (k * x).astype(BF16), NT_DIMS,
                                    preferred_element_type=F32)
                mask = (((row >> (sh + 1)) == (col >> (sh + 1)))
                        & (((row >> sh) & 1) == 1) & (((col >> sh) & 1) == 0))
                s = jnp.where(mask, p, s)
            b_incl = ex[nlev * c:(nlev + 1) * c]
            rev = ex[(nlev + 1) * c:(nlev + 2) * c]
        vb = v.astype(BF16)
        intra = jnp.dot(s.astype(BF16), vb, preferred_element_type=F32)
        st = st_ref[...]
        inter = lax.dot_general((qa * jnp.exp(b_incl)).astype(BF16), st.astype(BF16), NT_DIMS,
                                preferred_element_type=F32)
        o = inter + intra
        y = _rms(o) * og * (gt * _sigmoid(gt))
        o_ref[0, sl, :] = y.astype(o_ref.dtype)

        k2 = (k * jnp.exp(rev)).astype(BF16)
        upd = jnp.dot(v.T.astype(BF16), k2, preferred_element_type=F32)
        st_ref[...] = st * jnp.exp(b_incl[c - 1:c, :]) + upd
        return carry

    mild = jnp.min(lb) >= math.exp(-HG_SAFE_EXPONENT / (c // 2))

    @pl.when(mild)
    def _():
        lax.fori_loop(0, n_chunks, functools.partial(body, mild=True), 0, unroll=4)

    @pl.when(jnp.logical_not(mild))
    def _():
        lax.fori_loop(0, n_chunks, functools.partial(body, mild=False), 0, unroll=4)


def hgrn_mixer(x, norm_gain, w_in, lb_logits, out_gain, *, layer, seq_block):
    b, s, d = x.shape
    nh = d // HEAD
    c = min(HG_CHUNK, seq_block)
    msel = jnp.asarray(_hgrn_sum_matrix(c), BF16)
    nl = lb_logits.shape[0]
    w_heads = (w_in.astype(BF16).reshape(d, 4, nh, HEAD).transpose(0, 2, 1, 3)
               .reshape(d, nh * 4 * HEAD))
    kern = functools.partial(_hgrn_kernel, layer=layer, chunk=c, n_chunks=seq_block // c)
    return pl.pallas_call(
        kern,
        grid=(b, s // seq_block, nh),
        in_specs=[pl.BlockSpec((1, seq_block, d), lambda bi, si, hi: (bi, si, 0)),
                  pl.BlockSpec((1, d), lambda bi, si, hi: (0, 0)),
                  pl.BlockSpec((d, 4 * HEAD), lambda bi, si, hi: (0, hi)),
                  pl.BlockSpec((nl, HEAD), lambda bi, si, hi: (0, hi)),
                  pl.BlockSpec((1, HEAD), lambda bi, si, hi: (0, 0)),
                  pl.BlockSpec(msel.shape, lambda bi, si, hi: (0, 0))],
        out_specs=pl.BlockSpec((1, seq_block, HEAD), lambda bi, si, hi: (bi, si, hi)),
        out_shape=jax.ShapeDtypeStruct((b, s, d), BF16),
        scratch_shapes=[pltpu.VMEM((seq_block, d), BF16), pltpu.VMEM((seq_block, 4 * HEAD), F32),
                        pltpu.VMEM((nh, HEAD, HEAD), F32)],
        compiler_params=_cparams(("parallel", "arbitrary", "arbitrary")),
        name="hgrn_mixer",
    )(x, norm_gain.reshape(1, d), w_heads, lb_logits, out_gain.reshape(1, HEAD), msel)


def _score_weights_kernel(keys_ref, wq_ref, o_ref):
    o_ref[...] = lax.dot_general(keys_ref[...].astype(BF16), wq_ref[...].astype(BF16), NT_DIMS,
                                 preferred_element_type=F32).astype(o_ref.dtype)


def score_weights(wq, keys):
    d, nq = wq.shape
    nblk = nq // HEAD
    return pl.pallas_call(
        _score_weights_kernel,
        grid=(nblk,),
        in_specs=[pl.BlockSpec((None, N_KEYS, HEAD), lambda i: (i % 2, 0, 0)),
                  pl.BlockSpec((d, HEAD), lambda i: (0, i))],
        out_specs=pl.BlockSpec((N_KEYS, d), lambda i: (i, 0)),
        out_shape=jax.ShapeDtypeStruct((nblk * N_KEYS, d), BF16),
        compiler_params=_cparams(("parallel",)),
        name="score_weights",
    )(keys, wq)


def _peer_prep_kernel(h_ref, g_ref, wk_ref, xn_ref, s1_ref, s2_ref):
    xn = (_rms(h_ref[...]) * g_ref[...]).astype(BF16)
    xn_ref[...] = xn
    st = lax.dot_general(wk_ref[...], xn, NT_DIMS, preferred_element_type=F32)
    for hh in range(PEER_HEADS):
        base = hh * 2 * N_KEYS
        s1_ref[hh] = st[base:base + N_KEYS]
        s2_ref[hh] = st[base + N_KEYS:base + 2 * N_KEYS]


def peer_prep(h, gain, wk, *, tm):
    t, d = h.shape
    return pl.pallas_call(
        _peer_prep_kernel,
        grid=(t // tm,),
        in_specs=[pl.BlockSpec((tm, d), lambda i: (i, 0)),
                  pl.BlockSpec((1, d), lambda i: (0, 0)),
                  pl.BlockSpec(wk.shape, lambda i: (0, 0))],
        out_specs=[pl.BlockSpec((tm, d), lambda i: (i, 0)),
                   pl.BlockSpec((PEER_HEADS, N_KEYS, tm), lambda i: (0, 0, i)),
                   pl.BlockSpec((PEER_HEADS, N_KEYS, tm), lambda i: (0, 0, i))],
        out_shape=[jax.ShapeDtypeStruct((t, d), BF16),
                   jax.ShapeDtypeStruct((PEER_HEADS, N_KEYS, t), F32),
                   jax.ShapeDtypeStruct((PEER_HEADS, N_KEYS, t), F32)],
        compiler_params=_cparams(("parallel",)),
        name="peer_prep",
    )(h, gain.reshape(1, d), wk)


def _oddeven_merge(lo, hi, r):
    step = r * 2
    if step < hi - lo:
        yield from _oddeven_merge(lo, hi, step)
        yield from _oddeven_merge(lo + r, hi, step)
        yield from [(i, i + r) for i in range(lo + r, hi - r, step)]
    else:
        yield (lo, lo + r)


def _oddeven_sort(lo, hi):
    if hi - lo >= 1:
        mid = lo + (hi - lo) // 2
        yield from _oddeven_sort(lo, mid)
        yield from _oddeven_sort(mid + 1, hi)
        yield from _oddeven_merge(lo, hi, 1)


SORT16 = tuple(_oddeven_sort(0, PEER_TOPK - 1))
BITONIC16 = tuple((i, i + h) for h in (8, 4, 2, 1) for i in range(PEER_TOPK) if i % (2 * h) < h)
SUBLANES = 8


def _compare_exchange(v, pairs):
    v = list(v)
    for i, j in pairs:
        v[i], v[j] = jnp.maximum(v[i], v[j]), jnp.minimum(v[i], v[j])
    return v


def _merge_sublane_lists(v):
    n = len(v)
    for shift in (4, 2, 1):
        w = [pltpu.roll(x, shift, 0) for x in v]
        v = _compare_exchange([jnp.maximum(v[k], w[n - 1 - k]) for k in range(n)], BITONIC16)
    return v


def _top16(x):
    slabs = [x[SUBLANES * k:SUBLANES * (k + 1), :] for k in range(PEER_TOPK)]
    return _merge_sublane_lists(_compare_exchange(slabs, SORT16))


def _peer_topk_kernel(s1_ref, s2_ref, cnt_ref, ar_ref, rk_ref, bt_ref):
    k = PEER_TOPK
    lanes = s1_ref.shape[2]
    sub = lax.broadcasted_iota(jnp.int32, (SUBLANES, lanes), 0)

    def head(hh, carry):
        s1 = s1_ref[hh]
        s2 = s2_ref[hh]
        va = _top16(s1)
        vb = _top16(s2)
        bd = vb[0]
        for p in range(1, SUBLANES):
            bd = jnp.where(sub == p, vb[p], bd)
        cand = []
        for i in range(k):
            c = va[i] + bd
            lim = k // (i + 1)
            cand.append(c if lim >= SUBLANES else jnp.where(sub < lim, c, -jnp.inf))
        tail = [va[0] + vb[SUBLANES + q] for q in range(k - SUBLANES)]
        top16 = _merge_sublane_lists(cand)
        tau = top16[k - SUBLANES - 1]
        for q in range(k - SUBLANES):
            tau = jnp.minimum(tau, jnp.maximum(top16[k - 1 - q], tail[q]))
        top = va[0] + vb[0]
        zc = jnp.zeros((SUBLANES, lanes), F32)
        for c in cand:
            zc = zc + jnp.where(c >= tau, jnp.exp(c - top), 0.0)
        zt = jnp.zeros((SUBLANES, lanes), F32)
        for c in tail:
            zt = zt + jnp.where(c >= tau, jnp.exp(c - top), 0.0)
        z = jnp.sum(zc, axis=0, keepdims=True) + zt[0:1, :]
        tau_row = tau[0:1, :]
        b_rows = [x[0:1, :] for x in vb]
        cnt = jnp.zeros(s1.shape, F32)
        rank2 = jnp.zeros(s2.shape, F32)
        for j in range(k):
            cnt = jnp.where(s1 + b_rows[j] >= tau_row, float(j + 1), cnt)
            rank2 = jnp.where(b_rows[j] > s2, float(j + 1), rank2)
        cnt_ref[hh] = cnt
        ar_ref[hh] = jnp.exp(s1 - va[0][0:1, :]) / z
        bt = jnp.exp(s2 - b_rows[0])
        for lb in range(lanes // LANES):
            ln = slice(lb * LANES, (lb + 1) * LANES)
            rk_ref[hh, lb] = pltpu.bitcast(rank2[:, ln].astype(BF16), jnp.uint32)
            bt_ref[hh, lb] = pltpu.bitcast(bt[:, ln].astype(BF16), jnp.uint32)
        return carry

    lax.fori_loop(0, PEER_HEADS, head, 0)


def peer_topk(s1t, s2t, *, tt):
    nh, nk, t = s1t.shape
    spec = pl.BlockSpec((nh, nk, tt), lambda i: (0, 0, i))
    shp = jax.ShapeDtypeStruct((nh, nk, t), F32)
    tspec = pl.BlockSpec((nh, tt // LANES, nk // 2, LANES), lambda i: (0, i, 0, 0))
    tshp = jax.ShapeDtypeStruct((nh, t // LANES, nk // 2, LANES), jnp.uint32)
    return pl.pallas_call(
        _peer_topk_kernel,
        grid=(t // tt,),
        in_specs=[spec, spec],
        out_specs=[spec, spec, tspec, tspec],
        out_shape=[shp, shp, tshp, tshp],
        compiler_params=_cparams(("parallel",)),
        name="peer_topk",
    )(s1t, s2t)


GELU_C0 = math.sqrt(2.0 / math.pi)
GELU_C1 = GELU_C0 * 0.044715


def _gelu(z):
    return 0.5 * z * (1.0 + jnp.tanh(z * (GELU_C0 + GELU_C1 * (z * z))))


def _peer_main_kernel(h_ref, xn_ref, u_ref, vt_ref, cnt_ref, ar_ref, rk_ref, bt_ref, o_ref,
                      acc_ref, zt_ref, a_ref, *, n_blocks, blocks_per_tile, lane_blocks_per_iter):
    g_step = pl.program_id(0)
    et, tt = zt_ref.shape
    n_e1 = et // N_KEYS
    jb = jnp.maximum(g_step - 1, 0) % blocks_per_tile

    @pl.when(g_step == 0)
    def _():
        a_ref[...] = jnp.zeros_like(a_ref)

    @pl.when(jb == 0)
    def _():
        acc_ref[...] = jnp.zeros_like(acc_ref)

    acc_ref[...] += jnp.dot(vt_ref[...], a_ref[...], preferred_element_type=F32)
    zt_ref[...] = lax.dot_general(u_ref[...].astype(BF16), xn_ref[...], NT_DIMS,
                                  preferred_element_type=F32)

    @pl.when((g_step >= 1) & (jb == blocks_per_tile - 1))
    def _():
        o_ref[...] = h_ref[...] + acc_ref[...].T

    def lane_group(li, carry):
        for sub in range(lane_blocks_per_iter):
            lb = li * lane_blocks_per_iter + sub
            ln = pl.ds(pl.multiple_of(lb * LANES, LANES), LANES)
            for c in range(n_e1):
                rows = slice(c * N_KEYS, (c + 1) * N_KEYS)
                g = jnp.zeros((N_KEYS, LANES), BF16)
                for hh in range(PEER_HEADS):
                    cnt = cnt_ref[hh, c:c + 1, ln].astype(BF16)
                    ar = ar_ref[hh, c:c + 1, ln].astype(BF16)
                    rk = pltpu.bitcast(rk_ref[hh, lb], BF16)
                    bt = pltpu.bitcast(bt_ref[hh, lb], BF16)
                    g = g + jnp.where(rk < cnt, bt * ar, 0)
                a_ref[rows, ln] = _gelu(zt_ref[rows, ln].astype(BF16)) * g
        return carry

    lax.fori_loop(0, tt // (LANES * lane_blocks_per_iter), lane_group, 0)


def _transpose_cast_kernel(x_ref, o_ref):
    o_ref[...] = x_ref[...].T.astype(o_ref.dtype)


def transpose_cast(stack, layer, *, rows):
    _, e, d = stack.shape
    return pl.pallas_call(
        _transpose_cast_kernel,
        grid=(e // rows,),
        in_specs=[pl.BlockSpec((None, rows, d), lambda j: (layer, j, 0))],
        out_specs=pl.BlockSpec((d, rows), lambda j: (0, j)),
        out_shape=jax.ShapeDtypeStruct((d, e), BF16),
        compiler_params=_cparams(("parallel",)),
        name="transpose_cast",
    )(stack)


def peer_main(h, xn, u_stack, layer, vt, cnt, ar, rank_tiles, bt_tiles, *, tt, et,
              lane_blocks_per_iter):
    t, d = h.shape
    ne = u_stack.shape[1]
    n_e1 = et // N_KEYS
    bpt = ne // et
    n_blocks = (t // tt) * bpt
    kern = functools.partial(_peer_main_kernel, n_blocks=n_blocks, blocks_per_tile=bpt,
                             lane_blocks_per_iter=lane_blocks_per_iter)

    def cur(g):
        return jnp.minimum(g, n_blocks - 1)

    def prev(g):
        return jnp.maximum(g - 1, 0)

    row_spec = pl.BlockSpec((PEER_HEADS, n_e1, tt), lambda g: (0, cur(g) % bpt, cur(g) // bpt))
    tok_spec = pl.BlockSpec((PEER_HEADS, tt // LANES, N_KEYS // 2, LANES),
                            lambda g: (0, cur(g) // bpt, 0, 0))
    return pl.pallas_call(
        kern,
        grid=(n_blocks + 1,),
        in_specs=[pl.BlockSpec((tt, d), lambda g: (prev(g) // bpt, 0)),
                  pl.BlockSpec((tt, d), lambda g: (cur(g) // bpt, 0)),
                  pl.BlockSpec((None, et, d), lambda g: (layer, cur(g) % bpt, 0)),
                  pl.BlockSpec((d, et), lambda g: (0, prev(g) % bpt)),
                  row_spec, row_spec, tok_spec, tok_spec],
        out_specs=pl.BlockSpec((tt, d), lambda g: (prev(g) // bpt, 0)),
        out_shape=jax.ShapeDtypeStruct((t, d), F32),
        scratch_shapes=[pltpu.VMEM((d, tt), F32), pltpu.VMEM((et, tt), F32),
                        pltpu.VMEM((et, tt), BF16)],
        compiler_params=_cparams(("arbitrary",)),
        name="peer_main",
    )(h, xn, u_stack, vt, cnt, ar, rank_tiles, bt_tiles)


def peer_layer(h, gain, wq, keys, u_stack, v_stack, layer):
    t = h.shape[0]
    xn, s1t, s2t = peer_prep(h, gain, score_weights(wq, keys), tm=min(1024, t))
    cnt, ar, rank_tiles, bt_tiles = peer_topk(s1t, s2t, tt=min(256, t))
    vt = transpose_cast(v_stack, layer, rows=512)
    return peer_main(h, xn, u_stack, layer, vt, cnt, ar, rank_tiles, bt_tiles,
                     tt=min(512, t), et=2048, lane_blocks_per_iter=1)


def _rope_tables(s):
    half = ROT_DIM // 2
    inv = ROPE_THETA ** (-jnp.arange(half, dtype=F32) * 2.0 / ROT_DIM)
    ang = jnp.arange(s).astype(F32)[:, None] * inv[None, :]
    cos, sin = jnp.cos(ang), jnp.sin(ang)
    pad = HEAD - ROT_DIM
    ones = jnp.ones((s, pad), F32)
    zeros = jnp.zeros((s, pad), F32)
    zh = jnp.zeros((s, half), F32)
    ct = jnp.concatenate([cos, cos, ones], axis=1)
    at = jnp.concatenate([-sin, zh, zeros], axis=1)
    bt = jnp.concatenate([zh, sin, zeros], axis=1)
    return ct, at, bt


ATTN_QG = KV_PER_GROUP * Q_PER_KV * HEAD
ATTN_KG = KV_PER_GROUP * HEAD


def _attn_proj_kernel(h_ref, kvg_ref, ag_ref, wkv_ref, wq_ref, kn_ref, qn_ref,
                      ct_ref, at_ref, bt_ref, *refs):
    out_refs, (qs_ref, kvs_ref) = refs[:-2], refs[-2:]
    y = _rms(h_ref[...])
    kv = jnp.dot((y * kvg_ref[...]).astype(BF16), wkv_ref[...], preferred_element_type=F32)
    qq = jnp.dot((y * ag_ref[...]).astype(BF16), wq_ref[...], preferred_element_type=F32)
    ct, at, bt = ct_ref[...], at_ref[...], bt_ref[...]
    half = ROT_DIM // 2

    def head_norm_rope(x, gain):
        n = _rms(x) * gain
        return n * ct + pltpu.roll(n, HEAD - half, 1) * at + pltpu.roll(n, half, 1) * bt

    n_kh = kv.shape[1] // (2 * HEAD)
    for hd in range(n_kh):
        cs = slice(hd * HEAD, (hd + 1) * HEAD)
        kvs_ref[hd] = head_norm_rope(kv[:, cs], kn_ref[...])
        kvs_ref[n_kh + hd] = kv[:, n_kh * HEAD + hd * HEAD:n_kh * HEAD + (hd + 1) * HEAD]
    for hd in range(qq.shape[1] // HEAD):
        qs_ref[hd] = head_norm_rope(qq[:, hd * HEAD:(hd + 1) * HEAD], qn_ref[...])

    tm = qs_ref.shape[1]
    q_heads, k_heads = ATTN_QG // HEAD, ATTN_KG // HEAD
    for g, (_, dil) in enumerate(DIL_GROUPS):
        q_ref, k_ref, v_ref = out_refs[3 * g:3 * g + 3]
        for r in range(dil):
            rows = pl.ds(r, tm // dil, stride=dil) if dil > 1 else slice(None)
            for hd in range(q_heads):
                q_ref[:, (r * q_heads + hd) * HEAD:(r * q_heads + hd + 1) * HEAD] = (
                    qs_ref[g * q_heads + hd, rows, :].astype(q_ref.dtype))
            for hd in range(k_heads):
                cols = slice((r * k_heads + hd) * HEAD, (r * k_heads + hd + 1) * HEAD)
                k_ref[:, cols] = kvs_ref[g * k_heads + hd, rows, :].astype(k_ref.dtype)
                v_ref[:, cols] = kvs_ref[n_kh + g * k_heads + hd, rows, :].astype(v_ref.dtype)


def attn_proj(h, kv_gain, a_gain, wkv, wq, k_norm, q_norm, seq, *, tm):
    t, d = h.shape
    ct, at, bt = _rope_tables(seq)
    ns = seq // tm
    tab = pl.BlockSpec((tm, HEAD), lambda i: (i % ns, 0))
    vec = lambda n: pl.BlockSpec((1, n), lambda i: (0, 0))
    out_specs, out_shape = [], []
    for _, dil in DIL_GROUPS:
        for width in (ATTN_QG, ATTN_KG, ATTN_KG):
            out_specs.append(pl.BlockSpec((tm // dil, dil * width), lambda i: (i, 0)))
            out_shape.append(jax.ShapeDtypeStruct((t // dil, dil * width), BF16))
    outs = pl.pallas_call(
        _attn_proj_kernel,
        grid=(t // tm,),
        in_specs=[pl.BlockSpec((tm, d), lambda i: (i, 0)), vec(d), vec(d),
                  pl.BlockSpec(wkv.shape, lambda i: (0, 0)),
                  pl.BlockSpec(wq.shape, lambda i: (0, 0)),
                  vec(HEAD), vec(HEAD), tab, tab, tab],
        out_specs=out_specs,
        out_shape=out_shape,
        scratch_shapes=[pltpu.VMEM((wq.shape[1] // HEAD, tm, HEAD), F32),
                        pltpu.VMEM((wkv.shape[1] // HEAD, tm, HEAD), F32)],
        compiler_params=_cparams(("parallel",)),
        name="attn_proj",
    )(h, kv_gain.reshape(1, d), a_gain.reshape(1, d), wkv, wq,
      k_norm.reshape(1, HEAD), q_norm.reshape(1, HEAD), ct, at, bt)
    return [tuple(outs[3 * g:3 * g + 3]) for g in range(len(DIL_GROUPS))]


ATTN_EXP2_SCALE = HEAD ** -0.5 * math.log2(math.e)
ATTN_SUB = 128


def _band_bias(first):
    rq = np.arange(Q_PER_KV * ATTN_SUB)[:, None] % ATTN_SUB
    ck = np.arange(2 * ATTN_SUB)[None, :]
    diff = rq - ck if first else rq + ATTN_SUB - ck
    return np.where((diff >= 0) & (diff <= ATTN_SUB), 0.0, -np.inf).astype(np.float32)


def _attn_group_kernel(q_ref, k_ref, v_ref, bfirst_ref, bband_ref, o_ref, lse_ref, *, qb):
    nb = pl.program_id(3)
    scale = HEAD ** -0.5
    for sb in range(qb // ATTN_SUB):
        rows = slice(sb * ATTN_SUB, (sb + 1) * ATTN_SUB)
        n0 = nb * qb + sb * ATTN_SUB
        ks = pl.multiple_of(jnp.maximum(n0 - ATTN_SUB, 0), ATTN_SUB)
        q2 = jnp.concatenate([q_ref[0, rows, r * HEAD:(r + 1) * HEAD] for r in range(Q_PER_KV)],
                             axis=0)
        s = lax.dot_general(q2, k_ref[0, pl.ds(ks, 2 * ATTN_SUB), :], NT_DIMS,
                            preferred_element_type=F32)
        s = s + jnp.where(n0 == 0, bfirst_ref[...], bband_ref[...])
        m = jnp.max(s, axis=-1, keepdims=True)
        p = jnp.exp2((s - m) * ATTN_EXP2_SCALE)
        den = jnp.sum(p, axis=-1, keepdims=True)
        out = jnp.dot(p.astype(BF16), v_ref[0, pl.ds(ks, 2 * ATTN_SUB), :],
                      preferred_element_type=F32) / den
        lse = m * scale + jnp.log(den)
        for r in range(Q_PER_KV):
            part = slice(r * ATTN_SUB, (r + 1) * ATTN_SUB)
            o_ref[0, rows, r * HEAD:(r + 1) * HEAD] = out[part].astype(o_ref.dtype)
            lse_ref[0, rows, r * HEAD:(r + 1) * HEAD] = jnp.broadcast_to(lse[part],
                                                                          (ATTN_SUB, HEAD))


def attn_group(q, k, v, g, batch, *, qb):
    w, dil = DIL_GROUPS[g]
    assert w == ATTN_SUB * dil
    sub = q.shape[0] // batch
    qb = min(qb, sub)
    qcols = Q_PER_KV * HEAD
    kern = functools.partial(_attn_group_kernel, qb=qb)
    const = lambda bi, r, kv, nb: (0, 0)
    q_map = lambda bi, r, kv, nb: (bi, nb, r * KV_PER_GROUP + kv)
    kv_map = lambda bi, r, kv, nb: (bi, 0, r * KV_PER_GROUP + kv)
    o, lse = pl.pallas_call(
        kern,
        grid=(batch, dil, KV_PER_GROUP, sub // qb),
        in_specs=[pl.BlockSpec((1, qb, qcols), q_map),
                  pl.BlockSpec((1, sub, HEAD), kv_map),
                  pl.BlockSpec((1, sub, HEAD), kv_map),
                  pl.BlockSpec((Q_PER_KV * ATTN_SUB, 2 * ATTN_SUB), const),
                  pl.BlockSpec((Q_PER_KV * ATTN_SUB, 2 * ATTN_SUB), const)],
        out_specs=[pl.BlockSpec((1, qb, qcols), q_map), pl.BlockSpec((1, qb, qcols), q_map)],
        out_shape=[jax.ShapeDtypeStruct((batch, sub, dil * ATTN_QG), BF16),
                   jax.ShapeDtypeStruct((batch, sub, dil * ATTN_QG), F32)],
        compiler_params=_cparams(("parallel", "parallel", "parallel", "arbitrary")),
        name="attn_group",
    )(q.reshape(batch, sub, -1), k.reshape(batch, sub, -1), v.reshape(batch, sub, -1),
      jnp.asarray(_band_bias(True)), jnp.asarray(_band_bias(False)))
    return o.reshape(batch * sub, -1), lse.reshape(batch * sub, -1)


def _combine_proj_kernel(o0_ref, o1_ref, o2_ref, l0_ref, l1_ref, l2_ref, w_ref, r_ref, out_ref,
                         osc_ref, lsc_ref):
    tm = out_ref.shape[0]
    ng = len(DIL_GROUPS)
    n_heads = ATTN_QG // HEAD
    for g, (o_ref, l_ref) in enumerate(zip((o0_ref, o1_ref, o2_ref), (l0_ref, l1_ref, l2_ref))):
        dil = DIL_GROUPS[g][1]
        for r in range(dil):
            rows = pl.ds(r, tm // dil, stride=dil) if dil > 1 else slice(None)
            for hd in range(n_heads):
                cols = slice((r * n_heads + hd) * HEAD, (r * n_heads + hd + 1) * HEAD)
                osc_ref[g * n_heads + hd, rows, :] = o_ref[:, cols].astype(F32)
                lsc_ref[g * n_heads + hd, rows, :] = l_ref[:, cols]
    heads = []
    for hd in range(n_heads):
        lses = [lsc_ref[g * n_heads + hd] for g in range(ng)]
        m = functools.reduce(jnp.maximum, lses)
        ws = [jnp.exp(x - m) for x in lses]
        num = functools.reduce(lambda a, b: a + b,
                               [ws[g] * osc_ref[g * n_heads + hd] for g in range(ng)])
        heads.append((num / functools.reduce(lambda a, b: a + b, ws)).astype(BF16))
    comb = jnp.concatenate(heads, axis=1)
    out_ref[...] = r_ref[...] + jnp.dot(comb, w_ref[...], preferred_element_type=F32)


def combine_proj(outs, lses, w, res, *, tm):
    t, n = res.shape
    k = w.shape[0]
    ng = len(DIL_GROUPS)
    packed = [pl.BlockSpec((tm // dil, dil * k), lambda i: (i, 0)) for _, dil in DIL_GROUPS]
    return pl.pallas_call(
        _combine_proj_kernel,
        grid=(t // tm,),
        in_specs=packed + packed + [pl.BlockSpec((k, n), lambda i: (0, 0)),
                                    pl.BlockSpec((tm, n), lambda i: (i, 0))],
        out_specs=pl.BlockSpec((tm, n), lambda i: (i, 0)),
        out_shape=jax.ShapeDtypeStruct((t, n), F32),
        scratch_shapes=[pltpu.VMEM((ng * k // HEAD, tm, HEAD), F32),
                        pltpu.VMEM((ng * k // HEAD, tm, HEAD), F32)],
        compiler_params=_cparams(("parallel",)),
        name="combine_proj",
    )(*outs, *lses, w, res)


def kernel(x, hgrn_norm, hgrn_w_in, hgrn_lb_logits, hgrn_out_norm, hgrn_w_out, kv_norm, w_kv,
           k_norm, attn_norm, w_q, q_norm, w_o, ffn_norm, peer_w_q, peer_sub_keys, peer_u, peer_v):
    b, s, d = x.shape
    t = b * s
    depth = ffn_norm.shape[0]
    n_a = hgrn_norm.shape[0]
    h = x.reshape(t, d)
    tm = min(1024, t)
    for layer in range(depth):
        if layer < n_a:
            o = hgrn_mixer(h.reshape(b, s, d), hgrn_norm[layer], hgrn_w_in[layer], hgrn_lb_logits,
                           hgrn_out_norm[layer], layer=layer, seq_block=min(2048, s))
            h = matmul_residual(o.reshape(t, d), hgrn_w_out[layer].astype(BF16), h, tm=tm)
        else:
            j = layer - n_a
            qkv = attn_proj(h, kv_norm, attn_norm[j], w_kv.astype(BF16), w_q[j].astype(BF16),
                            k_norm, q_norm[j], s, tm=min(256, s))
            if j == 0:
                kv_sh = [(kg, vg) for _, kg, vg in qkv]
            parts = [attn_group(qkv[g][0], kv_sh[g][0], kv_sh[g][1], g, b, qb=512)
                     for g in range(len(DIL_GROUPS))]
            h = combine_proj([o for o, _ in parts], [l for _, l in parts],
                             w_o[j].astype(BF16), h, tm=min(512, t))
        h = peer_layer(h, ffn_norm[layer], peer_w_q[layer], peer_sub_keys[layer],
                       peer_u, peer_v, layer)
    return h.reshape(b, s, d)
```

```python
import functools
import math

import numpy as np
import jax
import jax.numpy as jnp
from jax import lax
from jax.experimental import pallas as pl
from jax.experimental.pallas import tpu as pltpu

F32 = jnp.float32
BF16 = jnp.bfloat16
EPS = 1e-6

LANES = 128
HEAD = 128
HG_CHUNK = 128
HG_SAFE_EXPONENT = 80.0
DIL_GROUPS = ((128, 1), (512, 4), (2048, 16))
KV_PER_GROUP = 2
Q_PER_KV = 2
ROT_DIM = HEAD // 4
ROPE_THETA = 500000.0
PEER_HEADS = 8
PEER_TOPK = 16
N_KEYS = 128
VMEM_LIMIT = 56 * 1024 * 1024

NT_DIMS = (((1,), (1,)), ((), ()))


def _cparams(sem):
    return pltpu.CompilerParams(dimension_semantics=sem, vmem_limit_bytes=VMEM_LIMIT)


def _rms(x):
    return x * lax.rsqrt(jnp.mean(x * x, axis=-1, keepdims=True) + EPS)


def _sigmoid(x):
    return 1.0 / (1.0 + jnp.exp(-x))


def _matmul_res_kernel(a_ref, w_ref, r_ref, o_ref):
    o_ref[...] = r_ref[...] + jnp.dot(a_ref[...], w_ref[...], preferred_element_type=F32)


def matmul_residual(a, w, res, *, tm):
    t, k = a.shape
    n = w.shape[1]
    return pl.pallas_call(
        _matmul_res_kernel,
        grid=(t // tm,),
        in_specs=[pl.BlockSpec((tm, k), lambda i: (i, 0)),
                  pl.BlockSpec((k, n), lambda i: (0, 0)),
                  pl.BlockSpec((tm, n), lambda i: (i, 0))],
        out_specs=pl.BlockSpec((tm, n), lambda i: (i, 0)),
        out_shape=jax.ShapeDtypeStruct((t, n), F32),
        compiler_params=_cparams(("parallel",)),
        name="matmul_residual",
    )(a, w, res)


def _hgrn_levels(c):
    return int(math.log2(c))


def _hgrn_sum_matrix(c):
    nlev = _hgrn_levels(c)
    m = np.zeros(((nlev + 2) * c, c), np.float32)
    for l in range(nlev):
        h = c >> (l + 1)
        for r in range(c):
            mid = (r // (2 * h)) * 2 * h + h
            if r >= mid:
                m[l * c + r, mid:r + 1] = 1.0
            else:
                m[l * c + r, r + 1:mid] = 1.0
    for r in range(c):
        m[nlev * c + r, :r + 1] = 1.0
        m[(nlev + 1) * c + r, r + 1:] = 1.0
    return m


def _hgrn_kernel(x_ref, ng_ref, w_ref, lbl_ref, og_ref, m_ref, o_ref, xn_ref, y_ref, st_all_ref, *,
                 layer, chunk, n_chunks):
    c = chunk
    nlev = _hgrn_levels(c)
    head = pl.program_id(2)
    st_ref = st_all_ref.at[head]

    @pl.when(head == 0)
    def _():
        xn_ref[...] = (_rms(x_ref[0]) * ng_ref[...]).astype(xn_ref.dtype)

    @pl.when(pl.program_id(1) == 0)
    def _():
        st_ref[...] = jnp.zeros_like(st_ref)

    y_ref[...] = jnp.dot(xn_ref[...], w_ref[...], preferred_element_type=F32)

    lg = lbl_ref[...]
    e = jnp.exp(lg - jnp.max(lg, axis=0, keepdims=True))
    lb = jnp.sum(e[:layer + 1], axis=0, keepdims=True) / jnp.sum(e, axis=0, keepdims=True)
    og = og_ref[...]
    row = lax.broadcasted_iota(jnp.int32, (c, c), 0)
    col = lax.broadcasted_iota(jnp.int32, (c, c), 1)

    def body(ci, carry, *, mild):
        sl = pl.ds(pl.multiple_of(ci * c, c), c)
        q = y_ref[sl, 0:HEAD]
        fr = y_ref[sl, HEAD:2 * HEAD]
        v = y_ref[sl, 2 * HEAD:3 * HEAD]
        gt = y_ref[sl, 3 * HEAD:4 * HEAD]
        qa = q * _sigmoid(q)
        f = lb + (1.0 - lb) * _sigmoid(fr)
        k = 1.0 - f
        logf = jnp.log(f)
        hi = logf.astype(BF16)
        lo = (logf - hi.astype(F32)).astype(BF16)
        g2 = jnp.concatenate([hi, lo], axis=1)
        if mild:
            e2 = jnp.dot(m_ref[nlev * c:(nlev + 1) * c, :], g2, preferred_element_type=F32)
            b_incl = e2[:, :HEAD] + e2[:, HEAD:]
            dmid = b_incl - b_incl[c // 2 - 1:c // 2, :]
            p = lax.dot_general((qa * jnp.exp(dmid)).astype(BF16),
                                (k * jnp.exp(-dmid)).astype(BF16), NT_DIMS,
                                preferred_element_type=F32)
            s = jnp.where(row >= col, p, 0.0)
            rev = b_incl[c - 1:c, :] - b_incl
        else:
            e2 = jnp.dot(m_ref[...], g2, preferred_element_type=F32)
            ex = e2[:, :HEAD] + e2[:, HEAD:]
            s = jnp.where(row == col,
                          lax.dot_general(qa.astype(BF16), k.astype(BF16), NT_DIMS,
                                          preferred_element_type=F32),
                          0.0)
            for l in range(nlev):
                sh = nlev - 1 - l
                x = jnp.exp(ex[l * c:(l + 1) * c])
                p = lax.dot_general((qa * x).astype(BF16), (k * x).astype(BF16), NT_DIMS,
                                    preferred_element_type=F32)
                mask = (((row >> (sh + 1)) == (col >> (sh + 1)))
                        & (((row >> sh) & 1) == 1) & (((col >> sh) & 1) == 0))
                s = jnp.where(mask, p, s)
            b_incl = ex[nlev * c:(nlev + 1) * c]
            rev = ex[(nlev + 1) * c:(nlev + 2) * c]
        vb = v.astype(BF16)
        intra = jnp.dot(s.astype(BF16), vb, preferred_element_type=F32)
        st = st_ref[...]
        inter = lax.dot_general((qa * jnp.exp(b_incl)).astype(BF16), st.astype(BF16), NT_DIMS,
                                preferred_element_type=F32)
        o = inter + intra
        y = _rms(o) * og * (gt * _sigmoid(gt))
        o_ref[0, sl, :] = y.astype(o_ref.dtype)

        k2 = (k * jnp.exp(rev)).astype(BF16)
        upd = jnp.dot(v.T.astype(BF16), k2, preferred_element_type=F32)
        st_ref[...] = st * jnp.exp(b_incl[c - 1:c, :]) + upd
        return carry

    mild = jnp.min(lb) >= math.exp(-HG_SAFE_EXPONENT / (c // 2))

    @pl.when(mild)
    def _():
        lax.fori_loop(0, n_chunks, functools.partial(body, mild=True), 0, unroll=4)

    @pl.when(jnp.logical_not(mild))
    def _():
        lax.fori_loop(0, n_chunks, functools.partial(body, mild=False), 0, unroll=4)


def hgrn_mixer(x, norm_gain, w_in, lb_logits, out_gain, *, layer, seq_block):
    b, s, d = x.shape
    nh = d // HEAD
    c = min(HG_CHUNK, seq_block)
    msel = jnp.asarray(_hgrn_sum_matrix(c), BF16)
    nl = lb_logits.shape[0]
    w_heads = (w_in.astype(BF16).reshape(d, 4, nh, HEAD).transpose(0, 2, 1, 3)
               .reshape(d, nh * 4 * HEAD))
    kern = functools.partial(_hgrn_kernel, layer=layer, chunk=c, n_chunks=seq_block // c)
    return pl.pallas_call(
        kern,
        grid=(b, s // seq_block, nh),
        in_specs=[pl.BlockSpec((1, seq_block, d), lambda bi, si, hi: (bi, si, 0)),
                  pl.BlockSpec((1, d), lambda bi, si, hi: (0, 0)),
                  pl.BlockSpec((d, 4 * HEAD), lambda bi, si, hi: (0, hi)),
                  pl.BlockSpec((nl, HEAD), lambda bi, si, hi: (0, hi)),
                  pl.BlockSpec((1, HEAD), lambda bi, si, hi: (0, 0)),
                  pl.BlockSpec(msel.shape, lambda bi, si, hi: (0, 0))],
        out_specs=pl.BlockSpec((1, seq_block, HEAD), lambda bi, si, hi: (bi, si, hi)),
        out_shape=jax.ShapeDtypeStruct((b, s, d), BF16),
        scratch_shapes=[pltpu.VMEM((seq_block, d), BF16), pltpu.VMEM((seq_block, 4 * HEAD), F32),
                        pltpu.VMEM((nh, HEAD, HEAD), F32)],
        compiler_params=_cparams(("parallel", "arbitrary", "arbitrary")),
        name="hgrn_mixer",
    )(x, norm_gain.reshape(1, d), w_heads, lb_logits, out_gain.reshape(1, HEAD), msel)


def _peer_prep_kernel(h_ref, g_ref, wq_ref, keys_ref, xn_ref, s1_ref, s2_ref):
    xn = (_rms(h_ref[...]) * g_ref[...]).astype(BF16)
    xn_ref[...] = xn
    q = jnp.dot(xn, wq_ref[...], preferred_element_type=F32)
    k0 = keys_ref[0].astype(BF16)
    k1 = keys_ref[1].astype(BF16)
    for hh in range(PEER_HEADS):
        base = hh * 2 * HEAD
        q1 = q[:, base:base + HEAD].astype(BF16)
        q2 = q[:, base + HEAD:base + 2 * HEAD].astype(BF16)
        s1_ref[hh] = lax.dot_general(k0, q1, NT_DIMS, preferred_element_type=F32)
        s2_ref[hh] = lax.dot_general(k1, q2, NT_DIMS, preferred_element_type=F32)


def peer_prep(h, gain, wq, keys, *, tm):
    t, d = h.shape
    nq = wq.shape[1]
    return pl.pallas_call(
        _peer_prep_kernel,
        grid=(t // tm,),
        in_specs=[pl.BlockSpec((tm, d), lambda i: (i, 0)),
                  pl.BlockSpec((1, d), lambda i: (0, 0)),
                  pl.BlockSpec((d, nq), lambda i: (0, 0)),
                  pl.BlockSpec(keys.shape, lambda i: (0, 0, 0))],
        out_specs=[pl.BlockSpec((tm, d), lambda i: (i, 0)),
                   pl.BlockSpec((PEER_HEADS, N_KEYS, tm), lambda i: (0, 0, i)),
                   pl.BlockSpec((PEER_HEADS, N_KEYS, tm), lambda i: (0, 0, i))],
        out_shape=[jax.ShapeDtypeStruct((t, d), BF16),
                   jax.ShapeDtypeStruct((PEER_HEADS, N_KEYS, t), F32),
                   jax.ShapeDtypeStruct((PEER_HEADS, N_KEYS, t), F32)],
        compiler_params=_cparams(("parallel",)),
        name="peer_prep",
    )(h, gain.reshape(1, d), wq, keys)


def _oddeven_merge(lo, hi, r):
    step = r * 2
    if step < hi - lo:
        yield from _oddeven_merge(lo, hi, step)
        yield from _oddeven_merge(lo + r, hi, step)
        yield from [(i, i + r) for i in range(lo + r, hi - r, step)]
    else:
        yield (lo, lo + r)


def _oddeven_sort(lo, hi):
    if hi - lo >= 1:
        mid = lo + (hi - lo) // 2
        yield from _oddeven_sort(lo, mid)
        yield from _oddeven_sort(mid + 1, hi)
        yield from _oddeven_merge(lo, hi, 1)


SORT16 = tuple(_oddeven_sort(0, PEER_TOPK - 1))
BITONIC16 = tuple((i, i + h) for h in (8, 4, 2, 1) for i in range(PEER_TOPK) if i % (2 * h) < h)
SUBLANES = 8


def _compare_exchange(v, pairs):
    v = list(v)
    for i, j in pairs:
        v[i], v[j] = jnp.maximum(v[i], v[j]), jnp.minimum(v[i], v[j])
    return v


def _merge_sublane_lists(v):
    n = len(v)
    for shift in (4, 2, 1):
        w = [pltpu.roll(x, shift, 0) for x in v]
        v = _compare_exchange([jnp.maximum(v[k], w[n - 1 - k]) for k in range(n)], BITONIC16)
    return v


def _top16(x):
    slabs = [x[SUBLANES * k:SUBLANES * (k + 1), :] for k in range(PEER_TOPK)]
    return _merge_sublane_lists(_compare_exchange(slabs, SORT16))


def _peer_topk_kernel(s1_ref, s2_ref, cnt_ref, ar_ref, rk_ref, bt_ref):
    k = PEER_TOPK
    lanes = s1_ref.shape[2]
    sub = lax.broadcasted_iota(jnp.int32, (SUBLANES, lanes), 0)

    def head(hh, carry):
        s1 = s1_ref[hh]
        s2 = s2_ref[hh]
        va = _top16(s1)
        vb = _top16(s2)
        bd = vb[0]
        for p in range(1, SUBLANES):
            bd = jnp.where(sub == p, vb[p], bd)
        cand = []
        for i in range(k):
            c = va[i] + bd
            lim = k // (i + 1)
            cand.append(c if lim >= SUBLANES else jnp.where(sub < lim, c, -jnp.inf))
        tail = [va[0] + vb[SUBLANES + q] for q in range(k - SUBLANES)]
        top16 = _merge_sublane_lists(cand)
        tau = top16[k - SUBLANES - 1]
        for q in range(k - SUBLANES):
            tau = jnp.minimum(tau, jnp.maximum(top16[k - 1 - q], tail[q]))
        top = va[0] + vb[0]
        zc = jnp.zeros((SUBLANES, lanes), F32)
        for c in cand:
            zc = zc + jnp.where(c >= tau, jnp.exp(c - top), 0.0)
        zt = jnp.zeros((SUBLANES, lanes), F32)
        for c in tail:
            zt = zt + jnp.where(c >= tau, jnp.exp(c - top), 0.0)
        z = jnp.sum(zc, axis=0, keepdims=True) + zt[0:1, :]
        tau_row = tau[0:1, :]
        b_rows = [x[0:1, :] for x in vb]
        cnt = jnp.zeros(s1.shape, F32)
        rank2 = jnp.zeros(s2.shape, F32)
        for j in range(k):
            cnt = jnp.where(s1 + b_rows[j] >= tau_row, float(j + 1), cnt)
            rank2 = jnp.where(b_rows[j] > s2, float(j + 1), rank2)
        cnt_ref[hh] = cnt
        ar_ref[hh] = jnp.exp(s1 - va[0][0:1, :]) / z
        bt = jnp.exp(s2 - b_rows[0])
        for lb in range(lanes // LANES):
            ln = slice(lb * LANES, (lb + 1) * LANES)
            rk_ref[hh, lb] = pltpu.bitcast(rank2[:, ln].astype(BF16), jnp.uint32)
            bt_ref[hh, lb] = pltpu.bitcast(bt[:, ln].astype(BF16), jnp.uint32)
        return carry

    lax.fori_loop(0, PEER_HEADS, head, 0)


def peer_topk(s1t, s2t, *, tt):
    nh, nk, t = s1t.shape
    spec = pl.BlockSpec((nh, nk, tt), lambda i: (0, 0, i))
    shp = jax.ShapeDtypeStruct((nh, nk, t), F32)
    tspec = pl.BlockSpec((nh, tt // LANES, nk // 2, LANES), lambda i: (0, i, 0, 0))
    tshp = jax.ShapeDtypeStruct((nh, t // LANES, nk // 2, LANES), jnp.uint32)
    return pl.pallas_call(
        _peer_topk_kernel,
        grid=(t // tt,),
        in_specs=[spec, spec],
        out_specs=[spec, spec, tspec, tspec],
        out_shape=[shp, shp, tshp, tshp],
        compiler_params=_cparams(("parallel",)),
        name="peer_topk",
    )(s1t, s2t)


GELU_C0 = math.sqrt(2.0 / math.pi)
GELU_C1 = GELU_C0 * 0.044715


def _gelu(z):
    return 0.5 * z * (1.0 + jnp.tanh(z * (GELU_C0 + GELU_C1 * (z * z))))


def _peer_main_kernel(h_ref, xn_ref, u_ref, vt_ref, cnt_ref, ar_ref, rk_ref, bt_ref, o_ref,
                      acc_ref, zt_ref, a_ref, *, n_blocks, blocks_per_tile, lane_blocks_per_iter):
    g_step = pl.program_id(0)
    et, tt = zt_ref.shape
    n_e1 = et // N_KEYS
    jb = jnp.maximum(g_step - 1, 0) % blocks_per_tile

    @pl.when(g_step == 0)
    def _():
        a_ref[...] = jnp.zeros_like(a_ref)

    @pl.when(jb == 0)
    def _():
        acc_ref[...] = jnp.zeros_like(acc_ref)

    acc_ref[...] += jnp.dot(vt_ref[...], a_ref[...], preferred_element_type=F32)
    zt_ref[...] = lax.dot_general(u_ref[...].astype(BF16), xn_ref[...], NT_DIMS,
                                  preferred_element_type=F32)

    @pl.when((g_step >= 1) & (jb == blocks_per_tile - 1))
    def _():
        o_ref[...] = h_ref[...] + acc_ref[...].T

    def lane_group(li, carry):
        for sub in range(lane_blocks_per_iter):
            lb = li * lane_blocks_per_iter + sub
            ln = pl.ds(pl.multiple_of(lb * LANES, LANES), LANES)
            for c in range(n_e1):
                rows = slice(c * N_KEYS, (c + 1) * N_KEYS)
                g = jnp.zeros((N_KEYS, LANES), BF16)
                for hh in range(PEER_HEADS):
                    cnt = cnt_ref[hh, c:c + 1, ln].astype(BF16)
                    ar = ar_ref[hh, c:c + 1, ln].astype(BF16)
                    rk = pltpu.bitcast(rk_ref[hh, lb], BF16)
                    bt = pltpu.bitcast(bt_ref[hh, lb], BF16)
                    g = g + jnp.where(rk < cnt, bt * ar, 0)
                a_ref[rows, ln] = _gelu(zt_ref[rows, ln].astype(BF16)) * g
        return carry

    lax.fori_loop(0, tt // (LANES * lane_blocks_per_iter), lane_group, 0)


def _transpose_cast_kernel(x_ref, o_ref):
    o_ref[...] = x_ref[...].T.astype(o_ref.dtype)


def transpose_cast(stack, layer, *, rows):
    _, e, d = stack.shape
    return pl.pallas_call(
        _transpose_cast_kernel,
        grid=(e // rows,),
        in_specs=[pl.BlockSpec((None, rows, d), lambda j: (layer, j, 0))],
        out_specs=pl.BlockSpec((d, rows), lambda j: (0, j)),
        out_shape=jax.ShapeDtypeStruct((d, e), BF16),
        compiler_params=_cparams(("parallel",)),
        name="transpose_cast",
    )(stack)


def peer_main(h, xn, u_stack, layer, vt, cnt, ar, rank_tiles, bt_tiles, *, tt, et,
              lane_blocks_per_iter):
    t, d = h.shape
    ne = u_stack.shape[1]
    n_e1 = et // N_KEYS
    bpt = ne // et
    n_blocks = (t // tt) * bpt
    kern = functools.partial(_peer_main_kernel, n_blocks=n_blocks, blocks_per_tile=bpt,
                             lane_blocks_per_iter=lane_blocks_per_iter)

    def cur(g):
        return jnp.minimum(g, n_blocks - 1)

    def prev(g):
        return jnp.maximum(g - 1, 0)

    row_spec = pl.BlockSpec((PEER_HEADS, n_e1, tt), lambda g: (0, cur(g) % bpt, cur(g) // bpt))
    tok_spec = pl.BlockSpec((PEER_HEADS, tt // LANES, N_KEYS // 2, LANES),
                            lambda g: (0, cur(g) // bpt, 0, 0))
    return pl.pallas_call(
        kern,
        grid=(n_blocks + 1,),
        in_specs=[pl.BlockSpec((tt, d), lambda g: (prev(g) // bpt, 0)),
                  pl.BlockSpec((tt, d), lambda g: (cur(g) // bpt, 0)),
                  pl.BlockSpec((None, et, d), lambda g: (layer, cur(g) % bpt, 0)),
                  pl.BlockSpec((d, et), lambda g: (0, prev(g) % bpt)),
                  row_spec, row_spec, tok_spec, tok_spec],
        out_specs=pl.BlockSpec((tt, d), lambda g: (prev(g) // bpt, 0)),
        out_shape=jax.ShapeDtypeStruct((t, d), F32),
        scratch_shapes=[pltpu.VMEM((d, tt), F32), pltpu.VMEM((et, tt), F32),
                        pltpu.VMEM((et, tt), BF16)],
        compiler_params=_cparams(("arbitrary",)),
        name="peer_main",
    )(h, xn, u_stack, vt, cnt, ar, rank_tiles, bt_tiles)


def peer_layer(h, gain, wq, keys, u_stack, v_stack, layer):
    t = h.shape[0]
    xn, s1t, s2t = peer_prep(h, gain, wq.astype(BF16), keys, tm=min(1024, t))
    cnt, ar, rank_tiles, bt_tiles = peer_topk(s1t, s2t, tt=min(256, t))
    vt = transpose_cast(v_stack, layer, rows=512)
    return peer_main(h, xn, u_stack, layer, vt, cnt, ar, rank_tiles, bt_tiles,
                     tt=min(512, t), et=2048, lane_blocks_per_iter=1)


def _rope_tables(s):
    half = ROT_DIM // 2
    inv = ROPE_THETA ** (-jnp.arange(half, dtype=F32) * 2.0 / ROT_DIM)
    ang = jnp.arange(s).astype(F32)[:, None] * inv[None, :]
    cos, sin = jnp.cos(ang), jnp.sin(ang)
    pad = HEAD - ROT_DIM
    ones = jnp.ones((s, pad), F32)
    zeros = jnp.zeros((s, pad), F32)
    zh = jnp.zeros((s, half), F32)
    ct = jnp.concatenate([cos, cos, ones], axis=1)
    at = jnp.concatenate([-sin, zh, zeros], axis=1)
    bt = jnp.concatenate([zh, sin, zeros], axis=1)
    return ct, at, bt


ATTN_QG = KV_PER_GROUP * Q_PER_KV * HEAD
ATTN_KG = KV_PER_GROUP * HEAD


def _attn_proj_kernel(h_ref, kvg_ref, ag_ref, wkv_ref, wq_ref, kn_ref, qn_ref,
                      ct_ref, at_ref, bt_ref, *refs):
    out_refs, (qs_ref, kvs_ref) = refs[:-2], refs[-2:]
    y = _rms(h_ref[...])
    kv = jnp.dot((y * kvg_ref[...]).astype(BF16), wkv_ref[...], preferred_element_type=F32)
    qq = jnp.dot((y * ag_ref[...]).astype(BF16), wq_ref[...], preferred_element_type=F32)
    ct, at, bt = ct_ref[...], at_ref[...], bt_ref[...]
    half = ROT_DIM // 2

    def head_norm_rope(x, gain):
        n = _rms(x) * gain
        return n * ct + pltpu.roll(n, HEAD - half, 1) * at + pltpu.roll(n, half, 1) * bt

    n_kh = kv.shape[1] // (2 * HEAD)
    for hd in range(n_kh):
        cs = slice(hd * HEAD, (hd + 1) * HEAD)
        kvs_ref[hd] = head_norm_rope(kv[:, cs], kn_ref[...])
        kvs_ref[n_kh + hd] = kv[:, n_kh * HEAD + hd * HEAD:n_kh * HEAD + (hd + 1) * HEAD]
    for hd in range(qq.shape[1] // HEAD):
        qs_ref[hd] = head_norm_rope(qq[:, hd * HEAD:(hd + 1) * HEAD], qn_ref[...])

    tm = qs_ref.shape[1]
    q_heads, k_heads = ATTN_QG // HEAD, ATTN_KG // HEAD
    for g, (_, dil) in enumerate(DIL_GROUPS):
        q_ref, k_ref, v_ref = out_refs[3 * g:3 * g + 3]
        for r in range(dil):
            rows = pl.ds(r, tm // dil, stride=dil) if dil > 1 else slice(None)
            for hd in range(q_heads):
                q_ref[:, (r * q_heads + hd) * HEAD:(r * q_heads + hd + 1) * HEAD] = (
                    qs_ref[g * q_heads + hd, rows, :].astype(q_ref.dtype))
            for hd in range(k_heads):
                cols = slice((r * k_heads + hd) * HEAD, (r * k_heads + hd + 1) * HEAD)
                k_ref[:, cols] = kvs_ref[g * k_heads + hd, rows, :].astype(k_ref.dtype)
                v_ref[:, cols] = kvs_ref[n_kh + g * k_heads + hd, rows, :].astype(v_ref.dtype)


def attn_proj(h, kv_gain, a_gain, wkv, wq, k_norm, q_norm, seq, *, tm):
    t, d = h.shape
    ct, at, bt = _rope_tables(seq)
    ns = seq // tm
    tab = pl.BlockSpec((tm, HEAD), lambda i: (i % ns, 0))
    vec = lambda n: pl.BlockSpec((1, n), lambda i: (0, 0))
    out_specs, out_shape = [], []
    for _, dil in DIL_GROUPS:
        for width in (ATTN_QG, ATTN_KG, ATTN_KG):
            out_specs.append(pl.BlockSpec((tm // dil, dil * width), lambda i: (i, 0)))
            out_shape.append(jax.ShapeDtypeStruct((t // dil, dil * width), BF16))
    outs = pl.pallas_call(
        _attn_proj_kernel,
        grid=(t // tm,),
        in_specs=[pl.BlockSpec((tm, d), lambda i: (i, 0)), vec(d), vec(d),
                  pl.BlockSpec(wkv.shape, lambda i: (0, 0)),
                  pl.BlockSpec(wq.shape, lambda i: (0, 0)),
                  vec(HEAD), vec(HEAD), tab, tab, tab],
        out_specs=out_specs,
        out_shape=out_shape,
        scratch_shapes=[pltpu.VMEM((wq.shape[1] // HEAD, tm, HEAD), F32),
                        pltpu.VMEM((wkv.shape[1] // HEAD, tm, HEAD), F32)],
        compiler_params=_cparams(("parallel",)),
        name="attn_proj",
    )(h, kv_gain.reshape(1, d), a_gain.reshape(1, d), wkv, wq,
      k_norm.reshape(1, HEAD), q_norm.reshape(1, HEAD), ct, at, bt)
    return [tuple(outs[3 * g:3 * g + 3]) for g in range(len(DIL_GROUPS))]


ATTN_EXP2_SCALE = HEAD ** -0.5 * math.log2(math.e)
ATTN_SUB = 128


def _band_bias(first):
    rq = np.arange(Q_PER_KV * ATTN_SUB)[:, None] % ATTN_SUB
    ck = np.arange(2 * ATTN_SUB)[None, :]
    diff = rq - ck if first else rq + ATTN_SUB - ck
    return np.where((diff >= 0) & (diff <= ATTN_SUB), 0.0, -np.inf).astype(np.float32)


def _attn_group_kernel(q_ref, k_ref, v_ref, bfirst_ref, bband_ref, o_ref, lse_ref, *, qb):
    nb = pl.program_id(3)
    scale = HEAD ** -0.5
    for sb in range(qb // ATTN_SUB):
        rows = slice(sb * ATTN_SUB, (sb + 1) * ATTN_SUB)
        n0 = nb * qb + sb * ATTN_SUB
        ks = pl.multiple_of(jnp.maximum(n0 - ATTN_SUB, 0), ATTN_SUB)
        q2 = jnp.concatenate([q_ref[0, rows, r * HEAD:(r + 1) * HEAD] for r in range(Q_PER_KV)],
                             axis=0)
        s = lax.dot_general(q2, k_ref[0, pl.ds(ks, 2 * ATTN_SUB), :], NT_DIMS,
                            preferred_element_type=F32)
        s = s + jnp.where(n0 == 0, bfirst_ref[...], bband_ref[...])
        m = jnp.max(s, axis=-1, keepdims=True)
        p = jnp.exp2((s - m) * ATTN_EXP2_SCALE)
        den = jnp.sum(p, axis=-1, keepdims=True)
        out = jnp.dot(p.astype(BF16), v_ref[0, pl.ds(ks, 2 * ATTN_SUB), :],
                      preferred_element_type=F32) / den
        lse = m * scale + jnp.log(den)
        for r in range(Q_PER_KV):
            part = slice(r * ATTN_SUB, (r + 1) * ATTN_SUB)
            o_ref[0, rows, r * HEAD:(r + 1) * HEAD] = out[part].astype(o_ref.dtype)
            lse_ref[0, rows, r * HEAD:(r + 1) * HEAD] = jnp.broadcast_to(lse[part],
                                                                          (ATTN_SUB, HEAD))


def attn_group(q, k, v, g, batch, *, qb):
    w, dil = DIL_GROUPS[g]
    assert w == ATTN_SUB * dil
    sub = q.shape[0] // batch
    qb = min(qb, sub)
    qcols = Q_PER_KV * HEAD
    kern = functools.partial(_attn_group_kernel, qb=qb)
    const = lambda bi, r, kv, nb: (0, 0)
    q_map = lambda bi, r, kv, nb: (bi, nb, r * KV_PER_GROUP + kv)
    kv_map = lambda bi, r, kv, nb: (bi, 0, r * KV_PER_GROUP + kv)
    o, lse = pl.pallas_call(
        kern,
        grid=(batch, dil, KV_PER_GROUP, sub // qb),
        in_specs=[pl.BlockSpec((1, qb, qcols), q_map),
                  pl.BlockSpec((1, sub, HEAD), kv_map),
                  pl.BlockSpec((1, sub, HEAD), kv_map),
                  pl.BlockSpec((Q_PER_KV * ATTN_SUB, 2 * ATTN_SUB), const),
                  pl.BlockSpec((Q_PER_KV * ATTN_SUB, 2 * ATTN_SUB), const)],
        out_specs=[pl.BlockSpec((1, qb, qcols), q_map), pl.BlockSpec((1, qb, qcols), q_map)],
        out_shape=[jax.ShapeDtypeStruct((batch, sub, dil * ATTN_QG), BF16),
                   jax.ShapeDtypeStruct((batch, sub, dil * ATTN_QG), F32)],
        compiler_params=_cparams(("parallel", "parallel", "parallel", "arbitrary")),
        name="attn_group",
    )(q.reshape(batch, sub, -1), k.reshape(batch, sub, -1), v.reshape(batch, sub, -1),
      jnp.asarray(_band_bias(True)), jnp.asarray(_band_bias(False)))
    return o.reshape(batch * sub, -1), lse.reshape(batch * sub, -1)


def _combine_proj_kernel(o0_ref, o1_ref, o2_ref, l0_ref, l1_ref, l2_ref, w_ref, r_ref, out_ref,
                         osc_ref, lsc_ref):
    tm = out_ref.shape[0]
    ng = len(DIL_GROUPS)
    n_heads = ATTN_QG // HEAD
    for g, (o_ref, l_ref) in enumerate(zip((o0_ref, o1_ref, o2_ref), (l0_ref, l1_ref, l2_ref))):
        dil = DIL_GROUPS[g][1]
        for r in range(dil):
            rows = pl.ds(r, tm // dil, stride=dil) if dil > 1 else slice(None)
            for hd in range(n_heads):
                cols = slice((r * n_heads + hd) * HEAD, (r * n_heads + hd + 1) * HEAD)
                osc_ref[g * n_heads + hd, rows, :] = o_ref[:, cols].astype(F32)
                lsc_ref[g * n_heads + hd, rows, :] = l_ref[:, cols]
    heads = []
    for hd in range(n_heads):
        lses = [lsc_ref[g * n_heads + hd] for g in range(ng)]
        m = functools.reduce(jnp.maximum, lses)
        ws = [jnp.exp(x - m) for x in lses]
        num = functools.reduce(lambda a, b: a + b,
                               [ws[g] * osc_ref[g * n_heads + hd] for g in range(ng)])
        heads.append((num / functools.reduce(lambda a, b: a + b, ws)).astype(BF16))
    comb = jnp.concatenate(heads, axis=1)
    out_ref[...] = r_ref[...] + jnp.dot(comb, w_ref[...], preferred_element_type=F32)


def combine_proj(outs, lses, w, res, *, tm):
    t, n = res.shape
    k = w.shape[0]
    ng = len(DIL_GROUPS)
    packed = [pl.BlockSpec((tm // dil, dil * k), lambda i: (i, 0)) for _, dil in DIL_GROUPS]
    return pl.pallas_call(
        _combine_proj_kernel,
        grid=(t // tm,),
        in_specs=packed + packed + [pl.BlockSpec((k, n), lambda i: (0, 0)),
                                    pl.BlockSpec((tm, n), lambda i: (i, 0))],
        out_specs=pl.BlockSpec((tm, n), lambda i: (i, 0)),
        out_shape=jax.ShapeDtypeStruct((t, n), F32),
        scratch_shapes=[pltpu.VMEM((ng * k // HEAD, tm, HEAD), F32),
                        pltpu.VMEM((ng * k // HEAD, tm, HEAD), F32)],
        compiler_params=_cparams(("parallel",)),
        name="combine_proj",
    )(*outs, *lses, w, res)


def kernel(x, hgrn_norm, hgrn_w_in, hgrn_lb_logits, hgrn_out_norm, hgrn_w_out, kv_norm, w_kv,
           k_norm, attn_norm, w_q, q_norm, w_o, ffn_norm, peer_w_q, peer_sub_keys, peer_u, peer_v):
    b, s, d = x.shape
    t = b * s
    depth = ffn_norm.shape[0]
    n_a = hgrn_norm.shape[0]
    h = x.reshape(t, d)
    tm = min(1024, t)
    for layer in range(depth):
        if layer < n_a:
            o = hgrn_mixer(h.reshape(b, s, d), hgrn_norm[layer], hgrn_w_in[layer], hgrn_lb_logits,
                           hgrn_out_norm[layer], layer=layer, seq_block=min(2048, s))
            h = matmul_residual(o.reshape(t, d), hgrn_w_out[layer].astype(BF16), h, tm=tm)
        else:
            j = layer - n_a
            qkv = attn_proj(h, kv_norm, attn_norm[j], w_kv.astype(BF16), w_q[j].astype(BF16),
                            k_norm, q_norm[j], s, tm=min(256, s))
            if j == 0:
                kv_sh = [(kg, vg) for _, kg, vg in qkv]
            parts = [attn_group(qkv[g][0], kv_sh[g][0], kv_sh[g][1], g, b, qb=2048)
                     for g in range(len(DIL_GROUPS))]
            h = combine_proj([o for o, _ in parts], [l for _, l in parts],
                             w_o[j].astype(BF16), h, tm=min(512, t))
        h = peer_layer(h, ffn_norm[layer], peer_w_q[layer], peer_sub_keys[layer],
                       peer_u, peer_v, layer)
    return h.reshape(b, s, d)
```

```python
import functools
import math

import numpy as np
import jax
import jax.numpy as jnp
from jax import lax
from jax.experimental import pallas as pl
from jax.experimental.pallas import tpu as pltpu

F32 = jnp.float32
BF16 = jnp.bfloat16
EPS = 1e-6

LANES = 128
HEAD = 128
HG_CHUNK = 128
HG_SAFE_EXPONENT = 80.0
DIL_GROUPS = ((128, 1), (512, 4), (2048, 16))
KV_PER_GROUP = 2
Q_PER_KV = 2
ROT_DIM = HEAD // 4
ROPE_THETA = 500000.0
PEER_HEADS = 8
PEER_TOPK = 16
N_KEYS = 128
VMEM_LIMIT = 56 * 1024 * 1024

NT_DIMS = (((1,), (1,)), ((), ()))


def _cparams(sem):
    return pltpu.CompilerParams(dimension_semantics=sem, vmem_limit_bytes=VMEM_LIMIT)


def _rms(x):
    return x * lax.rsqrt(jnp.mean(x * x, axis=-1, keepdims=True) + EPS)


def _sigmoid(x):
    return 1.0 / (1.0 + jnp.exp(-x))


def _matmul_res_kernel(a_ref, w_ref, r_ref, o_ref):
    o_ref[...] = r_ref[...] + jnp.dot(a_ref[...], w_ref[...], preferred_element_type=F32)


def matmul_residual(a, w, res, *, tm):
    t, k = a.shape
    n = w.shape[1]
    return pl.pallas_call(
        _matmul_res_kernel,
        grid=(t // tm,),
        in_specs=[pl.BlockSpec((tm, k), lambda i: (i, 0)),
                  pl.BlockSpec((k, n), lambda i: (0, 0)),
                  pl.BlockSpec((tm, n), lambda i: (i, 0))],
        out_specs=pl.BlockSpec((tm, n), lambda i: (i, 0)),
        out_shape=jax.ShapeDtypeStruct((t, n), F32),
        compiler_params=_cparams(("parallel",)),
        name="matmul_residual",
    )(a, w, res)


def _hgrn_levels(c):
    return int(math.log2(c))


def _hgrn_sum_matrix(c):
    nlev = _hgrn_levels(c)
    m = np.zeros(((nlev + 2) * c, c), np.float32)
    for l in range(nlev):
        h = c >> (l + 1)
        for r in range(c):
            mid = (r // (2 * h)) * 2 * h + h
            if r >= mid:
                m[l * c + r, mid:r + 1] = 1.0
            else:
                m[l * c + r, r + 1:mid] = 1.0
    for r in range(c):
        m[nlev * c + r, :r + 1] = 1.0
        m[(nlev + 1) * c + r, r + 1:] = 1.0
    return m


def _hgrn_kernel(x_ref, ng_ref, w_ref, lbl_ref, og_ref, m_ref, o_ref, xn_ref, y_ref, st_all_ref, *,
                 layer, chunk, n_chunks):
    c = chunk
    nlev = _hgrn_levels(c)
    head = pl.program_id(2)
    st_ref = st_all_ref.at[head]

    @pl.when(head == 0)
    def _():
        xn_ref[...] = (_rms(x_ref[0]) * ng_ref[...]).astype(xn_ref.dtype)

    @pl.when(pl.program_id(1) == 0)
    def _():
        st_ref[...] = jnp.zeros_like(st_ref)

    y_ref[...] = jnp.dot(xn_ref[...], w_ref[...], preferred_element_type=F32)

    lg = lbl_ref[...]
    e = jnp.exp(lg - jnp.max(lg, axis=0, keepdims=True))
    lb = jnp.sum(e[:layer + 1], axis=0, keepdims=True) / jnp.sum(e, axis=0, keepdims=True)
    og = og_ref[...]
    row = lax.broadcasted_iota(jnp.int32, (c, c), 0)
    col = lax.broadcasted_iota(jnp.int32, (c, c), 1)

    def body(ci, carry, *, mild):
        sl = pl.ds(pl.multiple_of(ci * c, c), c)
        q = y_ref[sl, 0:HEAD]
        fr = y_ref[sl, HEAD:2 * HEAD]
        v = y_ref[sl, 2 * HEAD:3 * HEAD]
        gt = y_ref[sl, 3 * HEAD:4 * HEAD]
        qa = q * _sigmoid(q)
        f = lb + (1.0 - lb) * _sigmoid(fr)
        k = 1.0 - f
        logf = jnp.log(f)
        hi = logf.astype(BF16)
        lo = (logf - hi.astype(F32)).astype(BF16)
        g2 = jnp.concatenate([hi, lo], axis=1)
        if mild:
            e2 = jnp.dot(m_ref[nlev * c:(nlev + 1) * c, :], g2, preferred_element_type=F32)
            b_incl = e2[:, :HEAD] + e2[:, HEAD:]
            dmid = b_incl - b_incl[c // 2 - 1:c // 2, :]
            p = lax.dot_general((qa * jnp.exp(dmid)).astype(BF16),
                                (k * jnp.exp(-dmid)).astype(BF16), NT_DIMS,
                                preferred_element_type=F32)
            s = jnp.where(row >= col, p, 0.0)
            rev = b_incl[c - 1:c, :] - b_incl
        else:
            e2 = jnp.dot(m_ref[...], g2, preferred_element_type=F32)
            ex = e2[:, :HEAD] + e2[:, HEAD:]
            s = jnp.where(row == col,
                          lax.dot_general(qa.astype(BF16), k.astype(BF16), NT_DIMS,
                                          preferred_element_type=F32),
                          0.0)
            for l in range(nlev):
                sh = nlev - 1 - l
                x = jnp.exp(ex[l * c:(l + 1) * c])
                p = lax.dot_general((qa * x).astype(BF16), (k * x).astype(BF16), NT_DIMS,
                                    preferred_element_type=F32)
                mask = (((row >> (sh + 1)) == (col >> (sh + 1)))
                        & (((row >> sh) & 1) == 1) & (((col >> sh) & 1) == 0))
                s = jnp.where(mask, p, s)
            b_incl = ex[nlev * c:(nlev + 1) * c]
            rev = ex[(nlev + 1) * c:(nlev + 2) * c]
        vb = v.astype(BF16)
        intra = jnp.dot(s.astype(BF16), vb, preferred_element_type=F32)
        st = st_ref[...]
        inter = lax.dot_general((qa * jnp.exp(b_incl)).astype(BF16), st.astype(BF16), NT_DIMS,
                                preferred_element_type=F32)
        o = inter + intra
        y = _rms(o) * og * (gt * _sigmoid(gt))
        o_ref[0, sl, :] = y.astype(o_ref.dtype)

        k2 = (k * jnp.exp(rev)).astype(BF16)
        upd = jnp.dot(v.T.astype(BF16), k2, preferred_element_type=F32)
        st_ref[...] = st * jnp.exp(b_incl[c - 1:c, :]) + upd
        return carry

    mild = jnp.min(lb) >= math.exp(-HG_SAFE_EXPONENT / (c // 2))

    @pl.when(mild)
    def _():
        lax.fori_loop(0, n_chunks, functools.partial(body, mild=True), 0, unroll=8)

    @pl.when(jnp.logical_not(mild))
    def _():
        lax.fori_loop(0, n_chunks, functools.partial(body, mild=False), 0, unroll=4)


def hgrn_mixer(x, norm_gain, w_in, lb_logits, out_gain, *, layer, seq_block):
    b, s, d = x.shape
    nh = d // HEAD
    c = min(HG_CHUNK, seq_block)
    msel = jnp.asarray(_hgrn_sum_matrix(c), BF16)
    nl = lb_logits.shape[0]
    w_heads = (w_in.astype(BF16).reshape(d, 4, nh, HEAD).transpose(0, 2, 1, 3)
               .reshape(d, nh * 4 * HEAD))
    kern = functools.partial(_hgrn_kernel, layer=layer, chunk=c, n_chunks=seq_block // c)
    return pl.pallas_call(
        kern,
        grid=(b, s // seq_block, nh),
        in_specs=[pl.BlockSpec((1, seq_block, d), lambda bi, si, hi: (bi, si, 0)),
                  pl.BlockSpec((1, d), lambda bi, si, hi: (0, 0)),
                  pl.BlockSpec((d, 4 * HEAD), lambda bi, si, hi: (0, hi)),
                  pl.BlockSpec((nl, HEAD), lambda bi, si, hi: (0, hi)),
                  pl.BlockSpec((1, HEAD), lambda bi, si, hi: (0, 0)),
                  pl.BlockSpec(msel.shape, lambda bi, si, hi: (0, 0))],
        out_specs=pl.BlockSpec((1, seq_block, HEAD), lambda bi, si, hi: (bi, si, hi)),
        out_shape=jax.ShapeDtypeStruct((b, s, d), BF16),
        scratch_shapes=[pltpu.VMEM((seq_block, d), BF16), pltpu.VMEM((seq_block, 4 * HEAD), F32),
                        pltpu.VMEM((nh, HEAD, HEAD), F32)],
        compiler_params=_cparams(("parallel", "arbitrary", "arbitrary")),
        name="hgrn_mixer",
    )(x, norm_gain.reshape(1, d), w_heads, lb_logits, out_gain.reshape(1, HEAD), msel)


def _peer_prep_kernel(h_ref, g_ref, wq_ref, keys_ref, xn_ref, s1_ref, s2_ref):
    xn = (_rms(h_ref[...]) * g_ref[...]).astype(BF16)
    xn_ref[...] = xn
    q = jnp.dot(xn, wq_ref[...], preferred_element_type=F32)
    k0 = keys_ref[0].astype(BF16)
    k1 = keys_ref[1].astype(BF16)
    for hh in range(PEER_HEADS):
        base = hh * 2 * HEAD
        q1 = q[:, base:base + HEAD].astype(BF16)
        q2 = q[:, base + HEAD:base + 2 * HEAD].astype(BF16)
        s1_ref[hh] = lax.dot_general(k0, q1, NT_DIMS, preferred_element_type=F32)
        s2_ref[hh] = lax.dot_general(k1, q2, NT_DIMS, preferred_element_type=F32)


def peer_prep(h, gain, wq, keys, *, tm):
    t, d = h.shape
    nq = wq.shape[1]
    return pl.pallas_call(
        _peer_prep_kernel,
        grid=(t // tm,),
        in_specs=[pl.BlockSpec((tm, d), lambda i: (i, 0)),
                  pl.BlockSpec((1, d), lambda i: (0, 0)),
                  pl.BlockSpec((d, nq), lambda i: (0, 0)),
                  pl.BlockSpec(keys.shape, lambda i: (0, 0, 0))],
        out_specs=[pl.BlockSpec((tm, d), lambda i: (i, 0)),
                   pl.BlockSpec((PEER_HEADS, N_KEYS, tm), lambda i: (0, 0, i)),
                   pl.BlockSpec((PEER_HEADS, N_KEYS, tm), lambda i: (0, 0, i))],
        out_shape=[jax.ShapeDtypeStruct((t, d), BF16),
                   jax.ShapeDtypeStruct((PEER_HEADS, N_KEYS, t), F32),
                   jax.ShapeDtypeStruct((PEER_HEADS, N_KEYS, t), F32)],
        compiler_params=_cparams(("parallel",)),
        name="peer_prep",
    )(h, gain.reshape(1, d), wq, keys)


def _oddeven_merge(lo, hi, r):
    step = r * 2
    if step < hi - lo:
        yield from _oddeven_merge(lo, hi, step)
        yield from _oddeven_merge(lo + r, hi, step)
        yield from [(i, i + r) for i in range(lo + r, hi - r, step)]
    else:
        yield (lo, lo + r)


def _oddeven_sort(lo, hi):
    if hi - lo >= 1:
        mid = lo + (hi - lo) // 2
        yield from _oddeven_sort(lo, mid)
        yield from _oddeven_sort(mid + 1, hi)
        yield from _oddeven_merge(lo, hi, 1)


SORT16 = tuple(_oddeven_sort(0, PEER_TOPK - 1))
BITONIC16 = tuple((i, i + h) for h in (8, 4, 2, 1) for i in range(PEER_TOPK) if i % (2 * h) < h)
SUBLANES = 8


def _compare_exchange(v, pairs):
    v = list(v)
    for i, j in pairs:
        v[i], v[j] = jnp.maximum(v[i], v[j]), jnp.minimum(v[i], v[j])
    return v


def _merge_sublane_lists(v):
    n = len(v)
    for shift in (4, 2, 1):
        w = [pltpu.roll(x, shift, 0) for x in v]
        v = _compare_exchange([jnp.maximum(v[k], w[n - 1 - k]) for k in range(n)], BITONIC16)
    return v


def _top16(x):
    slabs = [x[SUBLANES * k:SUBLANES * (k + 1), :] for k in range(PEER_TOPK)]
    return _merge_sublane_lists(_compare_exchange(slabs, SORT16))


def _peer_topk_kernel(s1_ref, s2_ref, cnt_ref, ar_ref, rk_ref, bt_ref):
    k = PEER_TOPK
    lanes = s1_ref.shape[2]
    sub = lax.broadcasted_iota(jnp.int32, (SUBLANES, lanes), 0)

    def head(hh, carry):
        s1 = s1_ref[hh]
        s2 = s2_ref[hh]
        va = _top16(s1)
        vb = _top16(s2)
        bd = vb[0]
        for p in range(1, SUBLANES):
            bd = jnp.where(sub == p, vb[p], bd)
        cand = []
        for i in range(k):
            c = va[i] + bd
            lim = k // (i + 1)
            cand.append(c if lim >= SUBLANES else jnp.where(sub < lim, c, -jnp.inf))
        tail = [va[0] + vb[SUBLANES + q] for q in range(k - SUBLANES)]
        top16 = _merge_sublane_lists(cand)
        tau = top16[k - SUBLANES - 1]
        for q in range(k - SUBLANES):
            tau = jnp.minimum(tau, jnp.maximum(top16[k - 1 - q], tail[q]))
        top = va[0] + vb[0]
        zc = jnp.zeros((SUBLANES, lanes), F32)
        for c in cand:
            zc = zc + jnp.where(c >= tau, jnp.exp(c - top), 0.0)
        zt = jnp.zeros((SUBLANES, lanes), F32)
        for c in tail:
            zt = zt + jnp.where(c >= tau, jnp.exp(c - top), 0.0)
        z = jnp.sum(zc, axis=0, keepdims=True) + zt[0:1, :]
        tau_row = tau[0:1, :]
        b_rows = [x[0:1, :] for x in vb]
        cnt = jnp.zeros(s1.shape, F32)
        rank2 = jnp.zeros(s2.shape, F32)
        for j in range(k):
            cnt = jnp.where(s1 + b_rows[j] >= tau_row, float(j + 1), cnt)
            rank2 = jnp.where(b_rows[j] > s2, float(j + 1), rank2)
        cnt_ref[hh] = cnt
        ar_ref[hh] = jnp.exp(s1 - va[0][0:1, :]) / z
        bt = jnp.exp(s2 - b_rows[0])
        for lb in range(lanes // LANES):
            ln = slice(lb * LANES, (lb + 1) * LANES)
            rk_ref[hh, lb] = pltpu.bitcast(rank2[:, ln].astype(BF16), jnp.uint32)
            bt_ref[hh, lb] = pltpu.bitcast(bt[:, ln].astype(BF16), jnp.uint32)
        return carry

    lax.fori_loop(0, PEER_HEADS, head, 0)


def peer_topk(s1t, s2t, *, tt):
    nh, nk, t = s1t.shape
    spec = pl.BlockSpec((nh, nk, tt), lambda i: (0, 0, i))
    shp = jax.ShapeDtypeStruct((nh, nk, t), F32)
    tspec = pl.BlockSpec((nh, tt // LANES, nk // 2, LANES), lambda i: (0, i, 0, 0))
    tshp = jax.ShapeDtypeStruct((nh, t // LANES, nk // 2, LANES), jnp.uint32)
    return pl.pallas_call(
        _peer_topk_kernel,
        grid=(t // tt,),
        in_specs=[spec, spec],
        out_specs=[spec, spec, tspec, tspec],
        out_shape=[shp, shp, tshp, tshp],
        compiler_params=_cparams(("parallel",)),
        name="peer_topk",
    )(s1t, s2t)


GELU_C0 = math.sqrt(2.0 / math.pi)
GELU_C1 = GELU_C0 * 0.044715


def _gelu(z):
    return 0.5 * z * (1.0 + jnp.tanh(z * (GELU_C0 + GELU_C1 * (z * z))))


def _peer_main_kernel(h_ref, xn_ref, u_ref, vt_ref, cnt_ref, ar_ref, rk_ref, bt_ref, o_ref,
                      acc_ref, zt_ref, a_ref, *, n_blocks, blocks_per_tile, lane_blocks_per_iter):
    g_step = pl.program_id(0)
    et, tt = zt_ref.shape
    n_e1 = et // N_KEYS
    jb = jnp.maximum(g_step - 1, 0) % blocks_per_tile

    @pl.when(g_step == 0)
    def _():
        a_ref[...] = jnp.zeros_like(a_ref)

    @pl.when(jb == 0)
    def _():
        acc_ref[...] = jnp.zeros_like(acc_ref)

    acc_ref[...] += jnp.dot(vt_ref[...], a_ref[...], preferred_element_type=F32)
    zt_ref[...] = lax.dot_general(u_ref[...].astype(BF16), xn_ref[...], NT_DIMS,
                                  preferred_element_type=F32)

    @pl.when((g_step >= 1) & (jb == blocks_per_tile - 1))
    def _():
        o_ref[...] = h_ref[...] + acc_ref[...].T

    def lane_group(li, carry):
        for sub in range(lane_blocks_per_iter):
            lb = li * lane_blocks_per_iter + sub
            ln = pl.ds(pl.multiple_of(lb * LANES, LANES), LANES)
            for c in range(n_e1):
                rows = slice(c * N_KEYS, (c + 1) * N_KEYS)
                g = jnp.zeros((N_KEYS, LANES), BF16)
                for hh in range(PEER_HEADS):
                    cnt = cnt_ref[hh, c:c + 1, ln].astype(BF16)
                    ar = ar_ref[hh, c:c + 1, ln].astype(BF16)
                    rk = pltpu.bitcast(rk_ref[hh, lb], BF16)
                    bt = pltpu.bitcast(bt_ref[hh, lb], BF16)
                    g = g + jnp.where(rk < cnt, bt * ar, 0)
                a_ref[rows, ln] = _gelu(zt_ref[rows, ln].astype(BF16)) * g
        return carry

    lax.fori_loop(0, tt // (LANES * lane_blocks_per_iter), lane_group, 0)


def _transpose_cast_kernel(x_ref, o_ref):
    o_ref[...] = x_ref[...].T.astype(o_ref.dtype)


def transpose_cast(stack, layer, *, rows):
    _, e, d = stack.shape
    return pl.pallas_call(
        _transpose_cast_kernel,
        grid=(e // rows,),
        in_specs=[pl.BlockSpec((None, rows, d), lambda j: (layer, j, 0))],
        out_specs=pl.BlockSpec((d, rows), lambda j: (0, j)),
        out_shape=jax.ShapeDtypeStruct((d, e), BF16),
        compiler_params=_cparams(("parallel",)),
        name="transpose_cast",
    )(stack)


def peer_main(h, xn, u_stack, layer, vt, cnt, ar, rank_tiles, bt_tiles, *, tt, et,
              lane_blocks_per_iter):
    t, d = h.shape
    ne = u_stack.shape[1]
    n_e1 = et // N_KEYS
    bpt = ne // et
    n_blocks = (t // tt) * bpt
    kern = functools.partial(_peer_main_kernel, n_blocks=n_blocks, blocks_per_tile=bpt,
                             lane_blocks_per_iter=lane_blocks_per_iter)

    def cur(g):
        return jnp.minimum(g, n_blocks - 1)

    def prev(g):
        return jnp.maximum(g - 1, 0)

    row_spec = pl.BlockSpec((PEER_HEADS, n_e1, tt), lambda g: (0, cur(g) % bpt, cur(g) // bpt))
    tok_spec = pl.BlockSpec((PEER_HEADS, tt // LANES, N_KEYS // 2, LANES),
                            lambda g: (0, cur(g) // bpt, 0, 0))
    return pl.pallas_call(
        kern,
        grid=(n_blocks + 1,),
        in_specs=[pl.BlockSpec((tt, d), lambda g: (prev(g) // bpt, 0)),
                  pl.BlockSpec((tt, d), lambda g: (cur(g) // bpt, 0)),
                  pl.BlockSpec((None, et, d), lambda g: (layer, cur(g) % bpt, 0)),
                  pl.BlockSpec((d, et), lambda g: (0, prev(g) % bpt)),
                  row_spec, row_spec, tok_spec, tok_spec],
        out_specs=pl.BlockSpec((tt, d), lambda g: (prev(g) // bpt, 0)),
        out_shape=jax.ShapeDtypeStruct((t, d), F32),
        scratch_shapes=[pltpu.VMEM((d, tt), F32), pltpu.VMEM((et, tt), F32),
                        pltpu.VMEM((et, tt), BF16)],
        compiler_params=_cparams(("arbitrary",)),
        name="peer_main",
    )(h, xn, u_stack, vt, cnt, ar, rank_tiles, bt_tiles)


def peer_layer(h, gain, wq, keys, u_stack, v_stack, layer):
    t = h.shape[0]
    xn, s1t, s2t = peer_prep(h, gain, wq.astype(BF16), keys, tm=min(1024, t))
    cnt, ar, rank_tiles, bt_tiles = peer_topk(s1t, s2t, tt=min(256, t))
    vt = transpose_cast(v_stack, layer, rows=512)
    return peer_main(h, xn, u_stack, layer, vt, cnt, ar, rank_tiles, bt_tiles,
                     tt=min(512, t), et=2048, lane_blocks_per_iter=1)


def _rope_tables(s):
    half = ROT_DIM // 2
    inv = ROPE_THETA ** (-jnp.arange(half, dtype=F32) * 2.0 / ROT_DIM)
    ang = jnp.arange(s).astype(F32)[:, None] * inv[None, :]
    cos, sin = jnp.cos(ang), jnp.sin(ang)
    pad = HEAD - ROT_DIM
    ones = jnp.ones((s, pad), F32)
    zeros = jnp.zeros((s, pad), F32)
    zh = jnp.zeros((s, half), F32)
    ct = jnp.concatenate([cos, cos, ones], axis=1)
    at = jnp.concatenate([-sin, zh, zeros], axis=1)
    bt = jnp.concatenate([zh, sin, zeros], axis=1)
    return ct, at, bt


ATTN_QG = KV_PER_GROUP * Q_PER_KV * HEAD
ATTN_KG = KV_PER_GROUP * HEAD


def _attn_proj_kernel(h_ref, kvg_ref, ag_ref, wkv_ref, wq_ref, kn_ref, qn_ref,
                      ct_ref, at_ref, bt_ref, *refs):
    out_refs, (qs_ref, kvs_ref) = refs[:-2], refs[-2:]
    y = _rms(h_ref[...])
    kv = jnp.dot((y * kvg_ref[...]).astype(BF16), wkv_ref[...], preferred_element_type=F32)
    qq = jnp.dot((y * ag_ref[...]).astype(BF16), wq_ref[...], preferred_element_type=F32)
    ct, at, bt = ct_ref[...], at_ref[...], bt_ref[...]
    half = ROT_DIM // 2

    def head_norm_rope(x, gain):
        n = _rms(x) * gain
        return n * ct + pltpu.roll(n, HEAD - half, 1) * at + pltpu.roll(n, half, 1) * bt

    n_kh = kv.shape[1] // (2 * HEAD)
    for hd in range(n_kh):
        cs = slice(hd * HEAD, (hd + 1) * HEAD)
        kvs_ref[hd] = head_norm_rope(kv[:, cs], kn_ref[...])
        kvs_ref[n_kh + hd] = kv[:, n_kh * HEAD + hd * HEAD:n_kh * HEAD + (hd + 1) * HEAD]
    for hd in range(qq.shape[1] // HEAD):
        qs_ref[hd] = head_norm_rope(qq[:, hd * HEAD:(hd + 1) * HEAD], qn_ref[...])

    tm = qs_ref.shape[1]
    q_heads, k_heads = ATTN_QG // HEAD, ATTN_KG // HEAD
    for g, (_, dil) in enumerate(DIL_GROUPS):
        q_ref, k_ref, v_ref = out_refs[3 * g:3 * g + 3]
        for r in range(dil):
            rows = pl.ds(r, tm // dil, stride=dil) if dil > 1 else slice(None)
            for hd in range(q_heads):
                q_ref[:, (r * q_heads + hd) * HEAD:(r * q_heads + hd + 1) * HEAD] = (
                    qs_ref[g * q_heads + hd, rows, :].astype(q_ref.dtype))
            for hd in range(k_heads):
                cols = slice((r * k_heads + hd) * HEAD, (r * k_heads + hd + 1) * HEAD)
                k_ref[:, cols] = kvs_ref[g * k_heads + hd, rows, :].astype(k_ref.dtype)
                v_ref[:, cols] = kvs_ref[n_kh + g * k_heads + hd, rows, :].astype(v_ref.dtype)


def attn_proj(h, kv_gain, a_gain, wkv, wq, k_norm, q_norm, seq, *, tm):
    t, d = h.shape
    ct, at, bt = _rope_tables(seq)
    ns = seq // tm
    tab = pl.BlockSpec((tm, HEAD), lambda i: (i % ns, 0))
    vec = lambda n: pl.BlockSpec((1, n), lambda i: (0, 0))
    out_specs, out_shape = [], []
    for _, dil in DIL_GROUPS:
        for width in (ATTN_QG, ATTN_KG, ATTN_KG):
            out_specs.append(pl.BlockSpec((tm // dil, dil * width), lambda i: (i, 0)))
            out_shape.append(jax.ShapeDtypeStruct((t // dil, dil * width), BF16))
    outs = pl.pallas_call(
        _attn_proj_kernel,
        grid=(t // tm,),
        in_specs=[pl.BlockSpec((tm, d), lambda i: (i, 0)), vec(d), vec(d),
                  pl.BlockSpec(wkv.shape, lambda i: (0, 0)),
                  pl.BlockSpec(wq.shape, lambda i: (0, 0)),
                  vec(HEAD), vec(HEAD), tab, tab, tab],
        out_specs=out_specs,
        out_shape=out_shape,
        scratch_shapes=[pltpu.VMEM((wq.shape[1] // HEAD, tm, HEAD), F32),
                        pltpu.VMEM((wkv.shape[1] // HEAD, tm, HEAD), F32)],
        compiler_params=_cparams(("parallel",)),
        name="attn_proj",
    )(h, kv_gain.reshape(1, d), a_gain.reshape(1, d), wkv, wq,
      k_norm.reshape(1, HEAD), q_norm.reshape(1, HEAD), ct, at, bt)
    return [tuple(outs[3 * g:3 * g + 3]) for g in range(len(DIL_GROUPS))]


ATTN_EXP2_SCALE = HEAD ** -0.5 * math.log2(math.e)
ATTN_SUB = 128


def _band_bias(first):
    rq = np.arange(Q_PER_KV * ATTN_SUB)[:, None] % ATTN_SUB
    ck = np.arange(2 * ATTN_SUB)[None, :]
    diff = rq - ck if first else rq + ATTN_SUB - ck
    return np.where((diff >= 0) & (diff <= ATTN_SUB), 0.0, -np.inf).astype(np.float32)


def _attn_group_kernel(q_ref, k_ref, v_ref, bfirst_ref, bband_ref, o_ref, lse_ref, *, qb):
    nb = pl.program_id(3)
    scale = HEAD ** -0.5
    for sb in range(qb // ATTN_SUB):
        rows = slice(sb * ATTN_SUB, (sb + 1) * ATTN_SUB)
        n0 = nb * qb + sb * ATTN_SUB
        ks = pl.multiple_of(jnp.maximum(n0 - ATTN_SUB, 0), ATTN_SUB)
        q2 = jnp.concatenate([q_ref[0, rows, r * HEAD:(r + 1) * HEAD] for r in range(Q_PER_KV)],
                             axis=0)
        s = lax.dot_general(q2, k_ref[0, pl.ds(ks, 2 * ATTN_SUB), :], NT_DIMS,
                            preferred_element_type=F32)
        s = s + jnp.where(n0 == 0, bfirst_ref[...], bband_ref[...])
        m = jnp.max(s, axis=-1, keepdims=True)
        p = jnp.exp2((s - m) * ATTN_EXP2_SCALE)
        den = jnp.sum(p, axis=-1, keepdims=True)
        out = jnp.dot(p.astype(BF16), v_ref[0, pl.ds(ks, 2 * ATTN_SUB), :],
                      preferred_element_type=F32) / den
        lse = m * scale + jnp.log(den)
        for r in range(Q_PER_KV):
            part = slice(r * ATTN_SUB, (r + 1) * ATTN_SUB)
            o_ref[0, rows, r * HEAD:(r + 1) * HEAD] = out[part].astype(o_ref.dtype)
            lse_ref[0, rows, r * HEAD:(r + 1) * HEAD] = jnp.broadcast_to(lse[part],
                                                                          (ATTN_SUB, HEAD))


def attn_group(q, k, v, g, batch, *, qb):
    w, dil = DIL_GROUPS[g]
    assert w == ATTN_SUB * dil
    sub = q.shape[0] // batch
    qb = min(qb, sub)
    qcols = Q_PER_KV * HEAD
    kern = functools.partial(_attn_group_kernel, qb=qb)
    const = lambda bi, r, kv, nb: (0, 0)
    q_map = lambda bi, r, kv, nb: (bi, nb, r * KV_PER_GROUP + kv)
    kv_map = lambda bi, r, kv, nb: (bi, 0, r * KV_PER_GROUP + kv)
    o, lse = pl.pallas_call(
        kern,
        grid=(batch, dil, KV_PER_GROUP, sub // qb),
        in_specs=[pl.BlockSpec((1, qb, qcols), q_map),
                  pl.BlockSpec((1, sub, HEAD), kv_map),
                  pl.BlockSpec((1, sub, HEAD), kv_map),
                  pl.BlockSpec((Q_PER_KV * ATTN_SUB, 2 * ATTN_SUB), const),
                  pl.BlockSpec((Q_PER_KV * ATTN_SUB, 2 * ATTN_SUB), const)],
        out_specs=[pl.BlockSpec((1, qb, qcols), q_map), pl.BlockSpec((1, qb, qcols), q_map)],
        out_shape=[jax.ShapeDtypeStruct((batch, sub, dil * ATTN_QG), BF16),
                   jax.ShapeDtypeStruct((batch, sub, dil * ATTN_QG), F32)],
        compiler_params=_cparams(("parallel", "parallel", "parallel", "arbitrary")),
        name="attn_group",
    )(q.reshape(batch, sub, -1), k.reshape(batch, sub, -1), v.reshape(batch, sub, -1),
      jnp.asarray(_band_bias(True)), jnp.asarray(_band_bias(False)))
    return o.reshape(batch * sub, -1), lse.reshape(batch * sub, -1)


def _combine_proj_kernel(o0_ref, o1_ref, o2_ref, l0_ref, l1_ref, l2_ref, w_ref, r_ref, out_ref,
                         osc_ref, lsc_ref):
    tm = out_ref.shape[0]
    ng = len(DIL_GROUPS)
    n_heads = ATTN_QG // HEAD
    for g, (o_ref, l_ref) in enumerate(zip((o0_ref, o1_ref, o2_ref), (l0_ref, l1_ref, l2_ref))):
        dil = DIL_GROUPS[g][1]
        for r in range(dil):
            rows = pl.ds(r, tm // dil, stride=dil) if dil > 1 else slice(None)
            for hd in range(n_heads):
                cols = slice((r * n_heads + hd) * HEAD, (r * n_heads + hd + 1) * HEAD)
                osc_ref[g * n_heads + hd, rows, :] = o_ref[:, cols].astype(F32)
                lsc_ref[g * n_heads + hd, rows, :] = l_ref[:, cols]
    heads = []
    for hd in range(n_heads):
        lses = [lsc_ref[g * n_heads + hd] for g in range(ng)]
        m = functools.reduce(jnp.maximum, lses)
        ws = [jnp.exp(x - m) for x in lses]
        num = functools.reduce(lambda a, b: a + b,
                               [ws[g] * osc_ref[g * n_heads + hd] for g in range(ng)])
        heads.append((num / functools.reduce(lambda a, b: a + b, ws)).astype(BF16))
    comb = jnp.concatenate(heads, axis=1)
    out_ref[...] = r_ref[...] + jnp.dot(comb, w_ref[...], preferred_element_type=F32)


def combine_proj(outs, lses, w, res, *, tm):
    t, n = res.shape
    k = w.shape[0]
    ng = len(DIL_GROUPS)
    packed = [pl.BlockSpec((tm // dil, dil * k), lambda i: (i, 0)) for _, dil in DIL_GROUPS]
    return pl.pallas_call(
        _combine_proj_kernel,
        grid=(t // tm,),
        in_specs=packed + packed + [pl.BlockSpec((k, n), lambda i: (0, 0)),
                                    pl.BlockSpec((tm, n), lambda i: (i, 0))],
        out_specs=pl.BlockSpec((tm, n), lambda i: (i, 0)),
        out_shape=jax.ShapeDtypeStruct((t, n), F32),
        scratch_shapes=[pltpu.VMEM((ng * k // HEAD, tm, HEAD), F32),
                        pltpu.VMEM((ng * k // HEAD, tm, HEAD), F32)],
        compiler_params=_cparams(("parallel",)),
        name="combine_proj",
    )(*outs, *lses, w, res)


def kernel(x, hgrn_norm, hgrn_w_in, hgrn_lb_logits, hgrn_out_norm, hgrn_w_out, kv_norm, w_kv,
           k_norm, attn_norm, w_q, q_norm, w_o, ffn_norm, peer_w_q, peer_sub_keys, peer_u, peer_v):
    b, s, d = x.shape
    t = b * s
    depth = ffn_norm.shape[0]
    n_a = hgrn_norm.shape[0]
    h = x.reshape(t, d)
    tm = min(1024, t)
    for layer in range(depth):
        if layer < n_a:
            o = hgrn_mixer(h.reshape(b, s, d), hgrn_norm[layer], hgrn_w_in[layer], hgrn_lb_logits,
                           hgrn_out_norm[layer], layer=layer, seq_block=min(2048, s))
            h = matmul_residual(o.reshape(t, d), hgrn_w_out[layer].astype(BF16), h, tm=tm)
        else:
            j = layer - n_a
            qkv = attn_proj(h, kv_norm, attn_norm[j], w_kv.astype(BF16), w_q[j].astype(BF16),
                            k_norm, q_norm[j], s, tm=min(256, s))
            if j == 0:
                kv_sh = [(kg, vg) for _, kg, vg in qkv]
            parts = [attn_group(qkv[g][0], kv_sh[g][0], kv_sh[g][1], g, b, qb=2048)
                     for g in range(len(DIL_GROUPS))]
            h = combine_proj([o for o, _ in parts], [l for _, l in parts],
                             w_o[j].astype(BF16), h, tm=min(512, t))
        h = peer_layer(h, ffn_norm[layer], peer_w_q[layer], peer_sub_keys[layer],
                       peer_u, peer_v, layer)
    return h.reshape(b, s, d)
```
